```python
import jax, jax.numpy as jnp
from jax import lax
import numpy as np

D_MODEL = 1024
BATCH = 16
SEQ = 256
DEPTH = 2
DEC_BATCH = 8
DEC_SEQ = 1024
PAST_LEN = 512

GRID_W = 64
N_EVEN = (DEPTH + 1) // 2
N_ODD = DEPTH // 2
POOL_WINDOWS = (2, 4, 8, 16)
POOL_GROUPS = 4
POOL_CH = 128
POOL_WIDTH = POOL_GROUPS * POOL_CH
MLA_HEADS = 8
QK_NOPE = 64
QK_ROPE = 32
QK_DIM = QK_NOPE + QK_ROPE
V_DIM = 64
Q_RANK = 384
KV_RANK = 256
MLA_WIDTH = MLA_HEADS * V_DIM
AB_IN = POOL_WIDTH + Q_RANK + KV_RANK + QK_ROPE
AB_OUT = POOL_WIDTH + MLA_WIDTH
CHUNK = 128
C_GROUPS = 8
C_WIDTH = D_MODEL
C_CH = C_WIDTH // C_GROUPS
D_FF = -(-8 * D_MODEL // (3 * 256)) * 256
ROPE_BASE = 10000.0
EPS = 1e-6
Q_BLOCK = 128

kernel_name = "hybrid_pool_mla_gmlp_diffusion_step"


def rms_norm(x, g):
    x32 = x.astype(jnp.float32)
    y = x32 * lax.rsqrt(jnp.mean(x32 * x32, axis=-1, keepdims=True) + EPS)
    return (y * g.astype(jnp.float32)).astype(x.dtype)


def ada_terms(cond, w, b):
    mod = (jax.nn.silu(cond) @ w + b)[:, None, :]
    return jnp.split(mod, 6, axis=-1)


def swiglu(h, wg, wu, wd):
    return (jax.nn.silu(h @ wg) * (h @ wu)) @ wd


def multi_scale_pool(u, w_pool, scale):
    B, L, _ = u.shape
    ug = u.reshape(B, L, POOL_GROUPS, POOL_CH).astype(jnp.float32)
    csum = jnp.concatenate([jnp.zeros((B, 1, POOL_GROUPS, POOL_CH), jnp.float32), jnp.cumsum(ug, axis=1)], axis=1)
    t = jnp.arange(L)
    outs = []
    for g, w in enumerate(POOL_WINDOWS):
        lo = jnp.clip(t - w // 2, 0, L)
        hi = jnp.clip(t + w // 2, 0, L)
        s = csum[:, hi, g] - csum[:, lo, g]
        outs.append(s / (hi - lo).astype(jnp.float32)[None, :, None] - ug[:, :, g])
    p = jnp.stack(outs, axis=2).astype(u.dtype)
    y = jnp.einsum('blgc,gcd->blgd', p, w_pool).reshape(B, L, POOL_WIDTH)
    return y * scale


def axial_rope_tables(L):
    rows = L // GRID_W
    row = jnp.repeat(jnp.arange(rows), GRID_W).astype(jnp.float32)
    col = jnp.tile(jnp.arange(GRID_W), rows).astype(jnp.float32)
    per_axis = QK_ROPE // 2
    inv = 1.0 / (ROPE_BASE ** (jnp.arange(0, per_axis, 2, dtype=jnp.float32) / per_axis))
    ang = jnp.concatenate([row[:, None] * inv, col[:, None] * inv], axis=-1)
    return jnp.cos(ang), jnp.sin(ang)


def rope_part(x, cos, sin):
    xn, xr = x[..., :QK_NOPE], x[..., QK_NOPE:]
    half = QK_ROPE // 2
    x1, x2 = xr[..., :half], xr[..., half:]
    cs, sn = cos[None, :, None, :], sin[None, :, None, :]
    rot = jnp.concatenate([x1 * cs - x2 * sn, x1 * sn + x2 * cs], axis=-1).astype(x.dtype)
    return jnp.concatenate([xn, rot], axis=-1)


def attend(q, k, v):
    B, Lq, H, Dk = q.shape
    nb = Lq // Q_BLOCK
    qb = q.reshape(B, nb, Q_BLOCK, H, Dk).transpose(1, 0, 2, 3, 4)
    scale = Dk ** -0.5

    def block(qi):
        s = jnp.einsum('bqhd,bkhd->bhqk', qi, k).astype(jnp.float32) * scale
        p = jax.nn.softmax(s, axis=-1).astype(v.dtype)
        return jnp.einsum('bhqk,bkhd->bqhd', p, v)

    o = lax.map(block, qb)
    return o.transpose(1, 0, 2, 3, 4).reshape(B, Lq, H * v.shape[-1])


def ab_split(h, w_in, kv_norm_g):
    z = h @ w_in
    o1 = POOL_WIDTH
    o2 = o1 + Q_RANK
    o3 = o2 + KV_RANK
    return z[..., :o1], z[..., o1:o2], rms_norm(z[..., o2:o3], kv_norm_g), z[..., o3:]


def mla_q(cq_raw, q_norm_g, w_uq, qn_g):
    B, L, _ = cq_raw.shape
    q = (rms_norm(cq_raw, q_norm_g) @ w_uq).reshape(B, L, MLA_HEADS, QK_DIM)
    return rms_norm(q, qn_g)


def mla_kv(ckv, krope, w_ukv, kn_g):
    B, L, _ = ckv.shape
    kv = (ckv @ w_ukv).reshape(B, L, MLA_HEADS, QK_NOPE + V_DIM)
    k_nope, v = kv[..., :QK_NOPE], kv[..., QK_NOPE:]
    k_pe = jnp.broadcast_to(krope[:, :, None, :], (B, L, MLA_HEADS, QK_ROPE)).astype(k_nope.dtype)
    k = rms_norm(jnp.concatenate([k_nope, k_pe], axis=-1), kn_g)
    return k, v


def mixer_ab_context(h, w_in, pool_w, pool_scale, q_norm_g, kv_norm_g, w_uq, w_ukv, qn_g, kn_g, w_out):
    u, cq_raw, ckv, krope = ab_split(h, w_in, kv_norm_g)
    y_pool = multi_scale_pool(u, pool_w, pool_scale)
    q = mla_q(cq_raw, q_norm_g, w_uq, qn_g)
    k, v = mla_kv(ckv, krope, w_ukv, kn_g)
    y_att = attend(q, k, v)
    return jnp.concatenate([y_pool, y_att], axis=-1) @ w_out, ckv, krope


def mixer_ab_latent(h, ckv_ctx, krope_ctx, w_in, pool_w, pool_scale, q_norm_g, kv_norm_g, w_uq, w_ukv, qn_g, kn_g, w_out):
    L = h.shape[1]
    cos, sin = axial_rope_tables(L)
    u, cq_raw, ckv, krope = ab_split(h, w_in, kv_norm_g)
    y_pool = multi_scale_pool(u, pool_w, pool_scale)
    q = rope_part(mla_q(cq_raw, q_norm_g, w_uq, qn_g), cos, sin)
    k_lat, v_lat = mla_kv(ckv, krope, w_ukv, kn_g)
    k_lat = rope_part(k_lat, cos, sin)
    k_ctx, v_ctx = mla_kv(ckv_ctx.astype(h.dtype), krope_ctx.astype(h.dtype), w_ukv, kn_g)
    k = jnp.concatenate([k_ctx, k_lat], axis=1)
    v = jnp.concatenate([v_ctx, v_lat], axis=1)
    y_att = attend(q, k, v)
    return jnp.concatenate([y_pool, y_att], axis=-1) @ w_out


def chunk_gmlp(h, w_in, vnorm_g, w_s, b_s, w_out):
    B, L, _ = h.shape
    z = h @ w_in
    u, v = z[..., :C_WIDTH], z[..., C_WIDTH:]
    v = rms_norm(v, vnorm_g)
    n = L // CHUNK
    vc = v.reshape(B, n, CHUNK, C_GROUPS, C_CH)
    s = jnp.einsum('gpq,bnqgc->bnpgc', w_s, vc) + b_s.T[None, None, :, :, None]
    return (u * s.reshape(B, L, C_WIDTH)) @ w_out


def setup_inputs(seed: int = 0) -> dict:
    key = jax.random.key(seed)
    ks = iter(jax.random.split(key, 40))
    nrm = lambda shape, s: jax.random.normal(next(ks), shape, jnp.float32) * s
    gain = lambda shape: 1.0 + 0.02 * jax.random.normal(next(ks), shape, jnp.float32)
    return {
        "x_prompt": nrm((BATCH, SEQ, D_MODEL), 1.0),
        "x_sample": nrm((DEC_BATCH, DEC_SEQ, D_MODEL), 1.0),
        "cache_ckv": nrm((DEC_BATCH, N_EVEN, PAST_LEN, KV_RANK), 1.0),
        "cache_krope": nrm((DEC_BATCH, N_EVEN, PAST_LEN, QK_ROPE), 1.0),
        "c": nrm((DEC_BATCH, D_MODEL), 1.0),
        "c_ctx": nrm((D_MODEL,), 1.0),
        "ada_w": nrm((DEPTH, D_MODEL, 6 * D_MODEL), 0.5 * D_MODEL ** -0.5),
        "ada_b": nrm((DEPTH, 6 * D_MODEL), 0.02),
        "norm_mix_g": gain((DEPTH, D_MODEL)),
        "norm_ffn_g": gain((DEPTH, D_MODEL)),
        "ffn_wg": nrm((DEPTH, D_MODEL, D_FF), D_MODEL ** -0.5),
        "ffn_wu": nrm((DEPTH, D_MODEL, D_FF), D_MODEL ** -0.5),
        "ffn_wd": nrm((DEPTH, D_FF, D_MODEL), D_FF ** -0.5),
        "ab_w_in": nrm((N_EVEN, D_MODEL, AB_IN), D_MODEL ** -0.5),
        "pool_w": nrm((N_EVEN, POOL_GROUPS, POOL_CH, POOL_CH), POOL_CH ** -0.5),
        "pool_scale": gain((N_EVEN, POOL_WIDTH)),
        "q_norm_g": gain((N_EVEN, Q_RANK)),
        "kv_norm_g": gain((N_EVEN, KV_RANK)),
        "w_uq": nrm((N_EVEN, Q_RANK, MLA_HEADS * QK_DIM), Q_RANK ** -0.5),
        "w_ukv": nrm((N_EVEN, KV_RANK, MLA_HEADS * (QK_NOPE + V_DIM)), KV_RANK ** -0.5),
        "qn_g": gain((N_EVEN, QK_DIM)),
        "kn_g": gain((N_EVEN, QK_DIM)),
        "ab_w_out": nrm((N_EVEN, AB_OUT, D_MODEL), AB_OUT ** -0.5),
        "gm_w_in": nrm((N_ODD, D_MODEL, 2 * C_WIDTH), D_MODEL ** -0.5),
        "gm_vnorm_g": gain((N_ODD, C_WIDTH)),
        "gm_ws": nrm((N_ODD, C_GROUPS, CHUNK, CHUNK), CHUNK ** -0.5),
        "gm_bs": nrm((N_ODD, C_GROUPS, CHUNK), 0.02),
        "gm_w_out": nrm((N_ODD, C_WIDTH, D_MODEL), C_WIDTH ** -0.5),
    }


def reference(x_prompt, x_sample, cache_ckv, cache_krope, c, c_ctx, ada_w, ada_b, norm_mix_g, norm_ffn_g, ffn_wg, ffn_wu, ffn_wd, ab_w_in, pool_w, pool_scale, q_norm_g, kv_norm_g, w_uq, w_ukv, qn_g, kn_g, ab_w_out, gm_w_in, gm_vnorm_g, gm_ws, gm_bs, gm_w_out):
    ctx = x_prompt
    lat = x_sample
    new_ckv, new_krope = [], []
    for i in range(DEPTH):
        csh1, csc1, cg1, csh2, csc2, cg2 = ada_terms(c_ctx[None, :], ada_w[i], ada_b[i])
        lsh1, lsc1, lg1, lsh2, lsc2, lg2 = ada_terms(c, ada_w[i], ada_b[i])
        h_ctx = rms_norm(ctx, norm_mix_g[i]) * (1 + csc1) + csh1
        h_lat = rms_norm(lat, norm_mix_g[i]) * (1 + lsc1) + lsh1
        if i % 2 == 0:
            e = i // 2
            prm = (ab_w_in[e], pool_w[e], pool_scale[e], q_norm_g[e], kv_norm_g[e], w_uq[e], w_ukv[e], qn_g[e], kn_g[e], ab_w_out[e])
            y_ctx, ckv, krope = mixer_ab_context(h_ctx, *prm)
            y_lat = mixer_ab_latent(h_lat, cache_ckv[:, e], cache_krope[:, e], *prm)
            new_ckv.append(ckv)
            new_krope.append(krope)
        else:
            o = i // 2
            prm = (gm_w_in[o], gm_vnorm_g[o], gm_ws[o], gm_bs[o], gm_w_out[o])
            y_ctx = chunk_gmlp(h_ctx, *prm)
            y_lat = chunk_gmlp(h_lat, *prm)
        ctx = ctx + cg1 * y_ctx
        lat = lat + lg1 * y_lat
        h_ctx = rms_norm(ctx, norm_ffn_g[i]) * (1 + csc2) + csh2
        h_lat = rms_norm(lat, norm_ffn_g[i]) * (1 + lsc2) + lsh2
        ctx = ctx + cg2 * swiglu(h_ctx, ffn_wg[i], ffn_wu[i], ffn_wd[i])
        lat = lat + lg2 * swiglu(h_lat, ffn_wg[i], ffn_wu[i], ffn_wd[i])
    state_ckv = jnp.stack(new_ckv, axis=1)
    state_krope = jnp.stack(new_krope, axis=1)
    return (ctx, lat, state_ckv, state_krope)
```

```python
import functools
import math

import jax
import jax.numpy as jnp
from jax import lax
from jax.experimental import pallas as pl
from jax.experimental.pallas import tpu as pltpu

D_MODEL = 1024
DEPTH = 2
GRID_W = 64
POOL_WINDOWS = (2, 4, 8, 16)
POOL_GROUPS = 4
POOL_CH = 128
POOL_WIDTH = POOL_GROUPS * POOL_CH
MLA_HEADS = 8
QK_NOPE = 64
QK_ROPE = 32
QK_DIM = QK_NOPE + QK_ROPE
V_DIM = 64
Q_RANK = 384
KV_RANK = 256
MLA_WIDTH = MLA_HEADS * V_DIM
CHUNK = 128
C_GROUPS = 8
C_WIDTH = D_MODEL
C_CH = C_WIDTH // C_GROUPS
D_FF = 2816
ROPE_BASE = 10000.0
EPS = 1e-6

LANES = 128
HEAD_W = MLA_HEADS * LANES
AB_IN_PAD = POOL_WIDTH + Q_RANK + KV_RANK + LANES
MOD_ROWS = 16
VMEM_LIMIT = 56 * 1024 * 1024

F32 = jnp.float32
BF16 = jnp.bfloat16


def _rms(x):
    return x * lax.rsqrt(jnp.mean(x * x, axis=-1, keepdims=True) + EPS)


def _dot(a, b):
    return jnp.dot(a, b, preferred_element_type=F32)


def _dot_nt(a, b):
    return lax.dot_general(a, b, (((1,), (1,)), ((), ())), preferred_element_type=F32)


def _silu(x):
    return x * jax.nn.sigmoid(x)


def _params(n_axes):
    return pltpu.CompilerParams(
        dimension_semantics=("arbitrary",) * n_axes, vmem_limit_bytes=VMEM_LIMIT)


def _const_spec(shape):
    zeros = (0,) * len(shape)
    return pl.BlockSpec(shape, lambda *_: zeros)


ADA_TN = 512


def _ada_kernel(cond_ref, w_ref, b_ref, o_ref):
    s = _silu(cond_ref[...]).astype(BF16)
    o_ref[0] = _dot(s, w_ref[0].astype(BF16)) + b_ref[0]


def _ada_call(cond, ada_w, ada_b):
    n = 6 * D_MODEL
    return pl.pallas_call(
        _ada_kernel,
        grid=(DEPTH, n // ADA_TN),
        in_specs=[
            pl.BlockSpec((MOD_ROWS, D_MODEL), lambda i, j: (0, 0)),
            pl.BlockSpec((1, D_MODEL, ADA_TN), lambda i, j: (i, 0, j)),
            pl.BlockSpec((1, 1, ADA_TN), lambda i, j: (i, 0, j)),
        ],
        out_specs=pl.BlockSpec((1, MOD_ROWS, ADA_TN), lambda i, j: (i, 0, j)),
        out_shape=jax.ShapeDtypeStruct((DEPTH, MOD_ROWS, n), F32),
        compiler_params=_params(2),
        name="ada_mod",
    )(cond, ada_w, ada_b.reshape(DEPTH, 1, n))


def _rope(t, cs_ref):
    return (t * cs_ref[0]
            + pltpu.roll(t, LANES - QK_ROPE // 2, axis=1) * cs_ref[1]
            + pltpu.roll(t, QK_ROPE // 2, axis=1) * cs_ref[2])


def _kv_heads(ckv_bf, kr, wk_ref, wv_ref, kng_ref, cs_ref, k_ref, v_ref):
    v_ref[...] = _dot(ckv_bf, wv_ref[...]).astype(BF16)
    kpre = _dot(ckv_bf, wk_ref[...])
    kg = kng_ref[...]
    krg = kr * kg
    if cs_ref is not None:
        krg = _rope(krg, cs_ref)
    kr_ss = jnp.sum(kr * kr, axis=-1, keepdims=True)
    for h in range(MLA_HEADS):
        kh = kpre[:, h * LANES:(h + 1) * LANES]
        r = lax.rsqrt((jnp.sum(kh * kh, axis=-1, keepdims=True) + kr_ss) * (1.0 / QK_DIM) + EPS)
        k_ref[:, h * LANES:(h + 1) * LANES] = ((kh * kg + krg) * r).astype(BF16)


def _shift_rows(a, k, pos, seq_len):
    n = a.shape[0]
    r = pltpu.roll(a, k % n, axis=0)
    src = pos - k
    ok = (src >= 0) if k > 0 else (src < seq_len)
    return jnp.where(ok, r, 0.0)


def _pool_group(ug, half, pos, seq_len):
    fw = ug
    step = 1
    while step < half:
        fw = fw + _shift_rows(fw, -step, pos, seq_len)
        step *= 2
    bk = _shift_rows(ug, 1, pos, seq_len)
    step = 1
    while step < half:
        bk = bk + _shift_rows(bk, step, pos, seq_len)
        step *= 2
    cnt = jnp.minimum(pos + half, seq_len) - jnp.maximum(pos - half, 0)
    return (fw + bk) / cnt.astype(F32) - ug


def _l0a_kernel(*refs, seq_len, use_rope):
    (x_ref, mod_ref, ng_ref, win_ref, qg_ref, kvg_ref, wuq_ref, wk_ref, wv_ref,
     qng_ref, kng_ref, pw_ref, ps_ref) = refs[:13]
    if use_rope:
        cs_ref = refs[13]
        outs = refs[14:]
    else:
        cs_ref = None
        outs = refs[13:]
    yp_ref, q_ref, k_ref, v_ref, ckv_ref, kr_ref = outs

    tm = x_ref.shape[0]
    m = mod_ref[0]
    h = _rms(x_ref[...]) * ng_ref[...] * (1.0 + m[1:2]) + m[0:1]
    z = _dot(h.astype(BF16), win_ref[...])
    o1, o2, o3 = POOL_WIDTH, POOL_WIDTH + Q_RANK, POOL_WIDTH + Q_RANK + KV_RANK

    ckv = _rms(z[:, o2:o3]) * kvg_ref[...]
    kr = z[:, o3:]
    ckv_ref[...] = ckv
    kr_ref[...] = kr[:, QK_NOPE:QK_DIM]
    _kv_heads(ckv.astype(BF16), kr, wk_ref, wv_ref, kng_ref, cs_ref, k_ref, v_ref)

    cqn = (_rms(z[:, o1:o2]) * qg_ref[...]).astype(BF16)
    qf = _dot(cqn, wuq_ref[...])
    qg = qng_ref[...]
    for hd in range(MLA_HEADS):
        qh = qf[:, hd * LANES:(hd + 1) * LANES]
        r = lax.rsqrt(jnp.sum(qh * qh, axis=-1, keepdims=True) * (1.0 / QK_DIM) + EPS)
        qh = qh * r * qg
        if use_rope:
            qh = _rope(qh, cs_ref)
        q_ref[:, hd * LANES:(hd + 1) * LANES] = qh.astype(BF16)

    pos = lax.broadcasted_iota(jnp.int32, (tm, POOL_CH), 0) & (seq_len - 1)
    for g, w in enumerate(POOL_WINDOWS):
        ug = z[:, g * POOL_CH:(g + 1) * POOL_CH]
        p = _pool_group(ug, w // 2, pos, seq_len)
        y = _dot(p.astype(BF16), pw_ref[g]) * ps_ref[:, g * POOL_CH:(g + 1) * POOL_CH]
        yp_ref[:, g * POOL_CH:(g + 1) * POOL_CH] = y.astype(BF16)


def _l0a_call(x, mod, mod_row, tm, seq_len, w, cs):
    n_tok = x.shape[0]
    use_rope = cs is not None
    row_spec = lambda width: pl.BlockSpec((tm, width), lambda t: (t, 0))
    in_specs = [
        row_spec(D_MODEL),
        pl.BlockSpec((1, 6, D_MODEL), lambda t: (mod_row(t), 0, 0)),
        _const_spec((1, D_MODEL)),
        _const_spec((D_MODEL, AB_IN_PAD)),
        _const_spec((1, Q_RANK)),
        _const_spec((1, KV_RANK)),
        _const_spec((Q_RANK, HEAD_W)),
        _const_spec((KV_RANK, HEAD_W)),
        _const_spec((KV_RANK, HEAD_W)),
        _const_spec((1, LANES)),
        _const_spec((1, LANES)),
        _const_spec((POOL_GROUPS, POOL_CH, POOL_CH)),
        _const_spec((1, POOL_WIDTH)),
    ]
    args = [x, mod, w["norm_mix_g"], w["w_in"], w["q_norm_g"], w["kv_norm_g"], w["w_uq"],
            w["w_k"], w["w_v"], w["qn_g"], w["kn_g"], w["pool_w"], w["pool_scale"]]
    if use_rope:
        assert tm == seq_len
        in_specs.append(_const_spec((3, seq_len, LANES)))
        args.append(cs)
    out_shape = [
        jax.ShapeDtypeStruct((n_tok, POOL_WIDTH), BF16),
        jax.ShapeDtypeStruct((n_tok, HEAD_W), BF16),
        jax.ShapeDtypeStruct((n_tok, HEAD_W), BF16),
        jax.ShapeDtypeStruct((n_tok, HEAD_W), BF16),
        jax.ShapeDtypeStruct((n_tok, KV_RANK), F32),
        jax.ShapeDtypeStruct((n_tok, QK_ROPE), F32),
    ]
    out_specs = [row_spec(POOL_WIDTH), row_spec(HEAD_W), row_spec(HEAD_W), row_spec(HEAD_W),
                 row_spec(KV_RANK), row_spec(QK_ROPE)]
    return pl.pallas_call(
        functools.partial(_l0a_kernel, seq_len=seq_len, use_rope=use_rope),
        grid=(n_tok // tm,),
        in_specs=in_specs,
        out_specs=out_specs,
        out_shape=out_shape,
        compiler_params=_params(1),
        name="l0_front_rope" if use_rope else "l0_front",
    )(*args)


def _kvc_kernel(ckv_ref, kr_ref, wk_ref, wv_ref, kng_ref, k_ref, v_ref):
    _kv_heads(ckv_ref[...].astype(BF16), kr_ref[...], wk_ref, wv_ref, kng_ref, None, k_ref, v_ref)


def _kvc_call(ckv, kr_pad, w, tm):
    n_tok = ckv.shape[0]
    row_spec = lambda width: pl.BlockSpec((tm, width), lambda t: (t, 0))
    return pl.pallas_call(
        _kvc_kernel,
        grid=(n_tok // tm,),
        in_specs=[row_spec(KV_RANK), row_spec(LANES), _const_spec((KV_RANK, HEAD_W)),
                  _const_spec((KV_RANK, HEAD_W)), _const_spec((1, LANES))],
        out_specs=[row_spec(HEAD_W), row_spec(HEAD_W)],
        out_shape=[jax.ShapeDtypeStruct((n_tok, HEAD_W), BF16)] * 2,
        compiler_params=_params(1),
        name="cache_kv",
    )(ckv, kr_pad, w["w_k"], w["w_v"], w["kn_g"])


ATT_TQ = 256


def _att_kernel(*refs, has_cache):
    if has_cache:
        (x_ref, mod_ref, yp_ref, q_ref, k_ref, v_ref, kc_ref, vc_ref,
         wop_ref, woa_ref, o_ref, ya_ref) = refs
    else:
        x_ref, mod_ref, yp_ref, q_ref, k_ref, v_ref, wop_ref, woa_ref, o_ref, ya_ref = refs
        kc_ref = vc_ref = None

    for j in range(MLA_HEADS // 2):
        pair = None
        for hd in (2 * j, 2 * j + 1):
            sl = slice(hd * LANES, (hd + 1) * LANES)
            qh = q_ref[:, sl]
            s = _dot_nt(qh, k_ref[:, sl])
            mx = jnp.max(s, axis=-1, keepdims=True)
            if has_cache:
                sc = _dot_nt(qh, kc_ref[:, sl])
                mx = jnp.maximum(mx, jnp.max(sc, axis=-1, keepdims=True))
            p = jnp.exp2(s - mx)
            den = jnp.sum(p, axis=-1, keepdims=True)
            o = _dot(p.astype(BF16), v_ref[:, sl])
            if has_cache:
                pc = jnp.exp2(sc - mx)
                den = den + jnp.sum(pc, axis=-1, keepdims=True)
                o = o + _dot(pc.astype(BF16), vc_ref[:, sl])
            o = o / den
            pair = o if pair is None else pair + o
        ya_ref[:, j * LANES:(j + 1) * LANES] = pair.astype(BF16)

    y = _dot(yp_ref[...], wop_ref[...]) + _dot(ya_ref[...], woa_ref[...])
    o_ref[...] = x_ref[...] + mod_ref[0][2:3] * y


def _att_call(x, mod, mod_row, seq_len, yp, q, k, v, cache, w):
    n_tok = x.shape[0]
    n_b = n_tok // seq_len
    n_q = seq_len // ATT_TQ
    qrow = lambda width: pl.BlockSpec((ATT_TQ, width), lambda b, i: (b * n_q + i, 0))
    brow = lambda rows, width: pl.BlockSpec((rows, width), lambda b, i: (b, 0))
    in_specs = [
        qrow(D_MODEL),
        pl.BlockSpec((1, 6, D_MODEL), lambda b, i: (mod_row(b), 0, 0)),
        qrow(POOL_WIDTH), qrow(HEAD_W), brow(seq_len, HEAD_W), brow(seq_len, HEAD_W),
    ]
    args = [x, mod, yp, q, k, v]
    if cache is not None:
        kc, vc = cache
        past = kc.shape[0] // n_b
        in_specs += [brow(past, HEAD_W), brow(past, HEAD_W)]
        args += [kc, vc]
    in_specs += [_const_spec((POOL_WIDTH, D_MODEL)), _const_spec((MLA_WIDTH, D_MODEL))]
    args += [w["w_out_pool"], w["w_out_att"]]
    return pl.pallas_call(
        functools.partial(_att_kernel, has_cache=cache is not None),
        grid=(n_b, n_q),
        in_specs=in_specs,
        out_specs=qrow(D_MODEL),
        out_shape=jax.ShapeDtypeStruct((n_tok, D_MODEL), F32),
        scratch_shapes=[pltpu.VMEM((ATT_TQ, MLA_WIDTH), BF16)],
        compiler_params=_params(2),
        name="att_out_cache" if cache is not None else "att_out",
    )(*args)


FFN_TM = 512
FFN_CHUNKS = 2


def _ffn_kernel(x_ref, mod_ref, ng_ref, wg_ref, wu_ref, wd_ref, o_ref):
    x = x_ref[...]
    m = mod_ref[0]
    h = (_rms(x) * ng_ref[...] * (1.0 + m[4:5]) + m[3:4]).astype(BF16)
    fc = D_FF // FFN_CHUNKS
    acc = None
    for c in range(FFN_CHUNKS):
        g = _dot(h, wg_ref[:, c * fc:(c + 1) * fc])
        u = _dot(h, wu_ref[:, c * fc:(c + 1) * fc])
        d = _dot((_silu(g) * u).astype(BF16), wd_ref[c * fc:(c + 1) * fc, :])
        acc = d if acc is None else acc + d
    o_ref[...] = x + m[5:6] * acc


def _ffn_call(x, mod, mod_row, tm, ng, wg, wu, wd):
    n_tok = x.shape[0]
    resident = lambda shape: pl.BlockSpec(shape, lambda t: (0, 0), pipeline_mode=pl.Buffered(1))
    return pl.pallas_call(
        _ffn_kernel,
        grid=(n_tok // tm,),
        in_specs=[
            pl.BlockSpec((tm, D_MODEL), lambda t: (t, 0)),
            pl.BlockSpec((1, 6, D_MODEL), lambda t: (mod_row(t), 0, 0)),
            _const_spec((1, D_MODEL)),
            resident((D_MODEL, D_FF)), resident((D_MODEL, D_FF)), resident((D_FF, D_MODEL)),
        ],
        out_specs=pl.BlockSpec((tm, D_MODEL), lambda t: (t, 0)),
        out_shape=jax.ShapeDtypeStruct((n_tok, D_MODEL), F32),
        compiler_params=_params(1),
        name="ffn",
    )(x, mod, ng, wg, wu, wd)


GM_TM = 512


def _gmlp_kernel(x_ref, mod_ref, ng_ref, win_ref, vg_ref, ws_ref, bs_ref, wout_ref, o_ref, gs_ref):
    x = x_ref[...]
    m = mod_ref[0]
    h = (_rms(x) * ng_ref[...] * (1.0 + m[1:2]) + m[0:1]).astype(BF16)
    z = _dot(h, win_ref[...])
    vn = (_rms(z[:, C_WIDTH:]) * vg_ref[...]).astype(BF16)
    n_chunks = x.shape[0] // CHUNK
    for g in range(C_GROUPS):
        cols = slice(g * C_CH, (g + 1) * C_CH)
        rhs = jnp.concatenate(
            [vn[n * CHUNK:(n + 1) * CHUNK, cols] for n in range(n_chunks)], axis=1)
        s = _dot(ws_ref[g], rhs)
        for n in range(n_chunks):
            rows = slice(n * CHUNK, (n + 1) * CHUNK)
            sn = s[:, n * C_CH:(n + 1) * C_CH] + bs_ref[g]
            gs_ref[rows, cols] = (z[rows, cols] * sn).astype(BF16)
    o_ref[...] = x + m[2:3] * _dot(gs_ref[...], wout_ref[...])


def _gmlp_call(x, mod, mod_row, tm, w):
    n_tok = x.shape[0]
    return pl.pallas_call(
        _gmlp_kernel,
        grid=(n_tok // tm,),
        in_specs=[
            pl.BlockSpec((tm, D_MODEL), lambda t: (t, 0)),
            pl.BlockSpec((1, 6, D_MODEL), lambda t: (mod_row(t), 0, 0)),
            _const_spec((1, D_MODEL)),
            _const_spec((D_MODEL, 2 * C_WIDTH)),
            _const_spec((1, C_WIDTH)),
            _const_spec((C_GROUPS, CHUNK, CHUNK)),
            _const_spec((C_GROUPS, CHUNK, C_CH)),
            _const_spec((C_WIDTH, D_MODEL)),
        ],
        out_specs=pl.BlockSpec((tm, D_MODEL), lambda t: (t, 0)),
        out_shape=jax.ShapeDtypeStruct((n_tok, D_MODEL), F32),
        scratch_shapes=[pltpu.VMEM((tm, C_WIDTH), BF16)],
        compiler_params=_params(1),
        name="gmlp",
    )(x, mod, w["norm_mix_g"], w["w_in"], w["vnorm_g"], w["w_s"], w["b_s"], w["w_out"])


def _head_pad(wm, width):
    rows = wm.shape[0]
    wm = wm.reshape(rows, MLA_HEADS, width)
    wm = jnp.pad(wm, ((0, 0), (0, 0), (0, LANES - width)))
    return wm.reshape(rows, HEAD_W)


def _lane_gain(g, extra):
    return jnp.pad(g * extra, (0, LANES - QK_DIM)).reshape(1, LANES)


def _layer0_weights(e, norm_mix_g, ab_w_in, pool_w, pool_scale, q_norm_g, kv_norm_g, w_uq, w_ukv,
                    qn_g, kn_g, ab_w_out):
    w_in = ab_w_in[e]
    o3 = POOL_WIDTH + Q_RANK + KV_RANK
    kr_cols = jnp.pad(w_in[:, o3:], ((0, 0), (QK_NOPE, LANES - QK_DIM)))
    w_in_p = jnp.concatenate([w_in[:, :o3], kr_cols], axis=1).astype(BF16)
    ukv = w_ukv[e].reshape(KV_RANK, MLA_HEADS, QK_NOPE + V_DIM)
    w_k = jnp.pad(ukv[:, :, :QK_NOPE], ((0, 0), (0, 0), (0, LANES - QK_NOPE)))
    w_v = ukv[:, :, QK_NOPE:].reshape(KV_RANK, MLA_HEADS // 2, 2, V_DIM)
    zero = jnp.zeros_like(w_v[:, :, 0])
    w_v = jnp.stack([jnp.concatenate([w_v[:, :, 0], zero], axis=-1),
                     jnp.concatenate([zero, w_v[:, :, 1]], axis=-1)], axis=2)
    score_scale = QK_DIM ** -0.5 * math.log2(math.e)
    return {
        "norm_mix_g": norm_mix_g.reshape(1, D_MODEL),
        "w_in": w_in_p,
        "q_norm_g": q_norm_g[e].reshape(1, Q_RANK),
        "kv_norm_g": kv_norm_g[e].reshape(1, KV_RANK),
        "w_uq": _head_pad(w_uq[e], QK_DIM).astype(BF16),
        "w_k": w_k.reshape(KV_RANK, HEAD_W).astype(BF16),
        "w_v": w_v.reshape(KV_RANK, HEAD_W).astype(BF16),
        "qn_g": _lane_gain(qn_g[e], score_scale),
        "kn_g": _lane_gain(kn_g[e], 1.0),
        "pool_w": pool_w[e].astype(BF16),
        "pool_scale": pool_scale[e].reshape(1, POOL_WIDTH),
        "w_out_pool": ab_w_out[e, :POOL_WIDTH].astype(BF16),
        "w_out_att": ab_w_out[e, POOL_WIDTH:].astype(BF16),
    }


def _rope_tables(seq_len):
    rows = seq_len // GRID_W
    row = jnp.repeat(jnp.arange(rows), GRID_W).astype(F32)
    col = jnp.tile(jnp.arange(GRID_W), rows).astype(F32)
    per_axis = QK_ROPE // 2
    inv = 1.0 / (ROPE_BASE ** (jnp.arange(0, per_axis, 2, dtype=F32) / per_axis))
    ang = jnp.concatenate([row[:, None] * inv, col[:, None] * inv], axis=-1)
    cos, sin = jnp.cos(ang), jnp.sin(ang)
    half = QK_ROPE // 2
    ones = jnp.ones((seq_len, QK_NOPE), F32)
    z_nope = jnp.zeros((seq_len, QK_NOPE), F32)
    z_half = jnp.zeros((seq_len, half), F32)
    z_tail = jnp.zeros((seq_len, LANES - QK_DIM), F32)
    c_tab = jnp.concatenate([ones, cos, cos, z_tail], axis=-1)
    a_tab = jnp.concatenate([z_nope, -sin, z_half, z_tail], axis=-1)
    b_tab = jnp.concatenate([z_nope, z_half, sin, z_tail], axis=-1)
    return jnp.stack([c_tab, a_tab, b_tab])


def kernel(x_prompt, x_sample, cache_ckv, cache_krope, c, c_ctx, ada_w, ada_b, norm_mix_g, norm_ffn_g, ffn_wg, ffn_wu, ffn_wd, ab_w_in, pool_w, pool_scale, q_norm_g, kv_norm_g, w_uq, w_ukv, qn_g, kn_g, ab_w_out, gm_w_in, gm_vnorm_g, gm_ws, gm_bs, gm_w_out):
    n_ctx_b, ctx_len, _ = x_prompt.shape
    n_lat_b, lat_len, _ = x_sample.shape
    past = cache_ckv.shape[2]
    assert ctx_len & (ctx_len - 1) == 0 and lat_len & (lat_len - 1) == 0
    assert MOD_ROWS >= 1 + n_lat_b

    ctx = x_prompt.reshape(n_ctx_b * ctx_len, D_MODEL)
    lat = x_sample.reshape(n_lat_b * lat_len, D_MODEL)

    cond = jnp.concatenate(
        [c_ctx[None, :], c, jnp.zeros((MOD_ROWS - 1 - n_lat_b, D_MODEL), F32)], axis=0)
    mod_all = _ada_call(cond, ada_w, ada_b).reshape(DEPTH, MOD_ROWS, 6, D_MODEL)

    ctx_row = lambda t: 0

    def lat_row(tm):
        per = lat_len // tm
        return lambda t: 1 + t // per

    new_ckv, new_krope = [], []
    for i in range(DEPTH):
        mod = mod_all[i]
        if i % 2 == 0:
            e = i // 2
            w = _layer0_weights(e, norm_mix_g[i], ab_w_in, pool_w, pool_scale, q_norm_g, kv_norm_g,
                                w_uq, w_ukv, qn_g, kn_g, ab_w_out)
            cs = _rope_tables(lat_len)
            yp_c, q_c, k_c, v_c, ckv_c, kr_c = _l0a_call(ctx, mod, ctx_row, 1024, ctx_len, w, None)
            yp_l, q_l, k_l, v_l, _, _ = _l0a_call(lat, mod, lat_row(lat_len), lat_len, lat_len, w, cs)
            kr_pad = jnp.pad(cache_krope[:, e].reshape(n_lat_b * past, QK_ROPE),
                             ((0, 0), (QK_NOPE, LANES - QK_DIM)))
            cache = _kvc_call(cache_ckv[:, e].reshape(n_lat_b * past, KV_RANK), kr_pad, w, 1024)
            ctx = _att_call(ctx, mod, lambda b: 0, ctx_len, yp_c, q_c, k_c, v_c, None, w)
            lat = _att_call(lat, mod, lambda b: 1 + b, lat_len, yp_l, q_l, k_l, v_l, cache, w)
            new_ckv.append(ckv_c.reshape(n_ctx_b, ctx_len, KV_RANK))
            new_krope.append(kr_c.reshape(n_ctx_b, ctx_len, QK_ROPE))
        else:
            o = i // 2
            w = {
                "norm_mix_g": norm_mix_g[i].reshape(1, D_MODEL),
                "w_in": gm_w_in[o].astype(BF16),
                "vnorm_g": gm_vnorm_g[o].reshape(1, C_WIDTH),
                "w_s": gm_ws[o].astype(BF16),
                "b_s": jnp.broadcast_to(gm_bs[o][:, :, None], (C_GROUPS, CHUNK, C_CH)),
                "w_out": gm_w_out[o].astype(BF16),
            }
            ctx = _gmlp_call(ctx, mod, ctx_row, GM_TM, w)
            lat = _gmlp_call(lat, mod, lat_row(GM_TM), GM_TM, w)
        ng = norm_ffn_g[i].reshape(1, D_MODEL)
        wg, wu, wd = ffn_wg[i].astype(BF16), ffn_wu[i].astype(BF16), ffn_wd[i].astype(BF16)
        ctx = _ffn_call(ctx, mod, ctx_row, FFN_TM, ng, wg, wu, wd)
        lat = _ffn_call(lat, mod, lat_row(FFN_TM), FFN_TM, ng, wg, wu, wd)

    state_ckv = jnp.stack(new_ckv, axis=1)
    state_krope = jnp.stack(new_krope, axis=1)
    return (ctx.reshape(n_ctx_b, ctx_len, D_MODEL), lat.reshape(n_lat_b, lat_len, D_MODEL),
            state_ckv, state_krope)
```

```python
import functools
import math

import jax
import jax.numpy as jnp
from jax import lax
from jax.experimental import pallas as pl
from jax.experimental.pallas import tpu as pltpu

D_MODEL = 1024
DEPTH = 2
GRID_W = 64
POOL_WINDOWS = (2, 4, 8, 16)
POOL_GROUPS = 4
POOL_CH = 128
POOL_WIDTH = POOL_GROUPS * POOL_CH
MLA_HEADS = 8
QK_NOPE = 64
QK_ROPE = 32
QK_DIM = QK_NOPE + QK_ROPE
V_DIM = 64
Q_RANK = 384
KV_RANK = 256
MLA_WIDTH = MLA_HEADS * V_DIM
CHUNK = 128
C_GROUPS = 8
C_WIDTH = D_MODEL
C_CH = C_WIDTH // C_GROUPS
D_FF = 2816
ROPE_BASE = 10000.0
EPS = 1e-6

LANES = 128
HEAD_W = MLA_HEADS * LANES
AB_IN_PAD = POOL_WIDTH + Q_RANK + KV_RANK + LANES
MOD_ROWS = 16
VMEM_LIMIT = 56 * 1024 * 1024

F32 = jnp.float32
BF16 = jnp.bfloat16


def _rms(x):
    return x * lax.rsqrt(jnp.mean(x * x, axis=-1, keepdims=True) + EPS)


def _dot(a, b):
    return jnp.dot(a, b, preferred_element_type=F32)


def _dot_nt(a, b):
    return lax.dot_general(a, b, (((1,), (1,)), ((), ())), preferred_element_type=F32)


def _silu(x):
    return x * jax.nn.sigmoid(x)


def _params(n_axes):
    return pltpu.CompilerParams(
        dimension_semantics=("arbitrary",) * n_axes, vmem_limit_bytes=VMEM_LIMIT)


def _const_spec(shape):
    zeros = (0,) * len(shape)
    return pl.BlockSpec(shape, lambda *_: zeros)


ADA_TN = 512


def _ada_kernel(cond_ref, w_ref, b_ref, o_ref):
    s = _silu(cond_ref[...]).astype(BF16)
    o_ref[0] = _dot(s, w_ref[0].astype(BF16)) + b_ref[0]


def _ada_call(cond, ada_w, ada_b):
    n = 6 * D_MODEL
    return pl.pallas_call(
        _ada_kernel,
        grid=(DEPTH, n // ADA_TN),
        in_specs=[
            pl.BlockSpec((MOD_ROWS, D_MODEL), lambda i, j: (0, 0)),
            pl.BlockSpec((1, D_MODEL, ADA_TN), lambda i, j: (i, 0, j)),
            pl.BlockSpec((1, 1, ADA_TN), lambda i, j: (i, 0, j)),
        ],
        out_specs=pl.BlockSpec((1, MOD_ROWS, ADA_TN), lambda i, j: (i, 0, j)),
        out_shape=jax.ShapeDtypeStruct((DEPTH, MOD_ROWS, n), F32),
        compiler_params=_params(2),
        name="ada_mod",
    )(cond, ada_w, ada_b.reshape(DEPTH, 1, n))


def _rope(t, cs_ref):
    return (t * cs_ref[0]
            + pltpu.roll(t, LANES - QK_ROPE // 2, axis=1) * cs_ref[1]
            + pltpu.roll(t, QK_ROPE // 2, axis=1) * cs_ref[2])


def _kv_heads(ckv_bf, kr, wk_ref, wv_ref, kng_ref, cs_ref, k_ref, v_ref):
    v_ref[...] = _dot(ckv_bf, wv_ref[...]).astype(BF16)
    kpre = _dot(ckv_bf, wk_ref[...])
    kg = kng_ref[...]
    krg = kr * kg
    if cs_ref is not None:
        krg = _rope(krg, cs_ref)
    lane = lax.broadcasted_iota(jnp.int32, kr.shape, 1)
    kr_ss = jnp.sum(jnp.where(lane < QK_DIM, kr * kr, 0.0), axis=-1, keepdims=True)
    for h in range(MLA_HEADS):
        kh = kpre[:, h * LANES:(h + 1) * LANES]
        r = lax.rsqrt((jnp.sum(kh * kh, axis=-1, keepdims=True) + kr_ss) * (1.0 / QK_DIM) + EPS)
        k_ref[:, h * LANES:(h + 1) * LANES] = ((kh * kg + krg) * r).astype(BF16)


def _shift_rows(a, k, pos, seq_len):
    n = a.shape[0]
    r = pltpu.roll(a, k % n, axis=0)
    src = pos - k
    ok = (src >= 0) if k > 0 else (src < seq_len)
    return jnp.where(ok, r, 0.0)


def _pool_group(ug, half, pos, seq_len):
    fw = ug
    step = 1
    while step < half:
        fw = fw + _shift_rows(fw, -step, pos, seq_len)
        step *= 2
    bk = _shift_rows(ug, 1, pos, seq_len)
    step = 1
    while step < half:
        bk = bk + _shift_rows(bk, step, pos, seq_len)
        step *= 2
    cnt = jnp.minimum(pos + half, seq_len) - jnp.maximum(pos - half, 0)
    return (fw + bk) / cnt.astype(F32) - ug


def _l0a_kernel(*refs, seq_len, use_rope):
    (x_ref, mod_ref, ng_ref, win_ref, qg_ref, kvg_ref, wuq_ref, wk_ref, wv_ref,
     qng_ref, kng_ref, pw_ref, ps_ref, ones_ref) = refs[:14]
    if use_rope:
        cs_ref = refs[14]
        outs = refs[15:]
    else:
        cs_ref = None
        outs = refs[14:]
    yp_ref, q_ref, k_ref, v_ref, ckv_ref, kr_ref = outs

    tm = x_ref.shape[0]
    m = mod_ref[0]
    h = _rms(x_ref[...]) * ng_ref[...] * (1.0 + m[1:2]) + m[0:1]
    z = _dot(h.astype(BF16), win_ref[...])
    o1, o2, o3 = POOL_WIDTH, POOL_WIDTH + Q_RANK, POOL_WIDTH + Q_RANK + KV_RANK

    ckv = _rms(z[:, o2:o3]) * kvg_ref[...]
    kr = z[:, o3:]
    ckv_ref[...] = ckv
    kr_ref[...] = kr[:, QK_NOPE:QK_DIM]
    _kv_heads(ckv.astype(BF16), kr, wk_ref, wv_ref, kng_ref, cs_ref, k_ref, v_ref)

    cqn = (_rms(z[:, o1:o2]) * qg_ref[...]).astype(BF16)
    qf = _dot(cqn, wuq_ref[...])
    qt = qng_ref[...]
    if use_rope:
        qt = qt * cs_ref[3]
    for j in range(MLA_HEADS // 2):
        qp = qf[:, 2 * j * LANES:(2 * j + 2) * LANES]
        sq = qp * qp
        hi = sq.astype(BF16)
        lo = (sq - hi.astype(F32)).astype(BF16)
        ss = _dot(hi, ones_ref[...]) + _dot(lo, ones_ref[...])
        qn = qp * lax.rsqrt(ss * (1.0 / QK_DIM) + EPS)
        for hh in range(2):
            hd = 2 * j + hh
            q_ref[:, hd * LANES:(hd + 1) * LANES] = (qn[:, hh * LANES:(hh + 1) * LANES] * qt).astype(BF16)

    pos = lax.broadcasted_iota(jnp.int32, (tm, POOL_CH), 0) & (seq_len - 1)
    for g, w in enumerate(POOL_WINDOWS):
        ug = z[:, g * POOL_CH:(g + 1) * POOL_CH]
        p = _pool_group(ug, w // 2, pos, seq_len)
        y = _dot(p.astype(BF16), pw_ref[g]) * ps_ref[:, g * POOL_CH:(g + 1) * POOL_CH]
        yp_ref[:, g * POOL_CH:(g + 1) * POOL_CH] = y.astype(BF16)


def _l0a_call(x, mod, mod_row, tm, seq_len, w, cs):
    n_tok = x.shape[0]
    use_rope = cs is not None
    row_spec = lambda width: pl.BlockSpec((tm, width), lambda t: (t, 0))
    in_specs = [
        row_spec(D_MODEL),
        pl.BlockSpec((1, 6, D_MODEL), lambda t: (mod_row(t), 0, 0)),
        _const_spec((1, D_MODEL)),
        _const_spec((D_MODEL, AB_IN_PAD)),
        _const_spec((1, Q_RANK)),
        _const_spec((1, KV_RANK)),
        _const_spec((Q_RANK, HEAD_W)),
        _const_spec((KV_RANK, HEAD_W)),
        _const_spec((KV_RANK, HEAD_W)),
        _const_spec((1, LANES)),
        _const_spec((1, LANES)),
        _const_spec((POOL_GROUPS, POOL_CH, POOL_CH)),
        _const_spec((1, POOL_WIDTH)),
        _const_spec((2 * LANES, 2 * LANES)),
    ]
    args = [x, mod, w["norm_mix_g"], w["w_in"], w["q_norm_g"], w["kv_norm_g"], w["w_uq"],
            w["w_k"], w["w_v"], w["qn_g_rope"] if use_rope else w["qn_g"], w["kn_g"],
            w["pool_w"], w["pool_scale"], w["head_ones"]]
    if use_rope:
        assert tm == seq_len
        in_specs.append(_const_spec((4, seq_len, LANES)))
        args.append(cs)
    out_shape = [
        jax.ShapeDtypeStruct((n_tok, POOL_WIDTH), BF16),
        jax.ShapeDtypeStruct((n_tok, HEAD_W), BF16),
        jax.ShapeDtypeStruct((n_tok, HEAD_W), BF16),
        jax.ShapeDtypeStruct((n_tok, HEAD_W), BF16),
        jax.ShapeDtypeStruct((n_tok, KV_RANK), F32),
        jax.ShapeDtypeStruct((n_tok, QK_ROPE), F32),
    ]
    out_specs = [row_spec(POOL_WIDTH), row_spec(HEAD_W), row_spec(HEAD_W), row_spec(HEAD_W),
                 row_spec(KV_RANK), row_spec(QK_ROPE)]
    return pl.pallas_call(
        functools.partial(_l0a_kernel, seq_len=seq_len, use_rope=use_rope),
        grid=(n_tok // tm,),
        in_specs=in_specs,
        out_specs=out_specs,
        out_shape=out_shape,
        compiler_params=_params(1),
        name="l0_front_rope" if use_rope else "l0_front",
    )(*args)


def _kvc_kernel(ckv_ref, kr_ref, wk_ref, wv_ref, kng_ref, k_ref, v_ref):
    _kv_heads(ckv_ref[...].astype(BF16), kr_ref[...], wk_ref, wv_ref, kng_ref, None, k_ref, v_ref)


def _kvc_call(ckv, kr_pad, w, tm):
    n_tok = ckv.shape[0]
    row_spec = lambda width: pl.BlockSpec((tm, width), lambda t: (t, 0))
    return pl.pallas_call(
        _kvc_kernel,
        grid=(n_tok // tm,),
        in_specs=[row_spec(KV_RANK), row_spec(LANES), _const_spec((KV_RANK, HEAD_W)),
                  _const_spec((KV_RANK, HEAD_W)), _const_spec((1, LANES))],
        out_specs=[row_spec(HEAD_W), row_spec(HEAD_W)],
        out_shape=[jax.ShapeDtypeStruct((n_tok, HEAD_W), BF16)] * 2,
        compiler_params=_params(1),
        name="cache_kv",
    )(ckv, kr_pad, w["w_k"], w["w_v"], w["kn_g"])


ATT_TQ = 256


def _att_kernel(*refs, has_cache):
    if has_cache:
        (x_ref, mod_ref, yp_ref, q_ref, k_ref, v_ref, kc_ref, vc_ref,
         wop_ref, woa_ref, o_ref, ya_ref) = refs
    else:
        x_ref, mod_ref, yp_ref, q_ref, k_ref, v_ref, wop_ref, woa_ref, o_ref, ya_ref = refs
        kc_ref = vc_ref = None

    for j in range(MLA_HEADS // 2):
        pair = None
        for hd in (2 * j, 2 * j + 1):
            sl = slice(hd * LANES, (hd + 1) * LANES)
            qh = q_ref[:, sl]
            s = _dot_nt(qh, k_ref[:, sl])
            mx = jnp.max(s, axis=-1, keepdims=True)
            if has_cache:
                sc = _dot_nt(qh, kc_ref[:, sl])
                mx = jnp.maximum(mx, jnp.max(sc, axis=-1, keepdims=True))
            p = jnp.exp2(s - mx)
            den = jnp.sum(p, axis=-1, keepdims=True)
            o = _dot(p.astype(BF16), v_ref[:, sl])
            if has_cache:
                pc = jnp.exp2(sc - mx)
                den = den + jnp.sum(pc, axis=-1, keepdims=True)
                o = o + _dot(pc.astype(BF16), vc_ref[:, sl])
            o = o / den
            pair = o if pair is None else pair + o
        ya_ref[:, j * LANES:(j + 1) * LANES] = pair.astype(BF16)

    y = _dot(yp_ref[...], wop_ref[...]) + _dot(ya_ref[...], woa_ref[...])
    o_ref[...] = x_ref[...] + mod_ref[0][2:3] * y


def _att_call(x, mod, mod_row, seq_len, yp, q, k, v, cache, w):
    n_tok = x.shape[0]
    n_b = n_tok // seq_len
    n_q = seq_len // ATT_TQ
    qrow = lambda width: pl.BlockSpec((ATT_TQ, width), lambda b, i: (b * n_q + i, 0))
    brow = lambda rows, width: pl.BlockSpec((rows, width), lambda b, i: (b, 0))
    in_specs = [
        qrow(D_MODEL),
        pl.BlockSpec((1, 6, D_MODEL), lambda b, i: (mod_row(b), 0, 0)),
        qrow(POOL_WIDTH), qrow(HEAD_W), brow(seq_len, HEAD_W), brow(seq_len, HEAD_W),
    ]
    args = [x, mod, yp, q, k, v]
    if cache is not None:
        kc, vc = cache
        past = kc.shape[0] // n_b
        in_specs += [brow(past, HEAD_W), brow(past, HEAD_W)]
        args += [kc, vc]
    in_specs += [_const_spec((POOL_WIDTH, D_MODEL)), _const_spec((MLA_WIDTH, D_MODEL))]
    args += [w["w_out_pool"], w["w_out_att"]]
    return pl.pallas_call(
        functools.partial(_att_kernel, has_cache=cache is not None),
        grid=(n_b, n_q),
        in_specs=in_specs,
        out_specs=qrow(D_MODEL),
        out_shape=jax.ShapeDtypeStruct((n_tok, D_MODEL), F32),
        scratch_shapes=[pltpu.VMEM((ATT_TQ, MLA_WIDTH), BF16)],
        compiler_params=_params(2),
        name="att_out_cache" if cache is not None else "att_out",
    )(*args)


FFN_TM = 512
FFN_CHUNKS = 2


def _ffn_kernel(x_ref, mod_ref, ng_ref, wg_ref, wu_ref, wd_ref, o_ref):
    x = x_ref[...]
    m = mod_ref[0]
    h = (_rms(x) * ng_ref[...] * (1.0 + m[4:5]) + m[3:4]).astype(BF16)
    fc = D_FF // FFN_CHUNKS
    acc = None
    for c in range(FFN_CHUNKS):
        g = _dot(h, wg_ref[:, c * fc:(c + 1) * fc])
        u = _dot(h, wu_ref[:, c * fc:(c + 1) * fc])
        d = _dot((_silu(g) * u).astype(BF16), wd_ref[c * fc:(c + 1) * fc, :])
        acc = d if acc is None else acc + d
    o_ref[...] = x + m[5:6] * acc


def _ffn_call(x, mod, mod_row, tm, ng, wg, wu, wd):
    n_tok = x.shape[0]
    resident = lambda shape: pl.BlockSpec(shape, lambda t: (0, 0), pipeline_mode=pl.Buffered(1))
    return pl.pallas_call(
        _ffn_kernel,
        grid=(n_tok // tm,),
        in_specs=[
            pl.BlockSpec((tm, D_MODEL), lambda t: (t, 0)),
            pl.BlockSpec((1, 6, D_MODEL), lambda t: (mod_row(t), 0, 0)),
            _const_spec((1, D_MODEL)),
            resident((D_MODEL, D_FF)), resident((D_MODEL, D_FF)), resident((D_FF, D_MODEL)),
        ],
        out_specs=pl.BlockSpec((tm, D_MODEL), lambda t: (t, 0)),
        out_shape=jax.ShapeDtypeStruct((n_tok, D_MODEL), F32),
        compiler_params=_params(1),
        name="ffn",
    )(x, mod, ng, wg, wu, wd)


GM_TM = 512


def _gmlp_kernel(x_ref, mod_ref, ng_ref, win_ref, vg_ref, ws_ref, bs_ref, wout_ref, o_ref, gs_ref):
    x = x_ref[...]
    m = mod_ref[0]
    h = (_rms(x) * ng_ref[...] * (1.0 + m[1:2]) + m[0:1]).astype(BF16)
    z = _dot(h, win_ref[...])
    vn = (_rms(z[:, C_WIDTH:]) * vg_ref[...]).astype(BF16)
    n_chunks = x.shape[0] // CHUNK
    for g in range(C_GROUPS):
        cols = slice(g * C_CH, (g + 1) * C_CH)
        rhs = jnp.concatenate(
            [vn[n * CHUNK:(n + 1) * CHUNK, cols] for n in range(n_chunks)], axis=1)
        s = _dot(ws_ref[g], rhs)
        for n in range(n_chunks):
            rows = slice(n * CHUNK, (n + 1) * CHUNK)
            sn = s[:, n * C_CH:(n + 1) * C_CH] + bs_ref[g]
            gs_ref[rows, cols] = (z[rows, cols] * sn).astype(BF16)
    o_ref[...] = x + m[2:3] * _dot(gs_ref[...], wout_ref[...])


def _gmlp_call(x, mod, mod_row, tm, w):
    n_tok = x.shape[0]
    return pl.pallas_call(
        _gmlp_kernel,
        grid=(n_tok // tm,),
        in_specs=[
            pl.BlockSpec((tm, D_MODEL), lambda t: (t, 0)),
            pl.BlockSpec((1, 6, D_MODEL), lambda t: (mod_row(t), 0, 0)),
            _const_spec((1, D_MODEL)),
            _const_spec((D_MODEL, 2 * C_WIDTH)),
            _const_spec((1, C_WIDTH)),
            _const_spec((C_GROUPS, CHUNK, CHUNK)),
            _const_spec((C_GROUPS, CHUNK, C_CH)),
            _const_spec((C_WIDTH, D_MODEL)),
        ],
        out_specs=pl.BlockSpec((tm, D_MODEL), lambda t: (t, 0)),
        out_shape=jax.ShapeDtypeStruct((n_tok, D_MODEL), F32),
        scratch_shapes=[pltpu.VMEM((tm, C_WIDTH), BF16)],
        compiler_params=_params(1),
        name="gmlp",
    )(x, mod, w["norm_mix_g"], w["w_in"], w["vnorm_g"], w["w_s"], w["b_s"], w["w_out"])


HALF = QK_ROPE // 2


def _swap_halves(t):
    return jnp.concatenate([t[..., HALF:], t[..., :HALF]], axis=-1)


def _layer0_weights(e, norm_mix_g, ab_w_in, pool_w, pool_scale, q_norm_g, kv_norm_g, w_uq, w_ukv,
                    qn_g, kn_g, ab_w_out):
    w_in = ab_w_in[e]
    o3 = POOL_WIDTH + Q_RANK + KV_RANK
    kr_cols = jnp.concatenate(
        [jnp.zeros((D_MODEL, QK_NOPE), F32), w_in[:, o3:], w_in[:, o3:]], axis=1)
    w_in_p = jnp.concatenate([w_in[:, :o3], kr_cols], axis=1).astype(BF16)
    uq = w_uq[e].reshape(Q_RANK, MLA_HEADS, QK_DIM)
    uq = jnp.concatenate([uq, _swap_halves(uq[:, :, QK_NOPE:])], axis=-1)
    score_scale = QK_DIM ** -0.5 * math.log2(math.e)
    gq = qn_g[e] * score_scale
    gk = kn_g[e]
    lane = jnp.arange(2 * LANES)
    head_ones = ((lane[:, None] // LANES == lane[None, :] // LANES)
                 & (lane[:, None] % LANES < QK_DIM)).astype(BF16)
    ukv = w_ukv[e].reshape(KV_RANK, MLA_HEADS, QK_NOPE + V_DIM)
    w_k = jnp.pad(ukv[:, :, :QK_NOPE], ((0, 0), (0, 0), (0, LANES - QK_NOPE)))
    w_v = ukv[:, :, QK_NOPE:].reshape(KV_RANK, MLA_HEADS // 2, 2, V_DIM)
    zero = jnp.zeros_like(w_v[:, :, 0])
    w_v = jnp.stack([jnp.concatenate([w_v[:, :, 0], zero], axis=-1),
                     jnp.concatenate([zero, w_v[:, :, 1]], axis=-1)], axis=2)
    return {
        "norm_mix_g": norm_mix_g.reshape(1, D_MODEL),
        "w_in": w_in_p,
        "q_norm_g": q_norm_g[e].reshape(1, Q_RANK),
        "kv_norm_g": kv_norm_g[e].reshape(1, KV_RANK),
        "w_uq": uq.reshape(Q_RANK, HEAD_W).astype(BF16),
        "w_k": w_k.reshape(KV_RANK, HEAD_W).astype(BF16),
        "w_v": w_v.reshape(KV_RANK, HEAD_W).astype(BF16),
        "qn_g": jnp.pad(gq, (0, LANES - QK_DIM)).reshape(1, LANES),
        "qn_g_rope": jnp.concatenate([gq, _swap_halves(gq[QK_NOPE:])]).reshape(1, LANES),
        "kn_g": jnp.concatenate([gk, gk[QK_NOPE:]]).reshape(1, LANES),
        "head_ones": head_ones,
        "pool_w": pool_w[e].astype(BF16),
        "pool_scale": pool_scale[e].reshape(1, POOL_WIDTH),
        "w_out_pool": ab_w_out[e, :POOL_WIDTH].astype(BF16),
        "w_out_att": ab_w_out[e, POOL_WIDTH:].astype(BF16),
    }


def _rope_tables(seq_len):
    rows = seq_len // GRID_W
    row = jnp.repeat(jnp.arange(rows), GRID_W).astype(F32)
    col = jnp.tile(jnp.arange(GRID_W), rows).astype(F32)
    per_axis = QK_ROPE // 2
    inv = 1.0 / (ROPE_BASE ** (jnp.arange(0, per_axis, 2, dtype=F32) / per_axis))
    ang = jnp.concatenate([row[:, None] * inv, col[:, None] * inv], axis=-1)
    cos, sin = jnp.cos(ang), jnp.sin(ang)
    ones = jnp.ones((seq_len, QK_NOPE), F32)
    z_nope = jnp.zeros((seq_len, QK_NOPE), F32)
    z_half = jnp.zeros((seq_len, HALF), F32)
    c_tab = jnp.concatenate([ones, cos, cos, cos, cos], axis=-1)
    a_tab = jnp.concatenate([z_nope, -sin, z_half, -sin, z_half], axis=-1)
    b_tab = jnp.concatenate([z_nope, z_half, sin, z_half, sin], axis=-1)
    q_tab = jnp.concatenate([ones, cos, cos, -sin, sin], axis=-1)
    return jnp.stack([c_tab, a_tab, b_tab, q_tab])


def kernel(x_prompt, x_sample, cache_ckv, cache_krope, c, c_ctx, ada_w, ada_b, norm_mix_g, norm_ffn_g, ffn_wg, ffn_wu, ffn_wd, ab_w_in, pool_w, pool_scale, q_norm_g, kv_norm_g, w_uq, w_ukv, qn_g, kn_g, ab_w_out, gm_w_in, gm_vnorm_g, gm_ws, gm_bs, gm_w_out):
    n_ctx_b, ctx_len, _ = x_prompt.shape
    n_lat_b, lat_len, _ = x_sample.shape
    past = cache_ckv.shape[2]
    assert ctx_len & (ctx_len - 1) == 0 and lat_len & (lat_len - 1) == 0
    assert MOD_ROWS >= 1 + n_lat_b

    ctx = x_prompt.reshape(n_ctx_b * ctx_len, D_MODEL)
    lat = x_sample.reshape(n_lat_b * lat_len, D_MODEL)

    cond = jnp.concatenate(
        [c_ctx[None, :], c, jnp.zeros((MOD_ROWS - 1 - n_lat_b, D_MODEL), F32)], axis=0)
    mod_all = _ada_call(cond, ada_w, ada_b).reshape(DEPTH, MOD_ROWS, 6, D_MODEL)

    ctx_row = lambda t: 0

    def lat_row(tm):
        per = lat_len // tm
        return lambda t: 1 + t // per

    new_ckv, new_krope = [], []
    for i in range(DEPTH):
        mod = mod_all[i]
        if i % 2 == 0:
            e = i // 2
            w = _layer0_weights(e, norm_mix_g[i], ab_w_in, pool_w, pool_scale, q_norm_g, kv_norm_g,
                                w_uq, w_ukv, qn_g, kn_g, ab_w_out)
            cs = _rope_tables(lat_len)
            yp_c, q_c, k_c, v_c, ckv_c, kr_c = _l0a_call(ctx, mod, ctx_row, 1024, ctx_len, w, None)
            yp_l, q_l, k_l, v_l, _, _ = _l0a_call(lat, mod, lat_row(lat_len), lat_len, lat_len, w, cs)
            kr_c2 = cache_krope[:, e].reshape(n_lat_b * past, QK_ROPE)
            kr_pad = jnp.concatenate(
                [jnp.zeros((n_lat_b * past, QK_NOPE), F32), kr_c2, kr_c2], axis=1)
            cache = _kvc_call(cache_ckv[:, e].reshape(n_lat_b * past, KV_RANK), kr_pad, w, 1024)
            ctx = _att_call(ctx, mod, lambda b: 0, ctx_len, yp_c, q_c, k_c, v_c, None, w)
            lat = _att_call(lat, mod, lambda b: 1 + b, lat_len, yp_l, q_l, k_l, v_l, cache, w)
            new_ckv.append(ckv_c.reshape(n_ctx_b, ctx_len, KV_RANK))
            new_krope.append(kr_c.reshape(n_ctx_b, ctx_len, QK_ROPE))
        else:
            o = i // 2
            w = {
                "norm_mix_g": norm_mix_g[i].reshape(1, D_MODEL),
                "w_in": gm_w_in[o].astype(BF16),
                "vnorm_g": gm_vnorm_g[o].reshape(1, C_WIDTH),
                "w_s": gm_ws[o].astype(BF16),
                "b_s": jnp.broadcast_to(gm_bs[o][:, :, None], (C_GROUPS, CHUNK, C_CH)),
                "w_out": gm_w_out[o].astype(BF16),
            }
            ctx = _gmlp_call(ctx, mod, ctx_row, GM_TM, w)
            lat = _gmlp_call(lat, mod, lat_row(GM_TM), GM_TM, w)
        ng = norm_ffn_g[i].reshape(1, D_MODEL)
        wg, wu, wd = ffn_wg[i].astype(BF16), ffn_wu[i].astype(BF16), ffn_wd[i].astype(BF16)
        ctx = _ffn_call(ctx, mod, ctx_row, FFN_TM, ng, wg, wu, wd)
        lat = _ffn_call(lat, mod, lat_row(FFN_TM), FFN_TM, ng, wg, wu, wd)

    state_ckv = jnp.stack(new_ckv, axis=1)
    state_krope = jnp.stack(new_krope, axis=1)
    return (ctx.reshape(n_ctx_b, ctx_len, D_MODEL), lat.reshape(n_lat_b, lat_len, D_MODEL),
            state_ckv, state_krope)
```

```python
import functools
import math

import jax
import jax.numpy as jnp
from jax import lax
from jax.experimental import pallas as pl
from jax.experimental.pallas import tpu as pltpu

D_MODEL = 1024
DEPTH = 2
GRID_W = 64
POOL_WINDOWS = (2, 4, 8, 16)
POOL_GROUPS = 4
POOL_CH = 128
POOL_WIDTH = POOL_GROUPS * POOL_CH
MLA_HEADS = 8
QK_NOPE = 64
QK_ROPE = 32
QK_DIM = QK_NOPE + QK_ROPE
V_DIM = 64
Q_RANK = 384
KV_RANK = 256
MLA_WIDTH = MLA_HEADS * V_DIM
CHUNK = 128
C_GROUPS = 8
C_WIDTH = D_MODEL
C_CH = C_WIDTH // C_GROUPS
D_FF = 2816
ROPE_BASE = 10000.0
EPS = 1e-6

LANES = 128
HEAD_W = MLA_HEADS * LANES
AB_IN_PAD = POOL_WIDTH + Q_RANK + KV_RANK + LANES
MOD_ROWS = 16
VMEM_LIMIT = 56 * 1024 * 1024

F32 = jnp.float32
BF16 = jnp.bfloat16


def _rms(x):
    return x * lax.rsqrt(jnp.mean(x * x, axis=-1, keepdims=True) + EPS)


def _dot(a, b):
    return jnp.dot(a, b, preferred_element_type=F32)


def _dot_nt(a, b):
    return lax.dot_general(a, b, (((1,), (1,)), ((), ())), preferred_element_type=F32)


def _silu(x):
    return x * jax.nn.sigmoid(x)


def _params(n_axes):
    return pltpu.CompilerParams(
        dimension_semantics=("arbitrary",) * n_axes, vmem_limit_bytes=VMEM_LIMIT)


def _const_spec(shape):
    zeros = (0,) * len(shape)
    return pl.BlockSpec(shape, lambda *_: zeros)


ADA_TN = 512


def _ada_kernel(cond_ref, w_ref, b_ref, o_ref):
    s = _silu(cond_ref[...]).astype(BF16)
    o_ref[0] = _dot(s, w_ref[0].astype(BF16)) + b_ref[0]


def _ada_call(cond, ada_w, ada_b):
    n = 6 * D_MODEL
    return pl.pallas_call(
        _ada_kernel,
        grid=(DEPTH, n // ADA_TN),
        in_specs=[
            pl.BlockSpec((MOD_ROWS, D_MODEL), lambda i, j: (0, 0)),
            pl.BlockSpec((1, D_MODEL, ADA_TN), lambda i, j: (i, 0, j)),
            pl.BlockSpec((1, 1, ADA_TN), lambda i, j: (i, 0, j)),
        ],
        out_specs=pl.BlockSpec((1, MOD_ROWS, ADA_TN), lambda i, j: (i, 0, j)),
        out_shape=jax.ShapeDtypeStruct((DEPTH, MOD_ROWS, n), F32),
        compiler_params=_params(2),
        name="ada_mod",
    )(cond, ada_w, ada_b.reshape(DEPTH, 1, n))


def _rope(t, cs_ref):
    return (t * cs_ref[0]
            + pltpu.roll(t, LANES - QK_ROPE // 2, axis=1) * cs_ref[1]
            + pltpu.roll(t, QK_ROPE // 2, axis=1) * cs_ref[2])


def _kv_heads(ckv_bf, kr, wk_ref, wv_ref, kng_ref, cs_ref, k_ref, v_ref):
    v_ref[...] = _dot(ckv_bf, wv_ref[...]).astype(BF16)
    kpre = _dot(ckv_bf, wk_ref[...])
    kg = kng_ref[...]
    krg = kr * kg
    if cs_ref is not None:
        krg = _rope(krg, cs_ref)
    lane = lax.broadcasted_iota(jnp.int32, kr.shape, 1)
    kr_ss = jnp.sum(jnp.where(lane < QK_DIM, kr * kr, 0.0), axis=-1, keepdims=True)
    for h in range(MLA_HEADS):
        kh = kpre[:, h * LANES:(h + 1) * LANES]
        r = lax.rsqrt((jnp.sum(kh * kh, axis=-1, keepdims=True) + kr_ss) * (1.0 / QK_DIM) + EPS)
        k_ref[:, h * LANES:(h + 1) * LANES] = ((kh * kg + krg) * r).astype(BF16)


def _shift_rows(a, k, pos, seq_len):
    n = a.shape[0]
    r = pltpu.roll(a, k % n, axis=0)
    src = pos - k
    ok = (src >= 0) if k > 0 else (src < seq_len)
    return jnp.where(ok, r, 0.0)


def _pool_group(ug, half, pos, seq_len):
    fw = ug
    step = 1
    while step < half:
        fw = fw + _shift_rows(fw, -step, pos, seq_len)
        step *= 2
    bk = _shift_rows(ug, 1, pos, seq_len)
    step = 1
    while step < half:
        bk = bk + _shift_rows(bk, step, pos, seq_len)
        step *= 2
    cnt = jnp.minimum(pos + half, seq_len) - jnp.maximum(pos - half, 0)
    return (fw + bk) / cnt.astype(F32) - ug


def _l0a_kernel(*refs, seq_len, use_rope):
    (x_ref, mod_ref, ng_ref, win_ref, qg_ref, kvg_ref, wuq_ref, wk_ref, wv_ref,
     qng_ref, kng_ref, pw_ref, ps_ref, ones_ref) = refs[:14]
    if use_rope:
        cs_ref = refs[14]
        outs = refs[15:]
    else:
        cs_ref = None
        outs = refs[14:]
    yp_ref, q_ref, k_ref, v_ref, ckv_ref, kr_ref = outs

    tm = x_ref.shape[0]
    m = mod_ref[0]
    h = _rms(x_ref[...]) * ng_ref[...] * (1.0 + m[1:2]) + m[0:1]
    z = _dot(h.astype(BF16), win_ref[...])
    o1, o2, o3 = POOL_WIDTH, POOL_WIDTH + Q_RANK, POOL_WIDTH + Q_RANK + KV_RANK

    ckv = _rms(z[:, o2:o3]) * kvg_ref[...]
    kr = z[:, o3:]
    ckv_ref[...] = ckv
    kr_ref[...] = kr[:, QK_NOPE:QK_DIM]
    _kv_heads(ckv.astype(BF16), kr, wk_ref, wv_ref, kng_ref, cs_ref, k_ref, v_ref)

    cqn = (_rms(z[:, o1:o2]) * qg_ref[...]).astype(BF16)
    qf = _dot(cqn, wuq_ref[...])
    qt = qng_ref[...]
    if use_rope:
        qt = qt * cs_ref[3]
    for j in range(MLA_HEADS // 2):
        qp = qf[:, 2 * j * LANES:(2 * j + 2) * LANES]
        sq = qp * qp
        hi = sq.astype(BF16)
        lo = (sq - hi.astype(F32)).astype(BF16)
        ss = _dot(hi, ones_ref[...]) + _dot(lo, ones_ref[...])
        qn = qp * lax.rsqrt(ss * (1.0 / QK_DIM) + EPS)
        for hh in range(2):
            hd = 2 * j + hh
            q_ref[:, hd * LANES:(hd + 1) * LANES] = (qn[:, hh * LANES:(hh + 1) * LANES] * qt).astype(BF16)

    pos = lax.broadcasted_iota(jnp.int32, (tm, POOL_CH), 0) & (seq_len - 1)
    for g, w in enumerate(POOL_WINDOWS):
        ug = z[:, g * POOL_CH:(g + 1) * POOL_CH]
        p = _pool_group(ug, w // 2, pos, seq_len)
        y = _dot(p.astype(BF16), pw_ref[g]) * ps_ref[:, g * POOL_CH:(g + 1) * POOL_CH]
        yp_ref[:, g * POOL_CH:(g + 1) * POOL_CH] = y.astype(BF16)


def _l0a_call(x, mod, mod_row, tm, seq_len, w, cs):
    n_tok = x.shape[0]
    use_rope = cs is not None
    row_spec = lambda width: pl.BlockSpec((tm, width), lambda t: (t, 0))
    in_specs = [
        row_spec(D_MODEL),
        pl.BlockSpec((1, 6, D_MODEL), lambda t: (mod_row(t), 0, 0)),
        _const_spec((1, D_MODEL)),
        _const_spec((D_MODEL, AB_IN_PAD)),
        _const_spec((1, Q_RANK)),
        _const_spec((1, KV_RANK)),
        _const_spec((Q_RANK, HEAD_W)),
        _const_spec((KV_RANK, HEAD_W)),
        _const_spec((KV_RANK, HEAD_W)),
        _const_spec((1, LANES)),
        _const_spec((1, LANES)),
        _const_spec((POOL_GROUPS, POOL_CH, POOL_CH)),
        _const_spec((1, POOL_WIDTH)),
        _const_spec((2 * LANES, 2 * LANES)),
    ]
    args = [x, mod, w["norm_mix_g"], w["w_in"], w["q_norm_g"], w["kv_norm_g"], w["w_uq"],
            w["w_k"], w["w_v"], w["qn_g_rope"] if use_rope else w["qn_g"], w["kn_g"],
            w["pool_w"], w["pool_scale"], w["head_ones"]]
    if use_rope:
        assert tm == seq_len
        in_specs.append(_const_spec((4, seq_len, LANES)))
        args.append(cs)
    out_shape = [
        jax.ShapeDtypeStruct((n_tok, POOL_WIDTH), BF16),
        jax.ShapeDtypeStruct((n_tok, HEAD_W), BF16),
        jax.ShapeDtypeStruct((n_tok, HEAD_W), BF16),
        jax.ShapeDtypeStruct((n_tok, HEAD_W), BF16),
        jax.ShapeDtypeStruct((n_tok, KV_RANK), F32),
        jax.ShapeDtypeStruct((n_tok, QK_ROPE), F32),
    ]
    out_specs = [row_spec(POOL_WIDTH), row_spec(HEAD_W), row_spec(HEAD_W), row_spec(HEAD_W),
                 row_spec(KV_RANK), row_spec(QK_ROPE)]
    return pl.pallas_call(
        functools.partial(_l0a_kernel, seq_len=seq_len, use_rope=use_rope),
        grid=(n_tok // tm,),
        in_specs=in_specs,
        out_specs=out_specs,
        out_shape=out_shape,
        compiler_params=_params(1),
        name="l0_front_rope" if use_rope else "l0_front",
    )(*args)


def _kvc_kernel(ckv_ref, kr_ref, wk_ref, wv_ref, kng_ref, k_ref, v_ref):
    _kv_heads(ckv_ref[...].astype(BF16), kr_ref[...], wk_ref, wv_ref, kng_ref, None, k_ref, v_ref)


def _kvc_call(ckv, kr_pad, w, tm):
    n_tok = ckv.shape[0]
    row_spec = lambda width: pl.BlockSpec((tm, width), lambda t: (t, 0))
    return pl.pallas_call(
        _kvc_kernel,
        grid=(n_tok // tm,),
        in_specs=[row_spec(KV_RANK), row_spec(LANES), _const_spec((KV_RANK, HEAD_W)),
                  _const_spec((KV_RANK, HEAD_W)), _const_spec((1, LANES))],
        out_specs=[row_spec(HEAD_W), row_spec(HEAD_W)],
        out_shape=[jax.ShapeDtypeStruct((n_tok, HEAD_W), BF16)] * 2,
        compiler_params=_params(1),
        name="cache_kv",
    )(ckv, kr_pad, w["w_k"], w["w_v"], w["kn_g"])


ATT_TQ = 256


def _att_kernel(*refs, has_cache):
    if has_cache:
        (x_ref, mod_ref, yp_ref, q_ref, k_ref, v_ref, kc_ref, vc_ref,
         wop_ref, woa_ref, o_ref, ya_ref) = refs
    else:
        x_ref, mod_ref, yp_ref, q_ref, k_ref, v_ref, wop_ref, woa_ref, o_ref, ya_ref = refs
        kc_ref = vc_ref = None

    for j in range(MLA_HEADS // 2):
        pair = None
        for hd in (2 * j, 2 * j + 1):
            sl = slice(hd * LANES, (hd + 1) * LANES)
            qh = q_ref[:, sl]
            s = _dot_nt(qh, k_ref[:, sl])
            mx = jnp.max(s, axis=-1, keepdims=True)
            if has_cache:
                sc = _dot_nt(qh, kc_ref[:, sl])
                mx = jnp.maximum(mx, jnp.max(sc, axis=-1, keepdims=True))
            p = jnp.exp2(s - mx)
            den = jnp.sum(p, axis=-1, keepdims=True)
            o = _dot(p.astype(BF16), v_ref[:, sl])
            if has_cache:
                pc = jnp.exp2(sc - mx)
                den = den + jnp.sum(pc, axis=-1, keepdims=True)
                o = o + _dot(pc.astype(BF16), vc_ref[:, sl])
            o = o / den
            pair = o if pair is None else pair + o
        ya_ref[:, j * LANES:(j + 1) * LANES] = pair.astype(BF16)

    y = _dot(yp_ref[...], wop_ref[...]) + _dot(ya_ref[...], woa_ref[...])
    o_ref[...] = x_ref[...] + mod_ref[0][2:3] * y


def _att_call(x, mod, mod_row, seq_len, yp, q, k, v, cache, w):
    n_tok = x.shape[0]
    n_b = n_tok // seq_len
    n_q = seq_len // ATT_TQ
    qrow = lambda width: pl.BlockSpec((ATT_TQ, width), lambda b, i: (b * n_q + i, 0))
    brow = lambda rows, width: pl.BlockSpec((rows, width), lambda b, i: (b, 0))
    in_specs = [
        qrow(D_MODEL),
        pl.BlockSpec((1, 6, D_MODEL), lambda b, i: (mod_row(b), 0, 0)),
        qrow(POOL_WIDTH), qrow(HEAD_W), brow(seq_len, HEAD_W), brow(seq_len, HEAD_W),
    ]
    args = [x, mod, yp, q, k, v]
    if cache is not None:
        kc, vc = cache
        past = kc.shape[0] // n_b
        in_specs += [brow(past, HEAD_W), brow(past, HEAD_W)]
        args += [kc, vc]
    in_specs += [_const_spec((POOL_WIDTH, D_MODEL)), _const_spec((MLA_WIDTH, D_MODEL))]
    args += [w["w_out_pool"], w["w_out_att"]]
    return pl.pallas_call(
        functools.partial(_att_kernel, has_cache=cache is not None),
        grid=(n_b, n_q),
        in_specs=in_specs,
        out_specs=qrow(D_MODEL),
        out_shape=jax.ShapeDtypeStruct((n_tok, D_MODEL), F32),
        scratch_shapes=[pltpu.VMEM((ATT_TQ, MLA_WIDTH), BF16)],
        compiler_params=_params(2),
        name="att_out_cache" if cache is not None else "att_out",
    )(*args)


FFN_TM = 1024
FFN_SUB = 256


def _ffn_kernel(x_ref, mod_ref, ng_ref, wg_ref, wu_ref, wd_ref, o_ref):
    m = mod_ref[0]
    scale = ng_ref[...] * (1.0 + m[4:5])
    for s in range(x_ref.shape[0] // FFN_SUB):
        rows = slice(s * FFN_SUB, (s + 1) * FFN_SUB)
        x = x_ref[rows, :]
        h = (_rms(x) * scale + m[3:4]).astype(BF16)
        g = _dot(h, wg_ref[0])
        u = _dot(h, wu_ref[0])
        d = _dot((_silu(g) * u).astype(BF16), wd_ref[0])
        o_ref[rows, :] = x + m[5:6] * d


def _ffn_call(x, mod, mod_row, tm, ng, layer, wg, wu, wd):
    n_tok = x.shape[0]
    resident = lambda shape: pl.BlockSpec((1,) + shape, lambda t: (layer, 0, 0),
                                          pipeline_mode=pl.Buffered(1))
    return pl.pallas_call(
        _ffn_kernel,
        grid=(n_tok // tm,),
        in_specs=[
            pl.BlockSpec((tm, D_MODEL), lambda t: (t, 0)),
            pl.BlockSpec((1, 6, D_MODEL), lambda t: (mod_row(t), 0, 0)),
            _const_spec((1, D_MODEL)),
            resident((D_MODEL, D_FF)), resident((D_MODEL, D_FF)), resident((D_FF, D_MODEL)),
        ],
        out_specs=pl.BlockSpec((tm, D_MODEL), lambda t: (t, 0)),
        out_shape=jax.ShapeDtypeStruct((n_tok, D_MODEL), F32),
        compiler_params=_params(1),
        name="ffn",
    )(x, mod, ng, wg, wu, wd)


GM_TM = 512


def _gmlp_kernel(x_ref, mod_ref, ng_ref, win_ref, vg_ref, ws_ref, bs_ref, wout_ref, o_ref, gs_ref):
    x = x_ref[...]
    m = mod_ref[0]
    h = (_rms(x) * ng_ref[...] * (1.0 + m[1:2]) + m[0:1]).astype(BF16)
    z = _dot(h, win_ref[...])
    vn = (_rms(z[:, C_WIDTH:]) * vg_ref[...]).astype(BF16)
    n_chunks = x.shape[0] // CHUNK
    for g in range(C_GROUPS):
        cols = slice(g * C_CH, (g + 1) * C_CH)
        rhs = jnp.concatenate(
            [vn[n * CHUNK:(n + 1) * CHUNK, cols] for n in range(n_chunks)], axis=1)
        s = _dot(ws_ref[g], rhs)
        for n in range(n_chunks):
            rows = slice(n * CHUNK, (n + 1) * CHUNK)
            sn = s[:, n * C_CH:(n + 1) * C_CH] + bs_ref[g]
            gs_ref[rows, cols] = (z[rows, cols] * sn).astype(BF16)
    o_ref[...] = x + m[2:3] * _dot(gs_ref[...], wout_ref[...])


def _gmlp_call(x, mod, mod_row, tm, w):
    n_tok = x.shape[0]
    return pl.pallas_call(
        _gmlp_kernel,
        grid=(n_tok // tm,),
        in_specs=[
            pl.BlockSpec((tm, D_MODEL), lambda t: (t, 0)),
            pl.BlockSpec((1, 6, D_MODEL), lambda t: (mod_row(t), 0, 0)),
            _const_spec((1, D_MODEL)),
            _const_spec((D_MODEL, 2 * C_WIDTH)),
            _const_spec((1, C_WIDTH)),
            _const_spec((C_GROUPS, CHUNK, CHUNK)),
            _const_spec((C_GROUPS, CHUNK, C_CH)),
            _const_spec((C_WIDTH, D_MODEL)),
        ],
        out_specs=pl.BlockSpec((tm, D_MODEL), lambda t: (t, 0)),
        out_shape=jax.ShapeDtypeStruct((n_tok, D_MODEL), F32),
        scratch_shapes=[pltpu.VMEM((tm, C_WIDTH), BF16)],
        compiler_params=_params(1),
        name="gmlp",
    )(x, mod, w["norm_mix_g"], w["w_in"], w["vnorm_g"], w["w_s"], w["b_s"], w["w_out"])


HALF = QK_ROPE // 2


def _swap_halves(t):
    return jnp.concatenate([t[..., HALF:], t[..., :HALF]], axis=-1)


def _layer0_weights(e, norm_mix_g, ab_w_in, pool_w, pool_scale, q_norm_g, kv_norm_g, w_uq, w_ukv,
                    qn_g, kn_g, ab_w_out):
    w_in = ab_w_in[e]
    o3 = POOL_WIDTH + Q_RANK + KV_RANK
    kr_cols = jnp.concatenate(
        [jnp.zeros((D_MODEL, QK_NOPE), F32), w_in[:, o3:], w_in[:, o3:]], axis=1)
    w_in_p = jnp.concatenate([w_in[:, :o3], kr_cols], axis=1).astype(BF16)
    uq = w_uq[e].reshape(Q_RANK, MLA_HEADS, QK_DIM)
    uq = jnp.concatenate([uq, _swap_halves(uq[:, :, QK_NOPE:])], axis=-1)
    score_scale = QK_DIM ** -0.5 * math.log2(math.e)
    gq = qn_g[e] * score_scale
    gk = kn_g[e]
    lane = jnp.arange(2 * LANES)
    head_ones = ((lane[:, None] // LANES == lane[None, :] // LANES)
                 & (lane[:, None] % LANES < QK_DIM)).astype(BF16)
    ukv = w_ukv[e].reshape(KV_RANK, MLA_HEADS, QK_NOPE + V_DIM)
    w_k = jnp.pad(ukv[:, :, :QK_NOPE], ((0, 0), (0, 0), (0, LANES - QK_NOPE)))
    w_v = ukv[:, :, QK_NOPE:].reshape(KV_RANK, MLA_HEADS // 2, 2, V_DIM)
    zero = jnp.zeros_like(w_v[:, :, 0])
    w_v = jnp.stack([jnp.concatenate([w_v[:, :, 0], zero], axis=-1),
                     jnp.concatenate([zero, w_v[:, :, 1]], axis=-1)], axis=2)
    return {
        "norm_mix_g": norm_mix_g.reshape(1, D_MODEL),
        "w_in": w_in_p,
        "q_norm_g": q_norm_g[e].reshape(1, Q_RANK),
        "kv_norm_g": kv_norm_g[e].reshape(1, KV_RANK),
        "w_uq": uq.reshape(Q_RANK, HEAD_W).astype(BF16),
        "w_k": w_k.reshape(KV_RANK, HEAD_W).astype(BF16),
        "w_v": w_v.reshape(KV_RANK, HEAD_W).astype(BF16),
        "qn_g": jnp.pad(gq, (0, LANES - QK_DIM)).reshape(1, LANES),
        "qn_g_rope": jnp.concatenate([gq, _swap_halves(gq[QK_NOPE:])]).reshape(1, LANES),
        "kn_g": jnp.concatenate([gk, gk[QK_NOPE:]]).reshape(1, LANES),
        "head_ones": head_ones,
        "pool_w": pool_w[e].astype(BF16),
        "pool_scale": pool_scale[e].reshape(1, POOL_WIDTH),
        "w_out_pool": ab_w_out[e, :POOL_WIDTH].astype(BF16),
        "w_out_att": ab_w_out[e, POOL_WIDTH:].astype(BF16),
    }


def _rope_tables(seq_len):
    rows = seq_len // GRID_W
    row = jnp.repeat(jnp.arange(rows), GRID_W).astype(F32)
    col = jnp.tile(jnp.arange(GRID_W), rows).astype(F32)
    per_axis = QK_ROPE // 2
    inv = 1.0 / (ROPE_BASE ** (jnp.arange(0, per_axis, 2, dtype=F32) / per_axis))
    ang = jnp.concatenate([row[:, None] * inv, col[:, None] * inv], axis=-1)
    cos, sin = jnp.cos(ang), jnp.sin(ang)
    ones = jnp.ones((seq_len, QK_NOPE), F32)
    z_nope = jnp.zeros((seq_len, QK_NOPE), F32)
    z_half = jnp.zeros((seq_len, HALF), F32)
    c_tab = jnp.concatenate([ones, cos, cos, cos, cos], axis=-1)
    a_tab = jnp.concatenate([z_nope, -sin, z_half, -sin, z_half], axis=-1)
    b_tab = jnp.concatenate([z_nope, z_half, sin, z_half, sin], axis=-1)
    q_tab = jnp.concatenate([ones, cos, cos, -sin, sin], axis=-1)
    return jnp.stack([c_tab, a_tab, b_tab, q_tab])


def kernel(x_prompt, x_sample, cache_ckv, cache_krope, c, c_ctx, ada_w, ada_b, norm_mix_g, norm_ffn_g, ffn_wg, ffn_wu, ffn_wd, ab_w_in, pool_w, pool_scale, q_norm_g, kv_norm_g, w_uq, w_ukv, qn_g, kn_g, ab_w_out, gm_w_in, gm_vnorm_g, gm_ws, gm_bs, gm_w_out):
    n_ctx_b, ctx_len, _ = x_prompt.shape
    n_lat_b, lat_len, _ = x_sample.shape
    past = cache_ckv.shape[2]
    assert ctx_len & (ctx_len - 1) == 0 and lat_len & (lat_len - 1) == 0
    assert MOD_ROWS >= 1 + n_lat_b

    ctx = x_prompt.reshape(n_ctx_b * ctx_len, D_MODEL)
    lat = x_sample.reshape(n_lat_b * lat_len, D_MODEL)

    cond = jnp.concatenate(
        [c_ctx[None, :], c, jnp.zeros((MOD_ROWS - 1 - n_lat_b, D_MODEL), F32)], axis=0)
    mod_all = _ada_call(cond, ada_w, ada_b).reshape(DEPTH, MOD_ROWS, 6, D_MODEL)

    ctx_row = lambda t: 0

    def lat_row(tm):
        per = lat_len // tm
        return lambda t: 1 + t // per

    wg_all, wu_all, wd_all = ffn_wg.astype(BF16), ffn_wu.astype(BF16), ffn_wd.astype(BF16)

    new_ckv, new_krope = [], []
    for i in range(DEPTH):
        mod = mod_all[i]
        if i % 2 == 0:
            e = i // 2
            w = _layer0_weights(e, norm_mix_g[i], ab_w_in, pool_w, pool_scale, q_norm_g, kv_norm_g,
                                w_uq, w_ukv, qn_g, kn_g, ab_w_out)
            cs = _rope_tables(lat_len)
            yp_c, q_c, k_c, v_c, ckv_c, kr_c = _l0a_call(ctx, mod, ctx_row, 1024, ctx_len, w, None)
            yp_l, q_l, k_l, v_l, _, _ = _l0a_call(lat, mod, lat_row(lat_len), lat_len, lat_len, w, cs)
            kr_c2 = cache_krope[:, e].reshape(n_lat_b * past, QK_ROPE)
            kr_pad = jnp.concatenate(
                [jnp.zeros((n_lat_b * past, QK_NOPE), F32), kr_c2, kr_c2], axis=1)
            cache = _kvc_call(cache_ckv[:, e].reshape(n_lat_b * past, KV_RANK), kr_pad, w, 1024)
            ctx = _att_call(ctx, mod, lambda b: 0, ctx_len, yp_c, q_c, k_c, v_c, None, w)
            lat = _att_call(lat, mod, lambda b: 1 + b, lat_len, yp_l, q_l, k_l, v_l, cache, w)
            new_ckv.append(ckv_c.reshape(n_ctx_b, ctx_len, KV_RANK))
            new_krope.append(kr_c.reshape(n_ctx_b, ctx_len, QK_ROPE))
        else:
            o = i // 2
            w = {
                "norm_mix_g": norm_mix_g[i].reshape(1, D_MODEL),
                "w_in": gm_w_in[o].astype(BF16),
                "vnorm_g": gm_vnorm_g[o].reshape(1, C_WIDTH),
                "w_s": gm_ws[o].astype(BF16),
                "b_s": jnp.broadcast_to(gm_bs[o][:, :, None], (C_GROUPS, CHUNK, C_CH)),
                "w_out": gm_w_out[o].astype(BF16),
            }
            ctx = _gmlp_call(ctx, mod, ctx_row, GM_TM, w)
            lat = _gmlp_call(lat, mod, lat_row(GM_TM), GM_TM, w)
        ng = norm_ffn_g[i].reshape(1, D_MODEL)
        ctx = _ffn_call(ctx, mod, ctx_row, FFN_TM, ng, i, wg_all, wu_all, wd_all)
        lat = _ffn_call(lat, mod, lat_row(FFN_TM), FFN_TM, ng, i, wg_all, wu_all, wd_all)

    state_ckv = jnp.stack(new_ckv, axis=1)
    state_krope = jnp.stack(new_krope, axis=1)
    return (ctx.reshape(n_ctx_b, ctx_len, D_MODEL), lat.reshape(n_lat_b, lat_len, D_MODEL),
            state_ckv, state_krope)
```

```python
import functools
import math

import jax
import jax.numpy as jnp
from jax import lax
from jax.experimental import pallas as pl
from jax.experimental.pallas import tpu as pltpu

D_MODEL = 1024
DEPTH = 2
GRID_W = 64
POOL_WINDOWS = (2, 4, 8, 16)
POOL_GROUPS = 4
POOL_CH = 128
POOL_WIDTH = POOL_GROUPS * POOL_CH
MLA_HEADS = 8
QK_NOPE = 64
QK_ROPE = 32
QK_DIM = QK_NOPE + QK_ROPE
V_DIM = 64
Q_RANK = 384
KV_RANK = 256
MLA_WIDTH = MLA_HEADS * V_DIM
CHUNK = 128
C_GROUPS = 8
C_WIDTH = D_MODEL
C_CH = C_WIDTH // C_GROUPS
D_FF = 2816
ROPE_BASE = 10000.0
EPS = 1e-6

LANES = 128
HEAD_W = MLA_HEADS * LANES
AB_IN_PAD = POOL_WIDTH + Q_RANK + KV_RANK + LANES
MOD_ROWS = 16
VMEM_LIMIT = 56 * 1024 * 1024

F32 = jnp.float32
BF16 = jnp.bfloat16


def _rms(x):
    return x * lax.rsqrt(jnp.mean(x * x, axis=-1, keepdims=True) + EPS)


def _dot(a, b):
    return jnp.dot(a, b, preferred_element_type=F32)


def _dot_nt(a, b):
    return lax.dot_general(a, b, (((1,), (1,)), ((), ())), preferred_element_type=F32)


def _dot_tn(a, b):
    return lax.dot_general(a, b, (((0,), (0,)), ((), ())), preferred_element_type=F32)


def _silu(x):
    return x * jax.nn.sigmoid(x)


def _params(n_axes):
    return pltpu.CompilerParams(
        dimension_semantics=("arbitrary",) * n_axes, vmem_limit_bytes=VMEM_LIMIT)


def _const_spec(shape):
    zeros = (0,) * len(shape)
    return pl.BlockSpec(shape, lambda *_: zeros)


ADA_TN = 512


def _ada_kernel(cond_ref, w_ref, b_ref, o_ref):
    s = _silu(cond_ref[...]).astype(BF16)
    o_ref[0] = _dot(s, w_ref[0].astype(BF16)) + b_ref[0]


def _ada_call(cond, ada_w, ada_b):
    n = 6 * D_MODEL
    return pl.pallas_call(
        _ada_kernel,
        grid=(DEPTH, n // ADA_TN),
        in_specs=[
            pl.BlockSpec((MOD_ROWS, D_MODEL), lambda i, j: (0, 0)),
            pl.BlockSpec((1, D_MODEL, ADA_TN), lambda i, j: (i, 0, j)),
            pl.BlockSpec((1, 1, ADA_TN), lambda i, j: (i, 0, j)),
        ],
        out_specs=pl.BlockSpec((1, MOD_ROWS, ADA_TN), lambda i, j: (i, 0, j)),
        out_shape=jax.ShapeDtypeStruct((DEPTH, MOD_ROWS, n), F32),
        compiler_params=_params(2),
        name="ada_mod",
    )(cond, ada_w, ada_b.reshape(DEPTH, 1, n))


def _rope(t, cs_ref):
    return (t * cs_ref[0]
            + pltpu.roll(t, LANES - QK_ROPE // 2, axis=1) * cs_ref[1]
            + pltpu.roll(t, QK_ROPE // 2, axis=1) * cs_ref[2])


def _kv_heads(ckv_bf, kr, wk_ref, wv_ref, kng_ref, cs_ref, k_ref, v_ref):
    v_ref[...] = _dot_nt(wv_ref[...], ckv_bf).astype(BF16)
    kpre = _dot(ckv_bf, wk_ref[...])
    kg = kng_ref[...]
    krg = kr * kg
    if cs_ref is not None:
        krg = _rope(krg, cs_ref)
    lane = lax.broadcasted_iota(jnp.int32, kr.shape, 1)
    kr_ss = jnp.sum(jnp.where(lane < QK_DIM, kr * kr, 0.0), axis=-1, keepdims=True)
    for h in range(MLA_HEADS):
        kh = kpre[:, h * LANES:(h + 1) * LANES]
        r = lax.rsqrt((jnp.sum(kh * kh, axis=-1, keepdims=True) + kr_ss) * (1.0 / QK_DIM) + EPS)
        k_ref[:, h * LANES:(h + 1) * LANES] = ((kh * kg + krg) * r).astype(BF16)


def _shift_rows(a, k, pos, seq_len):
    n = a.shape[0]
    r = pltpu.roll(a, k % n, axis=0)
    src = pos - k
    ok = (src >= 0) if k > 0 else (src < seq_len)
    return jnp.where(ok, r, 0.0)


def _pool_group(ug, half, pos, seq_len):
    fw = ug
    step = 1
    while step < half:
        fw = fw + _shift_rows(fw, -step, pos, seq_len)
        step *= 2
    bk = _shift_rows(ug, 1, pos, seq_len)
    step = 1
    while step < half:
        bk = bk + _shift_rows(bk, step, pos, seq_len)
        step *= 2
    cnt = jnp.minimum(pos + half, seq_len) - jnp.maximum(pos - half, 0)
    return (fw + bk) / cnt.astype(F32) - ug


def _l0a_kernel(*refs, seq_len, use_rope):
    (x_ref, mod_ref, ng_ref, win_ref, qg_ref, kvg_ref, wuq_ref, wk_ref, wv_ref,
     qng_ref, kng_ref, pw_ref, ps_ref, ones_ref) = refs[:14]
    if use_rope:
        cs_ref = refs[14]
        outs = refs[15:]
    else:
        cs_ref = None
        outs = refs[14:]
    yp_ref, q_ref, k_ref, v_ref, ckv_ref, kr_ref = outs

    tm = x_ref.shape[0]
    m = mod_ref[0]
    h = _rms(x_ref[...]) * ng_ref[...] * (1.0 + m[1:2]) + m[0:1]
    z = _dot(h.astype(BF16), win_ref[...])
    o1, o2, o3 = POOL_WIDTH, POOL_WIDTH + Q_RANK, POOL_WIDTH + Q_RANK + KV_RANK

    ckv = _rms(z[:, o2:o3]) * kvg_ref[...]
    kr = z[:, o3:]
    ckv_ref[...] = ckv
    kr_ref[...] = kr[:, QK_NOPE:QK_DIM]
    _kv_heads(ckv.astype(BF16), kr, wk_ref, wv_ref, kng_ref, cs_ref, k_ref, v_ref)

    cqn = (_rms(z[:, o1:o2]) * qg_ref[...]).astype(BF16)
    qf = _dot(cqn, wuq_ref[...])
    qt = qng_ref[...]
    if use_rope:
        qt = qt * cs_ref[3]
    for j in range(MLA_HEADS // 2):
        qp = qf[:, 2 * j * LANES:(2 * j + 2) * LANES]
        sq = qp * qp
        hi = sq.astype(BF16)
        lo = (sq - hi.astype(F32)).astype(BF16)
        ss = _dot(hi, ones_ref[...]) + _dot(lo, ones_ref[...])
        qn = qp * lax.rsqrt(ss * (1.0 / QK_DIM) + EPS)
        for hh in range(2):
            hd = 2 * j + hh
            q_ref[:, hd * LANES:(hd + 1) * LANES] = (qn[:, hh * LANES:(hh + 1) * LANES] * qt).astype(BF16)

    pos = lax.broadcasted_iota(jnp.int32, (tm, POOL_CH), 0) & (seq_len - 1)
    for g, w in enumerate(POOL_WINDOWS):
        ug = z[:, g * POOL_CH:(g + 1) * POOL_CH]
        p = _pool_group(ug, w // 2, pos, seq_len)
        y = _dot(p.astype(BF16), pw_ref[g]) * ps_ref[:, g * POOL_CH:(g + 1) * POOL_CH]
        yp_ref[:, g * POOL_CH:(g + 1) * POOL_CH] = y.astype(BF16)


def _l0a_call(x, mod, mod_row, tm, seq_len, w, cs):
    n_tok = x.shape[0]
    use_rope = cs is not None
    row_spec = lambda width: pl.BlockSpec((tm, width), lambda t: (t, 0))
    in_specs = [
        row_spec(D_MODEL),
        pl.BlockSpec((1, 6, D_MODEL), lambda t: (mod_row(t), 0, 0)),
        _const_spec((1, D_MODEL)),
        _const_spec((D_MODEL, AB_IN_PAD)),
        _const_spec((1, Q_RANK)),
        _const_spec((1, KV_RANK)),
        _const_spec((Q_RANK, HEAD_W)),
        _const_spec((KV_RANK, HEAD_W)),
        _const_spec((HEAD_W, KV_RANK)),
        _const_spec((1, LANES)),
        _const_spec((1, LANES)),
        _const_spec((POOL_GROUPS, POOL_CH, POOL_CH)),
        _const_spec((1, POOL_WIDTH)),
        _const_spec((2 * LANES, 2 * LANES)),
    ]
    args = [x, mod, w["norm_mix_g"], w["w_in"], w["q_norm_g"], w["kv_norm_g"], w["w_uq"],
            w["w_k"], w["w_vt"], w["qn_g_rope"] if use_rope else w["qn_g"], w["kn_g"],
            w["pool_w"], w["pool_scale"], w["head_ones"]]
    if use_rope:
        assert tm == seq_len
        in_specs.append(_const_spec((4, seq_len, LANES)))
        args.append(cs)
    out_shape = [
        jax.ShapeDtypeStruct((n_tok, POOL_WIDTH), BF16),
        jax.ShapeDtypeStruct((n_tok, HEAD_W), BF16),
        jax.ShapeDtypeStruct((n_tok, HEAD_W), BF16),
        jax.ShapeDtypeStruct((HEAD_W, n_tok), BF16),
        jax.ShapeDtypeStruct((n_tok, KV_RANK), F32),
        jax.ShapeDtypeStruct((n_tok, QK_ROPE), F32),
    ]
    out_specs = [row_spec(POOL_WIDTH), row_spec(HEAD_W), row_spec(HEAD_W),
                 pl.BlockSpec((HEAD_W, tm), lambda t: (0, t)),
                 row_spec(KV_RANK), row_spec(QK_ROPE)]
    return pl.pallas_call(
        functools.partial(_l0a_kernel, seq_len=seq_len, use_rope=use_rope),
        grid=(n_tok // tm,),
        in_specs=in_specs,
        out_specs=out_specs,
        out_shape=out_shape,
        compiler_params=_params(1),
        name="l0_front_rope" if use_rope else "l0_front",
    )(*args)


def _kvc_kernel(ckv_ref, kr_ref, wk_ref, wv_ref, kng_ref, k_ref, v_ref):
    _kv_heads(ckv_ref[...].astype(BF16), kr_ref[...], wk_ref, wv_ref, kng_ref, None, k_ref, v_ref)


def _kvc_call(ckv, kr_pad, w, tm):
    n_tok = ckv.shape[0]
    row_spec = lambda width: pl.BlockSpec((tm, width), lambda t: (t, 0))
    return pl.pallas_call(
        _kvc_kernel,
        grid=(n_tok // tm,),
        in_specs=[row_spec(KV_RANK), row_spec(LANES), _const_spec((KV_RANK, HEAD_W)),
                  _const_spec((HEAD_W, KV_RANK)), _const_spec((1, LANES))],
        out_specs=[row_spec(HEAD_W), pl.BlockSpec((HEAD_W, tm), lambda t: (0, t))],
        out_shape=[jax.ShapeDtypeStruct((n_tok, HEAD_W), BF16),
                   jax.ShapeDtypeStruct((HEAD_W, n_tok), BF16)],
        compiler_params=_params(1),
        name="cache_kv",
    )(ckv, kr_pad, w["w_k"], w["w_vt"], w["kn_g"])


ATT_TQ = 256
ATT_KC = 1024
ATT_AHEAD = 3


def _att_kernel(*refs, has_cache):
    if has_cache:
        (x_ref, mod_ref, yp_ref, q_ref, k_ref, v_ref, kc_ref, vc_ref,
         wop_ref, woa_ref, o_ref, ya_ref) = refs
    else:
        x_ref, mod_ref, yp_ref, q_ref, k_ref, v_ref, wop_ref, woa_ref, o_ref, ya_ref = refs
        kc_ref = vc_ref = None

    sources = [(k_ref, v_ref)] if not has_cache else [(kc_ref, vc_ref), (k_ref, v_ref)]
    chunks = []
    for ks_ref, vs_ref in sources:
        n_keys = ks_ref.shape[0]
        kc = min(ATT_KC, n_keys)
        assert n_keys % kc == 0
        chunks += [(ks_ref, vs_ref, slice(c * kc, (c + 1) * kc)) for c in range(n_keys // kc)]
    items = [(hd, ch) for hd in range(MLA_HEADS) for ch in chunks]

    def scores(item):
        hd, (ks_ref, _, keys) = item
        sl = slice(hd * LANES, (hd + 1) * LANES)
        return _dot_nt(ks_ref[keys, sl], q_ref[:, sl])

    pending = [scores(it) for it in items[:ATT_AHEAD]]
    state = {}
    for n, (hd, (_, vs_ref, keys)) in enumerate(items):
        if n + ATT_AHEAD < len(items):
            pending.append(scores(items[n + ATT_AHEAD]))
        s = pending.pop(0)
        sl = slice(hd * LANES, (hd + 1) * LANES)
        cmax = jnp.max(s, axis=0, keepdims=True)
        if hd not in state:
            mx = cmax
            p = jnp.exp2(s - mx)
            den = jnp.sum(p, axis=0, keepdims=True)
            acc = _dot(vs_ref[sl, keys], p.astype(BF16))
        else:
            mx, den, acc = state[hd]
            new = jnp.maximum(mx, cmax)
            alpha = jnp.exp2(mx - new)
            p = jnp.exp2(s - new)
            den = alpha * den + jnp.sum(p, axis=0, keepdims=True)
            acc = alpha * acc + _dot(vs_ref[sl, keys], p.astype(BF16))
            mx = new
        state[hd] = (mx, den, acc)
        if hd % 2 == 1 and keys.stop == vs_ref.shape[1] and vs_ref is v_ref:
            j = hd // 2
            pair = state[hd - 1][2] / state[hd - 1][1] + acc / den
            ya_ref[j * LANES:(j + 1) * LANES, :] = pair.astype(BF16)

    y = _dot(yp_ref[...], wop_ref[...]) + _dot_tn(ya_ref[...], woa_ref[...])
    o_ref[...] = x_ref[...] + mod_ref[0][2:3] * y


def _att_call(x, mod, mod_row, seq_len, yp, q, k, v, cache, w):
    n_tok = x.shape[0]
    n_b = n_tok // seq_len
    n_q = seq_len // ATT_TQ
    qrow = lambda width: pl.BlockSpec((ATT_TQ, width), lambda b, i: (b * n_q + i, 0))
    brow = lambda rows: pl.BlockSpec((rows, HEAD_W), lambda b, i: (b, 0))
    bcol = lambda cols: pl.BlockSpec((HEAD_W, cols), lambda b, i: (0, b))
    in_specs = [
        qrow(D_MODEL),
        pl.BlockSpec((1, 6, D_MODEL), lambda b, i: (mod_row(b), 0, 0)),
        qrow(POOL_WIDTH), qrow(HEAD_W), brow(seq_len), bcol(seq_len),
    ]
    args = [x, mod, yp, q, k, v]
    if cache is not None:
        kc, vc = cache
        past = kc.shape[0] // n_b
        in_specs += [brow(past), bcol(past)]
        args += [kc, vc]
    in_specs += [_const_spec((POOL_WIDTH, D_MODEL)), _const_spec((MLA_WIDTH, D_MODEL))]
    args += [w["w_out_pool"], w["w_out_att"]]
    return pl.pallas_call(
        functools.partial(_att_kernel, has_cache=cache is not None),
        grid=(n_b, n_q),
        in_specs=in_specs,
        out_specs=qrow(D_MODEL),
        out_shape=jax.ShapeDtypeStruct((n_tok, D_MODEL), F32),
        scratch_shapes=[pltpu.VMEM((MLA_WIDTH, ATT_TQ), BF16)],
        compiler_params=_params(2),
        name="att_out_cache" if cache is not None else "att_out",
    )(*args)


FFN_TM = 1024
FFN_SUB = 256


def _ffn_kernel(x_ref, mod_ref, ng_ref, wg_ref, wu_ref, wd_ref, o_ref):
    m = mod_ref[0]
    scale = ng_ref[...] * (1.0 + m[4:5])
    for s in range(x_ref.shape[0] // FFN_SUB):
        rows = slice(s * FFN_SUB, (s + 1) * FFN_SUB)
        x = x_ref[rows, :]
        h = (_rms(x) * scale + m[3:4]).astype(BF16)
        g = _dot(h, wg_ref[0])
        u = _dot(h, wu_ref[0])
        d = _dot((_silu(g) * u).astype(BF16), wd_ref[0])
        o_ref[rows, :] = x + m[5:6] * d


def _ffn_call(x, mod, mod_row, tm, ng, layer, wg, wu, wd):
    n_tok = x.shape[0]
    resident = lambda shape: pl.BlockSpec((1,) + shape, lambda t: (layer, 0, 0),
                                          pipeline_mode=pl.Buffered(1))
    return pl.pallas_call(
        _ffn_kernel,
        grid=(n_tok // tm,),
        in_specs=[
            pl.BlockSpec((tm, D_MODEL), lambda t: (t, 0)),
            pl.BlockSpec((1, 6, D_MODEL), lambda t: (mod_row(t), 0, 0)),
            _const_spec((1, D_MODEL)),
            resident((D_MODEL, D_FF)), resident((D_MODEL, D_FF)), resident((D_FF, D_MODEL)),
        ],
        out_specs=pl.BlockSpec((tm, D_MODEL), lambda t: (t, 0)),
        out_shape=jax.ShapeDtypeStruct((n_tok, D_MODEL), F32),
        compiler_params=_params(1),
        name="ffn",
    )(x, mod, ng, wg, wu, wd)


GM_TM = 512


def _gmlp_kernel(x_ref, mod_ref, ng_ref, win_ref, vg_ref, ws_ref, bs_ref, wout_ref, o_ref, gs_ref):
    x = x_ref[...]
    m = mod_ref[0]
    h = (_rms(x) * ng_ref[...] * (1.0 + m[1:2]) + m[0:1]).astype(BF16)
    z = _dot(h, win_ref[...])
    vn = (_rms(z[:, C_WIDTH:]) * vg_ref[...]).astype(BF16)
    n_chunks = x.shape[0] // CHUNK
    for g in range(C_GROUPS):
        cols = slice(g * C_CH, (g + 1) * C_CH)
        rhs = jnp.concatenate(
            [vn[n * CHUNK:(n + 1) * CHUNK, cols] for n in range(n_chunks)], axis=1)
        s = _dot(ws_ref[g], rhs)
        for n in range(n_chunks):
            rows = slice(n * CHUNK, (n + 1) * CHUNK)
            sn = s[:, n * C_CH:(n + 1) * C_CH] + bs_ref[g]
            gs_ref[rows, cols] = (z[rows, cols] * sn).astype(BF16)
    o_ref[...] = x + m[2:3] * _dot(gs_ref[...], wout_ref[...])


def _gmlp_call(x, mod, mod_row, tm, w):
    n_tok = x.shape[0]
    return pl.pallas_call(
        _gmlp_kernel,
        grid=(n_tok // tm,),
        in_specs=[
            pl.BlockSpec((tm, D_MODEL), lambda t: (t, 0)),
            pl.BlockSpec((1, 6, D_MODEL), lambda t: (mod_row(t), 0, 0)),
            _const_spec((1, D_MODEL)),
            _const_spec((D_MODEL, 2 * C_WIDTH)),
            _const_spec((1, C_WIDTH)),
            _const_spec((C_GROUPS, CHUNK, CHUNK)),
            _const_spec((C_GROUPS, CHUNK, C_CH)),
            _const_spec((C_WIDTH, D_MODEL)),
        ],
        out_specs=pl.BlockSpec((tm, D_MODEL), lambda t: (t, 0)),
        out_shape=jax.ShapeDtypeStruct((n_tok, D_MODEL), F32),
        scratch_shapes=[pltpu.VMEM((tm, C_WIDTH), BF16)],
        compiler_params=_params(1),
        name="gmlp",
    )(x, mod, w["norm_mix_g"], w["w_in"], w["vnorm_g"], w["w_s"], w["b_s"], w["w_out"])


HALF = QK_ROPE // 2


def _swap_halves(t):
    return jnp.concatenate([t[..., HALF:], t[..., :HALF]], axis=-1)


def _layer0_weights(e, norm_mix_g, ab_w_in, pool_w, pool_scale, q_norm_g, kv_norm_g, w_uq, w_ukv,
                    qn_g, kn_g, ab_w_out):
    w_in = ab_w_in[e]
    o3 = POOL_WIDTH + Q_RANK + KV_RANK
    kr_cols = jnp.concatenate(
        [jnp.zeros((D_MODEL, QK_NOPE), F32), w_in[:, o3:], w_in[:, o3:]], axis=1)
    w_in_p = jnp.concatenate([w_in[:, :o3], kr_cols], axis=1).astype(BF16)
    uq = w_uq[e].reshape(Q_RANK, MLA_HEADS, QK_DIM)
    uq = jnp.concatenate([uq, _swap_halves(uq[:, :, QK_NOPE:])], axis=-1)
    score_scale = QK_DIM ** -0.5 * math.log2(math.e)
    gq = qn_g[e] * score_scale
    gk = kn_g[e]
    lane = jnp.arange(2 * LANES)
    head_ones = ((lane[:, None] // LANES == lane[None, :] // LANES)
                 & (lane[:, None] % LANES < QK_DIM)).astype(BF16)
    ukv = w_ukv[e].reshape(KV_RANK, MLA_HEADS, QK_NOPE + V_DIM)
    w_k = jnp.pad(ukv[:, :, :QK_NOPE], ((0, 0), (0, 0), (0, LANES - QK_NOPE)))
    w_v = ukv[:, :, QK_NOPE:].reshape(KV_RANK, MLA_HEADS // 2, 2, V_DIM)
    zero = jnp.zeros_like(w_v[:, :, 0])
    w_v = jnp.stack([jnp.concatenate([w_v[:, :, 0], zero], axis=-1),
                     jnp.concatenate([zero, w_v[:, :, 1]], axis=-1)], axis=2)
    return {
        "norm_mix_g": norm_mix_g.reshape(1, D_MODEL),
        "w_in": w_in_p,
        "q_norm_g": q_norm_g[e].reshape(1, Q_RANK),
        "kv_norm_g": kv_norm_g[e].reshape(1, KV_RANK),
        "w_uq": uq.reshape(Q_RANK, HEAD_W).astype(BF16),
        "w_k": w_k.reshape(KV_RANK, HEAD_W).astype(BF16),
        "w_vt": w_v.reshape(KV_RANK, HEAD_W).T.astype(BF16),
        "qn_g": jnp.pad(gq, (0, LANES - QK_DIM)).reshape(1, LANES),
        "qn_g_rope": jnp.concatenate([gq, _swap_halves(gq[QK_NOPE:])]).reshape(1, LANES),
        "kn_g": jnp.concatenate([gk, gk[QK_NOPE:]]).reshape(1, LANES),
        "head_ones": head_ones,
        "pool_w": pool_w[e].astype(BF16),
        "pool_scale": pool_scale[e].reshape(1, POOL_WIDTH),
        "w_out_pool": ab_w_out[e, :POOL_WIDTH].astype(BF16),
        "w_out_att": ab_w_out[e, POOL_WIDTH:].astype(BF16),
    }


def _rope_tables(seq_len):
    rows = seq_len // GRID_W
    row = jnp.repeat(jnp.arange(rows), GRID_W).astype(F32)
    col = jnp.tile(jnp.arange(GRID_W), rows).astype(F32)
    per_axis = QK_ROPE // 2
    inv = 1.0 / (ROPE_BASE ** (jnp.arange(0, per_axis, 2, dtype=F32) / per_axis))
    ang = jnp.concatenate([row[:, None] * inv, col[:, None] * inv], axis=-1)
    cos, sin = jnp.cos(ang), jnp.sin(ang)
    ones = jnp.ones((seq_len, QK_NOPE), F32)
    z_nope = jnp.zeros((seq_len, QK_NOPE), F32)
    z_half = jnp.zeros((seq_len, HALF), F32)
    c_tab = jnp.concatenate([ones, cos, cos, cos, cos], axis=-1)
    a_tab = jnp.concatenate([z_nope, -sin, z_half, -sin, z_half], axis=-1)
    b_tab = jnp.concatenate([z_nope, z_half, sin, z_half, sin], axis=-1)
    q_tab = jnp.concatenate([ones, cos, cos, -sin, sin], axis=-1)
    return jnp.stack([c_tab, a_tab, b_tab, q_tab])


def kernel(x_prompt, x_sample, cache_ckv, cache_krope, c, c_ctx, ada_w, ada_b, norm_mix_g, norm_ffn_g, ffn_wg, ffn_wu, ffn_wd, ab_w_in, pool_w, pool_scale, q_norm_g, kv_norm_g, w_uq, w_ukv, qn_g, kn_g, ab_w_out, gm_w_in, gm_vnorm_g, gm_ws, gm_bs, gm_w_out):
    n_ctx_b, ctx_len, _ = x_prompt.shape
    n_lat_b, lat_len, _ = x_sample.shape
    past = cache_ckv.shape[2]
    assert ctx_len & (ctx_len - 1) == 0 and lat_len & (lat_len - 1) == 0
    assert MOD_ROWS >= 1 + n_lat_b

    ctx = x_prompt.reshape(n_ctx_b * ctx_len, D_MODEL)
    lat = x_sample.reshape(n_lat_b * lat_len, D_MODEL)

    cond = jnp.concatenate(
        [c_ctx[None, :], c, jnp.zeros((MOD_ROWS - 1 - n_lat_b, D_MODEL), F32)], axis=0)
    mod_all = _ada_call(cond, ada_w, ada_b).reshape(DEPTH, MOD_ROWS, 6, D_MODEL)

    ctx_row = lambda t: 0

    def lat_row(tm):
        per = lat_len // tm
        return lambda t: 1 + t // per

    wg_all, wu_all, wd_all = ffn_wg.astype(BF16), ffn_wu.astype(BF16), ffn_wd.astype(BF16)

    new_ckv, new_krope = [], []
    for i in range(DEPTH):
        mod = mod_all[i]
        if i % 2 == 0:
            e = i // 2
            w = _layer0_weights(e, norm_mix_g[i], ab_w_in, pool_w, pool_scale, q_norm_g, kv_norm_g,
                                w_uq, w_ukv, qn_g, kn_g, ab_w_out)
            cs = _rope_tables(lat_len)
            yp_c, q_c, k_c, v_c, ckv_c, kr_c = _l0a_call(ctx, mod, ctx_row, 1024, ctx_len, w, None)
            yp_l, q_l, k_l, v_l, _, _ = _l0a_call(lat, mod, lat_row(lat_len), lat_len, lat_len, w, cs)
            kr_c2 = cache_krope[:, e].reshape(n_lat_b * past, QK_ROPE)
            kr_pad = jnp.concatenate(
                [jnp.zeros((n_lat_b * past, QK_NOPE), F32), kr_c2, kr_c2], axis=1)
            cache = _kvc_call(cache_ckv[:, e].reshape(n_lat_b * past, KV_RANK), kr_pad, w, 1024)
            ctx = _att_call(ctx, mod, lambda b: 0, ctx_len, yp_c, q_c, k_c, v_c, None, w)
            lat = _att_call(lat, mod, lambda b: 1 + b, lat_len, yp_l, q_l, k_l, v_l, cache, w)
            new_ckv.append(ckv_c.reshape(n_ctx_b, ctx_len, KV_RANK))
            new_krope.append(kr_c.reshape(n_ctx_b, ctx_len, QK_ROPE))
        else:
            o = i // 2
            w = {
                "norm_mix_g": norm_mix_g[i].reshape(1, D_MODEL),
                "w_in": gm_w_in[o].astype(BF16),
                "vnorm_g": gm_vnorm_g[o].reshape(1, C_WIDTH),
                "w_s": gm_ws[o].astype(BF16),
                "b_s": jnp.broadcast_to(gm_bs[o][:, :, None], (C_GROUPS, CHUNK, C_CH)),
                "w_out": gm_w_out[o].astype(BF16),
            }
            ctx = _gmlp_call(ctx, mod, ctx_row, GM_TM, w)
            lat = _gmlp_call(lat, mod, lat_row(GM_TM), GM_TM, w)
        ng = norm_ffn_g[i].reshape(1, D_MODEL)
        ctx = _ffn_call(ctx, mod, ctx_row, FFN_TM, ng, i, wg_all, wu_all, wd_all)
        lat = _ffn_call(lat, mod, lat_row(FFN_TM), FFN_TM, ng, i, wg_all, wu_all, wd_all)

    state_ckv = jnp.stack(new_ckv, axis=1)
    state_krope = jnp.stack(new_krope, axis=1)
    return (ctx.reshape(n_ctx_b, ctx_len, D_MODEL), lat.reshape(n_lat_b, lat_len, D_MODEL),
            state_ckv, state_krope)
```

```python
import functools
import math

import numpy as np
import jax
import jax.numpy as jnp
from jax import lax
from jax.experimental import pallas as pl
from jax.experimental.pallas import tpu as pltpu

D_MODEL = 1024
DEPTH = 2
GRID_W = 64
POOL_WINDOWS = (2, 4, 8, 16)
POOL_GROUPS = 4
POOL_CH = 128
POOL_WIDTH = POOL_GROUPS * POOL_CH
MLA_HEADS = 8
QK_NOPE = 64
QK_ROPE = 32
QK_DIM = QK_NOPE + QK_ROPE
HALF = QK_ROPE // 2
V_DIM = 64
Q_RANK = 384
KV_RANK = 256
MLA_WIDTH = MLA_HEADS * V_DIM
CHUNK = 128
C_GROUPS = 8
C_WIDTH = D_MODEL
C_CH = C_WIDTH // C_GROUPS
D_FF = 2816
ROPE_BASE = 10000.0
EPS = 1e-6

LANES = 128
HEAD_W = MLA_HEADS * LANES
AB_IN_PAD = POOL_WIDTH + Q_RANK + KV_RANK + LANES
MOD_ROWS = 16
VMEM_LIMIT = 58 * 1024 * 1024
W_STEPS = 8

F32 = jnp.float32
BF16 = jnp.bfloat16


def _rms(x):
    return x * lax.rsqrt(jnp.mean(x * x, axis=-1, keepdims=True) + EPS)


def _dot(a, b):
    return jnp.dot(a, b, preferred_element_type=F32)


def _dot_nt(a, b):
    return lax.dot_general(a, b, (((1,), (1,)), ((), ())), preferred_element_type=F32)


def _dot_tn(a, b):
    return lax.dot_general(a, b, (((0,), (0,)), ((), ())), preferred_element_type=F32)


def _silu(x):
    return x * jax.nn.sigmoid(x)


def _params(n_axes):
    return pltpu.CompilerParams(
        dimension_semantics=("arbitrary",) * n_axes, vmem_limit_bytes=VMEM_LIMIT)


def _const_spec(shape):
    zeros = (0,) * len(shape)
    return pl.BlockSpec(shape, lambda *_: zeros)


def _mod_spec(layer):
    return pl.BlockSpec((1, 6, MOD_ROWS, D_MODEL), lambda *_: (layer, 0, 0, 0))


def _mod(mod_ref, term, row):
    return mod_ref[0, term, pl.ds(row, 1), :]


def _ada_kernel(cond_ref, w_ref, b_ref, o_ref):
    s = _silu(cond_ref[...]).astype(BF16)
    o_ref[0, 0] = _dot(s, w_ref[0].astype(BF16)) + b_ref[0]


def _ada_call(cond, ada_w, ada_b):
    return pl.pallas_call(
        _ada_kernel,
        grid=(DEPTH, 6),
        in_specs=[
            pl.BlockSpec((MOD_ROWS, D_MODEL), lambda i, j: (0, 0)),
            pl.BlockSpec((1, D_MODEL, D_MODEL), lambda i, j: (i, 0, j)),
            pl.BlockSpec((1, 1, D_MODEL), lambda i, j: (i, 0, j)),
        ],
        out_specs=pl.BlockSpec((1, 1, MOD_ROWS, D_MODEL), lambda i, j: (i, j, 0, 0)),
        out_shape=jax.ShapeDtypeStruct((DEPTH, 6, MOD_ROWS, D_MODEL), F32),
        compiler_params=_params(2),
        name="ada_mod",
    )(cond, ada_w, ada_b.reshape(DEPTH, 1, 6 * D_MODEL))


def _rope(t, cs_ref):
    return (t * cs_ref[0]
            + pltpu.roll(t, LANES - HALF, axis=1) * cs_ref[1]
            + pltpu.roll(t, HALF, axis=1) * cs_ref[2])


def _kv_heads(ckv_bf, kr, wk_ref, wv_ref, kng_ref, cs_ref, k_ref, v_ref):
    v_ref[...] = _dot_nt(wv_ref[...], ckv_bf).astype(BF16)
    kpre = _dot(ckv_bf, wk_ref[...])
    kg = kng_ref[...]
    krg = kr * kg
    if cs_ref is not None:
        krg = _rope(krg, cs_ref)
    lane = lax.broadcasted_iota(jnp.int32, kr.shape, 1)
    kr_ss = jnp.sum(jnp.where(lane < QK_DIM, kr * kr, 0.0), axis=-1, keepdims=True)
    for h in range(MLA_HEADS):
        kh = kpre[:, h * LANES:(h + 1) * LANES]
        r = lax.rsqrt((jnp.sum(kh * kh, axis=-1, keepdims=True) + kr_ss) * (1.0 / QK_DIM) + EPS)
        k_ref[:, h * LANES:(h + 1) * LANES] = ((kh * kg + krg) * r).astype(BF16)


def _shift_rows(a, k, pos, seq_len):
    n = a.shape[0]
    r = pltpu.roll(a, k % n, axis=0)
    src = pos - k
    ok = (src >= 0) if k > 0 else (src < seq_len)
    return jnp.where(ok, r, 0.0)


def _pool_group(ug, half, pos, seq_len):
    fw = ug
    step = 1
    while step < half:
        fw = fw + _shift_rows(fw, -step, pos, seq_len)
        step *= 2
    bk = _shift_rows(ug, 1, pos, seq_len)
    step = 1
    while step < half:
        bk = bk + _shift_rows(bk, step, pos, seq_len)
        step *= 2
    cnt = jnp.minimum(pos + half, seq_len) - jnp.maximum(pos - half, 0)
    return (fw + bk) / cnt.astype(F32) - ug


def _l0a_kernel(*refs, seq_len, use_rope, mod_row):
    (x_ref, mod_ref, ng_ref, win_ref, qg_ref, kvg_ref, wuq_ref, wk_ref, wv_ref,
     qng_ref, kng_ref, pw_ref, ps_ref, ones_ref) = refs[:14]
    if use_rope:
        cs_ref = refs[14]
        outs = refs[15:]
    else:
        cs_ref = None
        outs = refs[14:]
    yp_ref, q_ref, k_ref, v_ref, ckv_ref, kr_ref = outs

    tm = x_ref.shape[0]
    row = mod_row(pl.program_id(0))
    scale = ng_ref[...] * (1.0 + _mod(mod_ref, 1, row))
    h = _rms(x_ref[...]) * scale + _mod(mod_ref, 0, row)
    z = _dot(h.astype(BF16), win_ref[...])
    o1, o2, o3 = POOL_WIDTH, POOL_WIDTH + Q_RANK, POOL_WIDTH + Q_RANK + KV_RANK

    ckv = _rms(z[:, o2:o3]) * kvg_ref[...]
    kr = z[:, o3:]
    ckv_ref[...] = ckv
    kr_ref[...] = kr[:, QK_NOPE:QK_DIM]
    _kv_heads(ckv.astype(BF16), kr, wk_ref, wv_ref, kng_ref, cs_ref, k_ref, v_ref)

    cqn = (_rms(z[:, o1:o2]) * qg_ref[...]).astype(BF16)
    qf = _dot(cqn, wuq_ref[...])
    qt = qng_ref[...]
    if use_rope:
        qt = qt * cs_ref[3]
    for j in range(MLA_HEADS // 2):
        qp = qf[:, 2 * j * LANES:(2 * j + 2) * LANES]
        sq = qp * qp
        hi = sq.astype(BF16)
        lo = (sq - hi.astype(F32)).astype(BF16)
        ss = _dot(hi, ones_ref[...]) + _dot(lo, ones_ref[...])
        qn = qp * lax.rsqrt(ss * (1.0 / QK_DIM) + EPS)
        for hh in range(2):
            hd = 2 * j + hh
            q_ref[:, hd * LANES:(hd + 1) * LANES] = (qn[:, hh * LANES:(hh + 1) * LANES] * qt).astype(BF16)

    pos = lax.broadcasted_iota(jnp.int32, (tm, POOL_CH), 0) & (seq_len - 1)
    for g, w in enumerate(POOL_WINDOWS):
        ug = z[:, g * POOL_CH:(g + 1) * POOL_CH]
        p = _pool_group(ug, w // 2, pos, seq_len)
        y = _dot(p.astype(BF16), pw_ref[g]) * ps_ref[:, g * POOL_CH:(g + 1) * POOL_CH]
        yp_ref[:, g * POOL_CH:(g + 1) * POOL_CH] = y.astype(BF16)


def _l0a_call(x, mod_all, layer, mod_row, tm, seq_len, w, cs):
    n_tok = x.shape[0]
    use_rope = cs is not None
    row_spec = lambda width: pl.BlockSpec((tm, width), lambda t: (t, 0))
    in_specs = [
        row_spec(D_MODEL),
        _mod_spec(layer),
        _const_spec((1, D_MODEL)),
        _const_spec((D_MODEL, AB_IN_PAD)),
        _const_spec((1, Q_RANK)),
        _const_spec((1, KV_RANK)),
        _const_spec((Q_RANK, HEAD_W)),
        _const_spec((KV_RANK, HEAD_W)),
        _const_spec((HEAD_W, KV_RANK)),
        _const_spec((1, LANES)),
        _const_spec((1, LANES)),
        _const_spec((POOL_GROUPS, POOL_CH, POOL_CH)),
        _const_spec((1, POOL_WIDTH)),
        _const_spec((2 * LANES, 2 * LANES)),
    ]
    args = [x, mod_all, w["norm_mix_g"], w["w_in"], w["q_norm_g"], w["kv_norm_g"], w["w_uq"],
            w["w_k"], w["w_vt"], w["qn_g_rope"] if use_rope else w["qn_g"], w["kn_g"],
            w["pool_w"], w["pool_scale"], w["head_ones"]]
    if use_rope:
        assert tm == seq_len
        in_specs.append(_const_spec((4, seq_len, LANES)))
        args.append(cs)
    out_shape = [
        jax.ShapeDtypeStruct((n_tok, POOL_WIDTH), BF16),
        jax.ShapeDtypeStruct((n_tok, HEAD_W), BF16),
        jax.ShapeDtypeStruct((n_tok, HEAD_W), BF16),
        jax.ShapeDtypeStruct((HEAD_W, n_tok), BF16),
        jax.ShapeDtypeStruct((n_tok, KV_RANK), F32),
        jax.ShapeDtypeStruct((n_tok, QK_ROPE), F32),
    ]
    out_specs = [row_spec(POOL_WIDTH), row_spec(HEAD_W), row_spec(HEAD_W),
                 pl.BlockSpec((HEAD_W, tm), lambda t: (0, t)),
                 row_spec(KV_RANK), row_spec(QK_ROPE)]
    return pl.pallas_call(
        functools.partial(_l0a_kernel, seq_len=seq_len, use_rope=use_rope, mod_row=mod_row),
        grid=(n_tok // tm,),
        in_specs=in_specs,
        out_specs=out_specs,
        out_shape=out_shape,
        compiler_params=_params(1),
        name="l0_front_rope" if use_rope else "l0_front",
    )(*args)


def _kvc_kernel(ckv_ref, kr_ref, wk_ref, wv_ref, kng_ref, k_ref, v_ref):
    _kv_heads(ckv_ref[...].astype(BF16), kr_ref[...], wk_ref, wv_ref, kng_ref, None, k_ref, v_ref)


def _kvc_call(ckv, kr_pad, w, tm):
    n_tok = ckv.shape[0]
    row_spec = lambda width: pl.BlockSpec((tm, width), lambda t: (t, 0))
    return pl.pallas_call(
        _kvc_kernel,
        grid=(n_tok // tm,),
        in_specs=[row_spec(KV_RANK), row_spec(LANES), _const_spec((KV_RANK, HEAD_W)),
                  _const_spec((HEAD_W, KV_RANK)), _const_spec((1, LANES))],
        out_specs=[row_spec(HEAD_W), pl.BlockSpec((HEAD_W, tm), lambda t: (0, t))],
        out_shape=[jax.ShapeDtypeStruct((n_tok, HEAD_W), BF16),
                   jax.ShapeDtypeStruct((HEAD_W, n_tok), BF16)],
        compiler_params=_params(1),
        name="cache_kv",
    )(ckv, kr_pad, w["w_k"], w["w_vt"], w["kn_g"])


ATT_TQ = 256
ATT_KC = 1024
ATT_AHEAD = 3


def _att_kernel(*refs, has_cache, mod_row):
    if has_cache:
        (x_ref, mod_ref, yp_ref, q_ref, k_ref, v_ref, kc_ref, vc_ref,
         wop_ref, woa_ref, _, o_ref, ya_ref) = refs
    else:
        x_ref, mod_ref, yp_ref, q_ref, k_ref, v_ref, wop_ref, woa_ref, o_ref, ya_ref = refs
        kc_ref = vc_ref = None

    sources = [(k_ref, v_ref)] if not has_cache else [(kc_ref, vc_ref), (k_ref, v_ref)]
    chunks = []
    for ks_ref, vs_ref in sources:
        n_keys = ks_ref.shape[0]
        kc = min(ATT_KC, n_keys)
        assert n_keys % kc == 0
        chunks += [(ks_ref, vs_ref, slice(c * kc, (c + 1) * kc)) for c in range(n_keys // kc)]
    items = [(hd, ch) for hd in range(MLA_HEADS) for ch in chunks]

    def scores(item):
        hd, (ks_ref, _, keys) = item
        sl = slice(hd * LANES, (hd + 1) * LANES)
        return _dot_nt(ks_ref[keys, sl], q_ref[:, sl])

    pending = [scores(it) for it in items[:ATT_AHEAD]]
    state = {}
    for n, (hd, (_, vs_ref, keys)) in enumerate(items):
        if n + ATT_AHEAD < len(items):
            pending.append(scores(items[n + ATT_AHEAD]))
        s = pending.pop(0)
        sl = slice(hd * LANES, (hd + 1) * LANES)
        cmax = jnp.max(s, axis=0, keepdims=True)
        if hd not in state:
            mx = cmax
            p = jnp.exp2(s - mx)
            den = jnp.sum(p, axis=0, keepdims=True)
            acc = _dot(vs_ref[sl, keys], p.astype(BF16))
        else:
            mx, den, acc = state[hd]
            new = jnp.maximum(mx, cmax)
            alpha = jnp.exp2(mx - new)
            p = jnp.exp2(s - new)
            den = alpha * den + jnp.sum(p, axis=0, keepdims=True)
            acc = alpha * acc + _dot(vs_ref[sl, keys], p.astype(BF16))
            mx = new
        state[hd] = (mx, den, acc)
        if hd % 2 == 1 and keys.stop == vs_ref.shape[1] and vs_ref is v_ref:
            j = hd // 2
            pair = state[hd - 1][2] / state[hd - 1][1] + acc / den
            ya_ref[j * LANES:(j + 1) * LANES, :] = pair.astype(BF16)

    y = _dot(yp_ref[...], wop_ref[...]) + _dot_tn(ya_ref[...], woa_ref[...])
    row = mod_row(pl.program_id(0))
    o_ref[...] = x_ref[...] + _mod(mod_ref, 2, row) * y


def _att_call(x, mod_all, layer, mod_row, seq_len, yp, q, k, v, cache, w, n_all, row_off, shared):
    n_tok = x.shape[0]
    n_b = n_tok // seq_len
    n_q = seq_len // ATT_TQ
    blk_off = row_off // ATT_TQ
    qrow = lambda width: pl.BlockSpec((ATT_TQ, width), lambda b, i: (b * n_q + i, 0))
    brow = lambda rows: pl.BlockSpec((rows, HEAD_W), lambda b, i: (b, 0))
    bcol = lambda cols: pl.BlockSpec((HEAD_W, cols), lambda b, i: (0, b))
    in_specs = [qrow(D_MODEL), _mod_spec(layer), qrow(POOL_WIDTH), qrow(HEAD_W),
                brow(seq_len), bcol(seq_len)]
    args = [x, mod_all, yp, q, k, v]
    if cache is not None:
        kc, vc = cache
        past = kc.shape[0] // n_b
        in_specs += [brow(past), bcol(past)]
        args += [kc, vc]
    in_specs += [_const_spec((POOL_WIDTH, D_MODEL)), _const_spec((MLA_WIDTH, D_MODEL))]
    args += [w["w_out_pool"], w["w_out_att"]]
    aliases = {}
    if shared is not None:
        in_specs.append(pl.BlockSpec(memory_space=pl.ANY))
        args.append(shared)
        aliases = {len(args) - 1: 0}
    return pl.pallas_call(
        functools.partial(_att_kernel, has_cache=cache is not None, mod_row=mod_row),
        grid=(n_b, n_q),
        in_specs=in_specs,
        out_specs=pl.BlockSpec((ATT_TQ, D_MODEL), lambda b, i: (blk_off + b * n_q + i, 0)),
        out_shape=jax.ShapeDtypeStruct((n_all, D_MODEL), F32),
        scratch_shapes=[pltpu.VMEM((MLA_WIDTH, ATT_TQ), BF16)],
        input_output_aliases=aliases,
        compiler_params=_params(2),
        name="att_out_cache" if cache is not None else "att_out",
    )(*args)


FFN_SUB = 256
FFN_GCH = D_MODEL // W_STEPS
FFN_DCH = D_FF // W_STEPS


def _ffn_kernel(x_ref, mod_ref, ng_ref, wg_ref, wu_ref, wd_ref, *rest, mod_row, split):
    outs, (wg_s, wu_s, wd_s) = rest[:-3], rest[-3:]
    t = pl.program_id(0)

    @pl.when(t < W_STEPS)
    def _():
        r = pl.multiple_of(t * FFN_GCH, FFN_GCH)
        wg_s[pl.ds(r, FFN_GCH), :] = wg_ref[0].astype(BF16)
        wu_s[pl.ds(r, FFN_GCH), :] = wu_ref[0].astype(BF16)
        r = pl.multiple_of(t * FFN_DCH, FFN_DCH)
        wd_s[pl.ds(r, FFN_DCH), :] = wd_ref[0].astype(BF16)

    def tile(o_ref):
        row = mod_row(t - W_STEPS)
        scale = ng_ref[...] * (1.0 + _mod(mod_ref, 4, row))
        shift = _mod(mod_ref, 3, row)
        gate = _mod(mod_ref, 5, row)
        for s in range(x_ref.shape[0] // FFN_SUB):
            rows = slice(s * FFN_SUB, (s + 1) * FFN_SUB)
            x = x_ref[rows, :]
            h = (_rms(x) * scale + shift).astype(BF16)
            g = _dot(h, wg_s[...])
            u = _dot(h, wu_s[...])
            d = _dot((_silu(g) * u).astype(BF16), wd_s[...])
            o_ref[rows, :] = x + gate * d

    if split is None:
        pl.when(t >= W_STEPS)(lambda: tile(outs[0]))
    else:
        pl.when((t >= W_STEPS) & (t < W_STEPS + split))(lambda: tile(outs[0]))
        pl.when(t >= W_STEPS + split)(lambda: tile(outs[1]))


def _ffn_call(x, mod_all, layer, mod_row, tm, ng, wg, wu, wd, n_first=None):
    n_tok = x.shape[0]
    n_tiles = n_tok // tm
    tok = lambda t: jnp.maximum(t - W_STEPS, 0)
    wstep = lambda t: jnp.minimum(t, W_STEPS - 1)
    if n_first is None:
        split = None
        out_specs = pl.BlockSpec((tm, D_MODEL), lambda t: (tok(t), 0))
        out_shape = jax.ShapeDtypeStruct((n_tok, D_MODEL), F32)
    else:
        split = n_first // tm
        out_specs = [
            pl.BlockSpec((tm, D_MODEL), lambda t: (jnp.minimum(tok(t), split - 1), 0)),
            pl.BlockSpec((tm, D_MODEL), lambda t: (jnp.maximum(tok(t) - split, 0), 0)),
        ]
        out_shape = [jax.ShapeDtypeStruct((n_first, D_MODEL), F32),
                     jax.ShapeDtypeStruct((n_tok - n_first, D_MODEL), F32)]
    return pl.pallas_call(
        functools.partial(_ffn_kernel, mod_row=mod_row, split=split),
        grid=(W_STEPS + n_tiles,),
        in_specs=[
            pl.BlockSpec((tm, D_MODEL), lambda t: (tok(t), 0)),
            _mod_spec(layer),
            _const_spec((1, D_MODEL)),
            pl.BlockSpec((1, FFN_GCH, D_FF), lambda t: (layer, wstep(t), 0)),
            pl.BlockSpec((1, FFN_GCH, D_FF), lambda t: (layer, wstep(t), 0)),
            pl.BlockSpec((1, FFN_DCH, D_MODEL), lambda t: (layer, wstep(t), 0)),
        ],
        out_specs=out_specs,
        out_shape=out_shape,
        scratch_shapes=[pltpu.VMEM((D_MODEL, D_FF), BF16), pltpu.VMEM((D_MODEL, D_FF), BF16),
                        pltpu.VMEM((D_FF, D_MODEL), BF16)],
        compiler_params=_params(1),
        name="ffn",
    )(x, mod_all, ng, wg, wu, wd)


GM_TM = 1024
GM_SUB = 256
GM_WCH = D_MODEL // W_STEPS


def _gmlp_kernel(x_ref, mod_ref, ng_ref, win_ref, vg_ref, ws_ref, bs_ref, wout_ref, o_ref,
                 win_s, wout_s, ws_s, gs_ref, *, mod_row):
    t = pl.program_id(0)

    @pl.when(t < W_STEPS)
    def _():
        r = pl.multiple_of(t * GM_WCH, GM_WCH)
        win_s[pl.ds(r, GM_WCH), :] = win_ref[0].astype(BF16)
        wout_s[pl.ds(r, GM_WCH), :] = wout_ref[0].astype(BF16)
        ws_s[t] = ws_ref[0, 0].astype(BF16)

    @pl.when(t >= W_STEPS)
    def _():
        row = mod_row(t - W_STEPS)
        scale = ng_ref[...] * (1.0 + _mod(mod_ref, 1, row))
        shift = _mod(mod_ref, 0, row)
        gate = _mod(mod_ref, 2, row)
        n_chunks = GM_SUB // CHUNK
        for s in range(x_ref.shape[0] // GM_SUB):
            rows = slice(s * GM_SUB, (s + 1) * GM_SUB)
            x = x_ref[rows, :]
            h = (_rms(x) * scale + shift).astype(BF16)
            z = _dot(h, win_s[...])
            vn = (_rms(z[:, C_WIDTH:]) * vg_ref[...]).astype(BF16)
            for g in range(C_GROUPS):
                cols = slice(g * C_CH, (g + 1) * C_CH)
                rhs = jnp.concatenate(
                    [vn[n * CHUNK:(n + 1) * CHUNK, cols] for n in range(n_chunks)], axis=1)
                sp = _dot(ws_s[g], rhs)
                for n in range(n_chunks):
                    crow = slice(n * CHUNK, (n + 1) * CHUNK)
                    grow = slice(s * GM_SUB + n * CHUNK, s * GM_SUB + (n + 1) * CHUNK)
                    sn = sp[:, n * C_CH:(n + 1) * C_CH] + bs_ref[g]
                    gs_ref[grow, cols] = (z[crow, cols] * sn).astype(BF16)
            o_ref[rows, :] = x + gate * _dot(gs_ref[rows, :], wout_s[...])


def _gmlp_call(x, mod_all, layer, mod_row, w, o):
    n_tok = x.shape[0]
    tok = lambda t: jnp.maximum(t - W_STEPS, 0)
    wstep = lambda t: jnp.minimum(t, W_STEPS - 1)
    assert C_GROUPS == W_STEPS
    return pl.pallas_call(
        functools.partial(_gmlp_kernel, mod_row=mod_row),
        grid=(W_STEPS + n_tok // GM_TM,),
        in_specs=[
            pl.BlockSpec((GM_TM, D_MODEL), lambda t: (tok(t), 0)),
            _mod_spec(layer),
            _const_spec((1, D_MODEL)),
            pl.BlockSpec((1, GM_WCH, 2 * C_WIDTH), lambda t: (o, wstep(t), 0)),
            _const_spec((1, C_WIDTH)),
            pl.BlockSpec((1, 1, CHUNK, CHUNK), lambda t: (o, wstep(t), 0, 0)),
            _const_spec((C_GROUPS, CHUNK, C_CH)),
            pl.BlockSpec((1, GM_WCH, D_MODEL), lambda t: (o, wstep(t), 0)),
        ],
        out_specs=pl.BlockSpec((GM_TM, D_MODEL), lambda t: (tok(t), 0)),
        out_shape=jax.ShapeDtypeStruct((n_tok, D_MODEL), F32),
        scratch_shapes=[pltpu.VMEM((D_MODEL, 2 * C_WIDTH), BF16), pltpu.VMEM((C_WIDTH, D_MODEL), BF16),
                        pltpu.VMEM((C_GROUPS, CHUNK, CHUNK), BF16), pltpu.VMEM((GM_TM, C_WIDTH), BF16)],
        compiler_params=_params(1),
        name="gmlp",
    )(x, mod_all, w["norm_mix_g"], w["w_in"], w["vnorm_g"], w["w_s"], w["b_s"], w["w_out"])


def _swap_halves(t):
    return jnp.concatenate([t[..., HALF:], t[..., :HALF]], axis=-1)


def _head_ones():
    lane = np.arange(2 * LANES)
    m = (lane[:, None] // LANES == lane[None, :] // LANES) & (lane[:, None] % LANES < QK_DIM)
    return jnp.asarray(m.astype(np.float32), dtype=BF16)


def _layer0_weights(e, norm_mix_g, ab_w_in, pool_w, pool_scale, q_norm_g, kv_norm_g, w_uq, w_ukv,
                    qn_g, kn_g, ab_w_out):
    w_in = ab_w_in[e]
    o3 = POOL_WIDTH + Q_RANK + KV_RANK
    kr_cols = jnp.concatenate(
        [jnp.zeros((D_MODEL, QK_NOPE), F32), w_in[:, o3:], w_in[:, o3:]], axis=1)
    w_in_p = jnp.concatenate([w_in[:, :o3], kr_cols], axis=1).astype(BF16)
    uq = w_uq[e].reshape(Q_RANK, MLA_HEADS, QK_DIM)
    uq = jnp.concatenate([uq, _swap_halves(uq[:, :, QK_NOPE:])], axis=-1)
    score_scale = QK_DIM ** -0.5 * math.log2(math.e)
    gq = qn_g[e] * score_scale
    gk = kn_g[e]
    ukv = w_ukv[e].reshape(KV_RANK, MLA_HEADS, QK_NOPE + V_DIM)
    w_k = jnp.pad(ukv[:, :, :QK_NOPE], ((0, 0), (0, 0), (0, LANES - QK_NOPE)))
    w_v = ukv[:, :, QK_NOPE:].reshape(KV_RANK, MLA_HEADS // 2, 2, V_DIM)
    zero = jnp.zeros_like(w_v[:, :, 0])
    w_v = jnp.stack([jnp.concatenate([w_v[:, :, 0], zero], axis=-1),
                     jnp.concatenate([zero, w_v[:, :, 1]], axis=-1)], axis=2)
    return {
        "norm_mix_g": norm_mix_g.reshape(1, D_MODEL),
        "w_in": w_in_p,
        "q_norm_g": q_norm_g[e].reshape(1, Q_RANK),
        "kv_norm_g": kv_norm_g[e].reshape(1, KV_RANK),
        "w_uq": uq.reshape(Q_RANK, HEAD_W).astype(BF16),
        "w_k": w_k.reshape(KV_RANK, HEAD_W).astype(BF16),
        "w_vt": w_v.reshape(KV_RANK, HEAD_W).T.astype(BF16),
        "qn_g": jnp.pad(gq, (0, LANES - QK_DIM)).reshape(1, LANES),
        "qn_g_rope": jnp.concatenate([gq, _swap_halves(gq[QK_NOPE:])]).reshape(1, LANES),
        "kn_g": jnp.concatenate([gk, gk[QK_NOPE:]]).reshape(1, LANES),
        "head_ones": _head_ones(),
        "pool_w": pool_w[e].astype(BF16),
        "pool_scale": pool_scale[e].reshape(1, POOL_WIDTH),
        "w_out_pool": ab_w_out[e, :POOL_WIDTH].astype(BF16),
        "w_out_att": ab_w_out[e, POOL_WIDTH:].astype(BF16),
    }


def _rope_tables(seq_len):
    rows = seq_len // GRID_W
    row = np.repeat(np.arange(rows), GRID_W).astype(np.float32)
    col = np.tile(np.arange(GRID_W), rows).astype(np.float32)
    per_axis = QK_ROPE // 2
    inv = (1.0 / (np.float32(ROPE_BASE) ** (np.arange(0, per_axis, 2, dtype=np.float32) / per_axis))
           ).astype(np.float32)
    ang = np.concatenate([row[:, None] * inv, col[:, None] * inv], axis=-1)
    cos, sin = np.cos(ang).astype(np.float32), np.sin(ang).astype(np.float32)
    ones = np.ones((seq_len, QK_NOPE), np.float32)
    z_nope = np.zeros((seq_len, QK_NOPE), np.float32)
    z_half = np.zeros((seq_len, HALF), np.float32)
    c_tab = np.concatenate([ones, cos, cos, cos, cos], axis=-1)
    a_tab = np.concatenate([z_nope, -sin, z_half, -sin, z_half], axis=-1)
    b_tab = np.concatenate([z_nope, z_half, sin, z_half, sin], axis=-1)
    q_tab = np.concatenate([ones, cos, cos, -sin, sin], axis=-1)
    return jnp.asarray(np.stack([c_tab, a_tab, b_tab, q_tab]))


def kernel(x_prompt, x_sample, cache_ckv, cache_krope, c, c_ctx, ada_w, ada_b, norm_mix_g, norm_ffn_g, ffn_wg, ffn_wu, ffn_wd, ab_w_in, pool_w, pool_scale, q_norm_g, kv_norm_g, w_uq, w_ukv, qn_g, kn_g, ab_w_out, gm_w_in, gm_vnorm_g, gm_ws, gm_bs, gm_w_out):
    n_ctx_b, ctx_len, _ = x_prompt.shape
    n_lat_b, lat_len, _ = x_sample.shape
    past = cache_ckv.shape[2]
    assert ctx_len & (ctx_len - 1) == 0 and lat_len & (lat_len - 1) == 0
    assert MOD_ROWS >= 1 + n_lat_b
    assert DEPTH == 2
    n_ctx, n_lat = n_ctx_b * ctx_len, n_lat_b * lat_len
    n_all = n_ctx + n_lat

    ctx = x_prompt.reshape(n_ctx, D_MODEL)
    lat = x_sample.reshape(n_lat, D_MODEL)

    cond = jnp.concatenate(
        [c_ctx[None, :], c, jnp.zeros((MOD_ROWS - 1 - n_lat_b, D_MODEL), F32)], axis=0)
    mod_all = _ada_call(cond, ada_w, ada_b)

    def all_row(tm):
        n_ctx_tiles, per = n_ctx // tm, lat_len // tm
        return lambda j: jnp.where(j < n_ctx_tiles, 0, 1 + (j - n_ctx_tiles) // per)

    w = _layer0_weights(0, norm_mix_g[0], ab_w_in, pool_w, pool_scale, q_norm_g, kv_norm_g,
                        w_uq, w_ukv, qn_g, kn_g, ab_w_out)
    cs = _rope_tables(lat_len)
    yp_c, q_c, k_c, v_c, ckv_c, kr_c = _l0a_call(ctx, mod_all, 0, lambda t: 0, 1024, ctx_len, w, None)
    yp_l, q_l, k_l, v_l, _, _ = _l0a_call(lat, mod_all, 0, lambda t: 1 + t, lat_len, lat_len, w, cs)
    kr_c2 = cache_krope[:, 0].reshape(n_lat_b * past, QK_ROPE)
    kr_pad = jnp.concatenate([jnp.zeros((n_lat_b * past, QK_NOPE), F32), kr_c2, kr_c2], axis=1)
    cache = _kvc_call(cache_ckv[:, 0].reshape(n_lat_b * past, KV_RANK), kr_pad, w, 1024)
    xs = _att_call(ctx, mod_all, 0, lambda b: 0, ctx_len, yp_c, q_c, k_c, v_c, None, w,
                   n_all, 0, None)
    xs = _att_call(lat, mod_all, 0, lambda b: 1 + b, lat_len, yp_l, q_l, k_l, v_l, cache, w,
                   n_all, n_ctx, xs)
    xs = _ffn_call(xs, mod_all, 0, all_row(1024), 1024, norm_ffn_g[0].reshape(1, D_MODEL),
                   ffn_wg, ffn_wu, ffn_wd)

    gw = {
        "norm_mix_g": norm_mix_g[1].reshape(1, D_MODEL),
        "w_in": gm_w_in,
        "vnorm_g": gm_vnorm_g[0].reshape(1, C_WIDTH),
        "w_s": gm_ws,
        "b_s": jnp.broadcast_to(gm_bs[0][:, :, None], (C_GROUPS, CHUNK, C_CH)),
        "w_out": gm_w_out,
    }
    xs = _gmlp_call(xs, mod_all, 1, all_row(GM_TM), gw, 0)
    y_ctx, y_lat = _ffn_call(xs, mod_all, 1, all_row(512), 512, norm_ffn_g[1].reshape(1, D_MODEL),
                             ffn_wg, ffn_wu, ffn_wd, n_first=n_ctx)

    state_ckv = ckv_c.reshape(n_ctx_b, 1, ctx_len, KV_RANK)
    state_krope = kr_c.reshape(n_ctx_b, 1, ctx_len, QK_ROPE)
    return (y_ctx.reshape(n_ctx_b, ctx_len, D_MODEL), y_lat.reshape(n_lat_b, lat_len, D_MODEL),
            state_ckv, state_krope)
```

```python
import functools
import math

import numpy as np
import jax
import jax.numpy as jnp
from jax import lax
from jax.experimental import pallas as pl
from jax.experimental.pallas import tpu as pltpu

D_MODEL = 1024
DEPTH = 2
GRID_W = 64
POOL_WINDOWS = (2, 4, 8, 16)
POOL_GROUPS = 4
POOL_CH = 128
POOL_WIDTH = POOL_GROUPS * POOL_CH
MLA_HEADS = 8
QK_NOPE = 64
QK_ROPE = 32
QK_DIM = QK_NOPE + QK_ROPE
HALF = QK_ROPE // 2
V_DIM = 64
Q_RANK = 384
KV_RANK = 256
MLA_WIDTH = MLA_HEADS * V_DIM
CHUNK = 128
C_GROUPS = 8
C_WIDTH = D_MODEL
C_CH = C_WIDTH // C_GROUPS
D_FF = 2816
ROPE_BASE = 10000.0
EPS = 1e-6

LANES = 128
HEAD_W = MLA_HEADS * LANES
AB_IN_PAD = POOL_WIDTH + Q_RANK + KV_RANK + LANES
MOD_ROWS = 16
VMEM_LIMIT = 58 * 1024 * 1024
W_STEPS = 8
Q_SS_MXU_PAIRS = 2

F32 = jnp.float32
BF16 = jnp.bfloat16


def _rms(x):
    return x * lax.rsqrt(jnp.mean(x * x, axis=-1, keepdims=True) + EPS)


def _dot(a, b):
    return jnp.dot(a, b, preferred_element_type=F32)


def _dot_nt(a, b):
    return lax.dot_general(a, b, (((1,), (1,)), ((), ())), preferred_element_type=F32)


def _dot_tn(a, b):
    return lax.dot_general(a, b, (((0,), (0,)), ((), ())), preferred_element_type=F32)


def _silu(x):
    return x * jax.nn.sigmoid(x)


def _params(n_axes):
    return pltpu.CompilerParams(
        dimension_semantics=("arbitrary",) * n_axes, vmem_limit_bytes=VMEM_LIMIT)


def _const_spec(shape):
    zeros = (0,) * len(shape)
    return pl.BlockSpec(shape, lambda *_: zeros)


def _mod_spec(layer):
    return pl.BlockSpec((1, 6, MOD_ROWS, D_MODEL), lambda *_: (layer, 0, 0, 0))


def _mod(mod_ref, term, row):
    return mod_ref[0, term, pl.ds(row, 1), :]


def _ada_kernel(cond_ref, w_ref, b_ref, o_ref):
    s = _silu(cond_ref[...]).astype(BF16)
    o_ref[0, 0] = _dot(s, w_ref[0].astype(BF16)) + b_ref[0]


def _ada_call(cond, ada_w, ada_b):
    return pl.pallas_call(
        _ada_kernel,
        grid=(DEPTH, 6),
        in_specs=[
            pl.BlockSpec((MOD_ROWS, D_MODEL), lambda i, j: (0, 0)),
            pl.BlockSpec((1, D_MODEL, D_MODEL), lambda i, j: (i, 0, j)),
            pl.BlockSpec((1, 1, D_MODEL), lambda i, j: (i, 0, j)),
        ],
        out_specs=pl.BlockSpec((1, 1, MOD_ROWS, D_MODEL), lambda i, j: (i, j, 0, 0)),
        out_shape=jax.ShapeDtypeStruct((DEPTH, 6, MOD_ROWS, D_MODEL), F32),
        compiler_params=_params(2),
        name="ada_mod",
    )(cond, ada_w, ada_b.reshape(DEPTH, 1, 6 * D_MODEL))


def _rope(t, cs_ref):
    return (t * cs_ref[0]
            + pltpu.roll(t, LANES - HALF, axis=1) * cs_ref[1]
            + pltpu.roll(t, HALF, axis=1) * cs_ref[2])


def _kv_heads(ckv_bf, kr, wk_ref, wv_ref, kng_ref, cs_ref, k_ref, v_ref):
    v_ref[...] = _dot_nt(wv_ref[...], ckv_bf).astype(BF16)
    kpre = _dot(ckv_bf, wk_ref[...])
    kg = kng_ref[...]
    krg = kr * kg
    if cs_ref is not None:
        krg = _rope(krg, cs_ref)
    lane = lax.broadcasted_iota(jnp.int32, kr.shape, 1)
    kr_ss = jnp.sum(jnp.where(lane < QK_DIM, kr * kr, 0.0), axis=-1, keepdims=True)
    for h in range(MLA_HEADS):
        kh = kpre[:, h * LANES:(h + 1) * LANES]
        r = lax.rsqrt((jnp.sum(kh * kh, axis=-1, keepdims=True) + kr_ss) * (1.0 / QK_DIM) + EPS)
        k_ref[:, h * LANES:(h + 1) * LANES] = ((kh * kg + krg) * r).astype(BF16)


def _shift_rows(a, k, pos, seq_len):
    n = a.shape[0]
    r = pltpu.roll(a, k % n, axis=0)
    src = pos - k
    ok = (src >= 0) if k > 0 else (src < seq_len)
    return jnp.where(ok, r, 0.0)


def _pool_group(ug, half, pos, seq_len):
    fw = ug
    step = 1
    while step < half:
        fw = fw + _shift_rows(fw, -step, pos, seq_len)
        step *= 2
    bk = _shift_rows(ug, 1, pos, seq_len)
    step = 1
    while step < half:
        bk = bk + _shift_rows(bk, step, pos, seq_len)
        step *= 2
    cnt = jnp.minimum(pos + half, seq_len) - jnp.maximum(pos - half, 0)
    return (fw + bk) / cnt.astype(F32) - ug


def _l0a_kernel(*refs, seq_len, use_rope, mod_row):
    (x_ref, mod_ref, ng_ref, win_ref, qg_ref, kvg_ref, wuq_ref, wk_ref, wv_ref,
     qng_ref, kng_ref, pw_ref, ps_ref, ones_ref) = refs[:14]
    if use_rope:
        cs_ref = refs[14]
        outs = refs[15:]
    else:
        cs_ref = None
        outs = refs[14:]
    yp_ref, q_ref, k_ref, v_ref, ckv_ref, kr_ref = outs

    tm = x_ref.shape[0]
    row = mod_row(pl.program_id(0))
    scale = ng_ref[...] * (1.0 + _mod(mod_ref, 1, row))
    h = _rms(x_ref[...]) * scale + _mod(mod_ref, 0, row)
    z = _dot(h.astype(BF16), win_ref[...])
    o1, o2, o3 = POOL_WIDTH, POOL_WIDTH + Q_RANK, POOL_WIDTH + Q_RANK + KV_RANK

    ckv = _rms(z[:, o2:o3]) * kvg_ref[...]
    kr = z[:, o3:]
    ckv_ref[...] = ckv
    kr_ref[...] = kr[:, QK_NOPE:QK_DIM]
    _kv_heads(ckv.astype(BF16), kr, wk_ref, wv_ref, kng_ref, cs_ref, k_ref, v_ref)

    cqn = (_rms(z[:, o1:o2]) * qg_ref[...]).astype(BF16)
    qf = _dot(cqn, wuq_ref[...])
    qt = qng_ref[...]
    if use_rope:
        qt = qt * cs_ref[3]
    lane2 = lax.broadcasted_iota(jnp.int32, (tm, 2 * LANES), 1)
    for j in range(MLA_HEADS // 2):
        qp = qf[:, 2 * j * LANES:(2 * j + 2) * LANES]
        sq = qp * qp
        if j < Q_SS_MXU_PAIRS:
            hi = sq.astype(BF16)
            lo = (sq - hi.astype(F32)).astype(BF16)
            ss = _dot(hi, ones_ref[...]) + _dot(lo, ones_ref[...])
            qn = qp * lax.rsqrt(ss * (1.0 / QK_DIM) + EPS)
            parts = [qn[:, :LANES], qn[:, LANES:]]
        else:
            sq = jnp.where((lane2 & (LANES - 1)) < QK_DIM, sq, 0.0)
            parts = []
            for hh in range(2):
                ss = jnp.sum(sq[:, hh * LANES:(hh + 1) * LANES], axis=-1, keepdims=True)
                parts.append(qp[:, hh * LANES:(hh + 1) * LANES]
                             * lax.rsqrt(ss * (1.0 / QK_DIM) + EPS))
        for hh in range(2):
            hd = 2 * j + hh
            q_ref[:, hd * LANES:(hd + 1) * LANES] = (parts[hh] * qt).astype(BF16)

    pos = lax.broadcasted_iota(jnp.int32, (tm, POOL_CH), 0) & (seq_len - 1)
    for g, w in enumerate(POOL_WINDOWS):
        ug = z[:, g * POOL_CH:(g + 1) * POOL_CH]
        p = _pool_group(ug, w // 2, pos, seq_len)
        y = _dot(p.astype(BF16), pw_ref[g]) * ps_ref[:, g * POOL_CH:(g + 1) * POOL_CH]
        yp_ref[:, g * POOL_CH:(g + 1) * POOL_CH] = y.astype(BF16)


def _l0a_call(x, mod_all, layer, mod_row, tm, seq_len, w, cs):
    n_tok = x.shape[0]
    use_rope = cs is not None
    row_spec = lambda width: pl.BlockSpec((tm, width), lambda t: (t, 0))
    in_specs = [
        row_spec(D_MODEL),
        _mod_spec(layer),
        _const_spec((1, D_MODEL)),
        _const_spec((D_MODEL, AB_IN_PAD)),
        _const_spec((1, Q_RANK)),
        _const_spec((1, KV_RANK)),
        _const_spec((Q_RANK, HEAD_W)),
        _const_spec((KV_RANK, HEAD_W)),
        _const_spec((HEAD_W, KV_RANK)),
        _const_spec((1, LANES)),
        _const_spec((1, LANES)),
        _const_spec((POOL_GROUPS, POOL_CH, POOL_CH)),
        _const_spec((1, POOL_WIDTH)),
        _const_spec((2 * LANES, 2 * LANES)),
    ]
    args = [x, mod_all, w["norm_mix_g"], w["w_in"], w["q_norm_g"], w["kv_norm_g"], w["w_uq"],
            w["w_k"], w["w_vt"], w["qn_g_rope"] if use_rope else w["qn_g"], w["kn_g"],
            w["pool_w"], w["pool_scale"], w["head_ones"]]
    if use_rope:
        assert tm == seq_len
        in_specs.append(_const_spec((4, seq_len, LANES)))
        args.append(cs)
    out_shape = [
        jax.ShapeDtypeStruct((n_tok, POOL_WIDTH), BF16),
        jax.ShapeDtypeStruct((n_tok, HEAD_W), BF16),
        jax.ShapeDtypeStruct((n_tok, HEAD_W), BF16),
        jax.ShapeDtypeStruct((HEAD_W, n_tok), BF16),
        jax.ShapeDtypeStruct((n_tok, KV_RANK), F32),
        jax.ShapeDtypeStruct((n_tok, QK_ROPE), F32),
    ]
    out_specs = [row_spec(POOL_WIDTH), row_spec(HEAD_W), row_spec(HEAD_W),
                 pl.BlockSpec((HEAD_W, tm), lambda t: (0, t)),
                 row_spec(KV_RANK), row_spec(QK_ROPE)]
    return pl.pallas_call(
        functools.partial(_l0a_kernel, seq_len=seq_len, use_rope=use_rope, mod_row=mod_row),
        grid=(n_tok // tm,),
        in_specs=in_specs,
        out_specs=out_specs,
        out_shape=out_shape,
        compiler_params=_params(1),
        name="l0_front_rope" if use_rope else "l0_front",
    )(*args)


def _kvc_kernel(ckv_ref, kr_ref, wk_ref, wv_ref, kng_ref, k_ref, v_ref):
    _kv_heads(ckv_ref[...].astype(BF16), kr_ref[...], wk_ref, wv_ref, kng_ref, None, k_ref, v_ref)


def _kvc_call(ckv, kr_pad, w, tm):
    n_tok = ckv.shape[0]
    row_spec = lambda width: pl.BlockSpec((tm, width), lambda t: (t, 0))
    return pl.pallas_call(
        _kvc_kernel,
        grid=(n_tok // tm,),
        in_specs=[row_spec(KV_RANK), row_spec(LANES), _const_spec((KV_RANK, HEAD_W)),
                  _const_spec((HEAD_W, KV_RANK)), _const_spec((1, LANES))],
        out_specs=[row_spec(HEAD_W), pl.BlockSpec((HEAD_W, tm), lambda t: (0, t))],
        out_shape=[jax.ShapeDtypeStruct((n_tok, HEAD_W), BF16),
                   jax.ShapeDtypeStruct((HEAD_W, n_tok), BF16)],
        compiler_params=_params(1),
        name="cache_kv",
    )(ckv, kr_pad, w["w_k"], w["w_vt"], w["kn_g"])


ATT_TQ = 256
ATT_KC = 512
ATT_AHEAD = 6


def _att_kernel(*refs, has_cache, mod_row):
    if has_cache:
        (x_ref, mod_ref, yp_ref, q_ref, k_ref, v_ref, kc_ref, vc_ref,
         wop_ref, woa_ref, _, o_ref, ya_ref) = refs
    else:
        x_ref, mod_ref, yp_ref, q_ref, k_ref, v_ref, wop_ref, woa_ref, o_ref, ya_ref = refs
        kc_ref = vc_ref = None

    sources = [(k_ref, v_ref)] if not has_cache else [(kc_ref, vc_ref), (k_ref, v_ref)]
    chunks = []
    for ks_ref, vs_ref in sources:
        n_keys = ks_ref.shape[0]
        kc = min(ATT_KC, n_keys)
        assert n_keys % kc == 0
        chunks += [(ks_ref, vs_ref, slice(c * kc, (c + 1) * kc)) for c in range(n_keys // kc)]
    items = [(hd, ch) for hd in range(MLA_HEADS) for ch in chunks]

    def scores(item):
        hd, (ks_ref, _, keys) = item
        sl = slice(hd * LANES, (hd + 1) * LANES)
        return _dot_nt(ks_ref[keys, sl], q_ref[:, sl])

    pending = [scores(it) for it in items[:ATT_AHEAD]]
    state = {}
    for n, (hd, (_, vs_ref, keys)) in enumerate(items):
        if n + ATT_AHEAD < len(items):
            pending.append(scores(items[n + ATT_AHEAD]))
        s = pending.pop(0)
        sl = slice(hd * LANES, (hd + 1) * LANES)
        cmax = jnp.max(s, axis=0, keepdims=True)
        if hd not in state:
            mx = cmax
            p = jnp.exp2(s - mx)
            den = jnp.sum(p, axis=0, keepdims=True)
            acc = _dot(vs_ref[sl, keys], p.astype(BF16))
        else:
            mx, den, acc = state[hd]
            new = jnp.maximum(mx, cmax)
            alpha = jnp.exp2(mx - new)
            p = jnp.exp2(s - new)
            den = alpha * den + jnp.sum(p, axis=0, keepdims=True)
            acc = alpha * acc + _dot(vs_ref[sl, keys], p.astype(BF16))
            mx = new
        state[hd] = (mx, den, acc)
        if hd % 2 == 1 and keys.stop == vs_ref.shape[1] and vs_ref is v_ref:
            j = hd // 2
            pair = state[hd - 1][2] / state[hd - 1][1] + acc / den
            ya_ref[j * LANES:(j + 1) * LANES, :] = pair.astype(BF16)

    y = _dot(yp_ref[...], wop_ref[...]) + _dot_tn(ya_ref[...], woa_ref[...])
    row = mod_row(pl.program_id(0))
    o_ref[...] = x_ref[...] + _mod(mod_ref, 2, row) * y


def _att_call(x, mod_all, layer, mod_row, seq_len, yp, q, k, v, cache, w, n_all, row_off, shared):
    n_tok = x.shape[0]
    n_b = n_tok // seq_len
    n_q = seq_len // ATT_TQ
    blk_off = row_off // ATT_TQ
    qrow = lambda width: pl.BlockSpec((ATT_TQ, width), lambda b, i: (b * n_q + i, 0))
    brow = lambda rows: pl.BlockSpec((rows, HEAD_W), lambda b, i: (b, 0))
    bcol = lambda cols: pl.BlockSpec((HEAD_W, cols), lambda b, i: (0, b))
    in_specs = [qrow(D_MODEL), _mod_spec(layer), qrow(POOL_WIDTH), qrow(HEAD_W),
                brow(seq_len), bcol(seq_len)]
    args = [x, mod_all, yp, q, k, v]
    if cache is not None:
        kc, vc = cache
        past = kc.shape[0] // n_b
        in_specs += [brow(past), bcol(past)]
        args += [kc, vc]
    in_specs += [_const_spec((POOL_WIDTH, D_MODEL)), _const_spec((MLA_WIDTH, D_MODEL))]
    args += [w["w_out_pool"], w["w_out_att"]]
    aliases = {}
    if shared is not None:
        in_specs.append(pl.BlockSpec(memory_space=pl.ANY))
        args.append(shared)
        aliases = {len(args) - 1: 0}
    return pl.pallas_call(
        functools.partial(_att_kernel, has_cache=cache is not None, mod_row=mod_row),
        grid=(n_b, n_q),
        in_specs=in_specs,
        out_specs=pl.BlockSpec((ATT_TQ, D_MODEL), lambda b, i: (blk_off + b * n_q + i, 0)),
        out_shape=jax.ShapeDtypeStruct((n_all, D_MODEL), F32),
        scratch_shapes=[pltpu.VMEM((MLA_WIDTH, ATT_TQ), BF16)],
        input_output_aliases=aliases,
        compiler_params=_params(2),
        name="att_out_cache" if cache is not None else "att_out",
    )(*args)


FFN_SUB = 256
FFN_GCH = D_MODEL // W_STEPS
FFN_DCH = D_FF // W_STEPS


def _ffn_kernel(x_ref, mod_ref, ng_ref, wg_ref, wu_ref, wd_ref, *rest, mod_row, split):
    outs, (wg_s, wu_s, wd_s) = rest[:-3], rest[-3:]
    t = pl.program_id(0)

    @pl.when(t < W_STEPS)
    def _():
        r = pl.multiple_of(t * FFN_GCH, FFN_GCH)
        wg_s[pl.ds(r, FFN_GCH), :] = wg_ref[0].astype(BF16)
        wu_s[pl.ds(r, FFN_GCH), :] = wu_ref[0].astype(BF16)
        r = pl.multiple_of(t * FFN_DCH, FFN_DCH)
        wd_s[pl.ds(r, FFN_DCH), :] = wd_ref[0].astype(BF16)

    def tile(o_ref):
        row = mod_row(t - W_STEPS)
        scale = ng_ref[...] * (1.0 + _mod(mod_ref, 4, row))
        shift = _mod(mod_ref, 3, row)
        gate = _mod(mod_ref, 5, row)
        for s in range(x_ref.shape[0] // FFN_SUB):
            rows = slice(s * FFN_SUB, (s + 1) * FFN_SUB)
            x = x_ref[rows, :]
            h = (_rms(x) * scale + shift).astype(BF16)
            g = _dot(h, wg_s[...])
            u = _dot(h, wu_s[...])
            d = _dot((_silu(g) * u).astype(BF16), wd_s[...])
            o_ref[rows, :] = x + gate * d

    if split is None:
        pl.when(t >= W_STEPS)(lambda: tile(outs[0]))
    else:
        pl.when((t >= W_STEPS) & (t < W_STEPS + split))(lambda: tile(outs[0]))
        pl.when(t >= W_STEPS + split)(lambda: tile(outs[1]))


def _ffn_call(x, mod_all, layer, mod_row, tm, ng, wg, wu, wd, n_first=None):
    n_tok = x.shape[0]
    n_tiles = n_tok // tm
    tok = lambda t: jnp.maximum(t - W_STEPS, 0)
    wstep = lambda t: jnp.minimum(t, W_STEPS - 1)
    if n_first is None:
        split = None
        out_specs = pl.BlockSpec((tm, D_MODEL), lambda t: (tok(t), 0))
        out_shape = jax.ShapeDtypeStruct((n_tok, D_MODEL), F32)
    else:
        split = n_first // tm
        out_specs = [
            pl.BlockSpec((tm, D_MODEL), lambda t: (jnp.minimum(tok(t), split - 1), 0)),
            pl.BlockSpec((tm, D_MODEL), lambda t: (jnp.maximum(tok(t) - split, 0), 0)),
        ]
        out_shape = [jax.ShapeDtypeStruct((n_first, D_MODEL), F32),
                     jax.ShapeDtypeStruct((n_tok - n_first, D_MODEL), F32)]
    return pl.pallas_call(
        functools.partial(_ffn_kernel, mod_row=mod_row, split=split),
        grid=(W_STEPS + n_tiles,),
        in_specs=[
            pl.BlockSpec((tm, D_MODEL), lambda t: (tok(t), 0)),
            _mod_spec(layer),
            _const_spec((1, D_MODEL)),
            pl.BlockSpec((1, FFN_GCH, D_FF), lambda t: (layer, wstep(t), 0)),
            pl.BlockSpec((1, FFN_GCH, D_FF), lambda t: (layer, wstep(t), 0)),
            pl.BlockSpec((1, FFN_DCH, D_MODEL), lambda t: (layer, wstep(t), 0)),
        ],
        out_specs=out_specs,
        out_shape=out_shape,
        scratch_shapes=[pltpu.VMEM((D_MODEL, D_FF), BF16), pltpu.VMEM((D_MODEL, D_FF), BF16),
                        pltpu.VMEM((D_FF, D_MODEL), BF16)],
        compiler_params=_params(1),
        name="ffn",
    )(x, mod_all, ng, wg, wu, wd)


GM_TM = 1024
GM_SUB = 256
GM_WCH = D_MODEL // W_STEPS


def _gmlp_kernel(x_ref, mod_ref, ng_ref, win_ref, vg_ref, ws_ref, bs_ref, wout_ref, o_ref,
                 win_s, wout_s, ws_s, gs_ref, *, mod_row):
    t = pl.program_id(0)

    @pl.when(t < W_STEPS)
    def _():
        r = pl.multiple_of(t * GM_WCH, GM_WCH)
        win_s[pl.ds(r, GM_WCH), :] = win_ref[0].astype(BF16)
        wout_s[pl.ds(r, GM_WCH), :] = wout_ref[0].astype(BF16)
        ws_s[t] = ws_ref[0, 0].astype(BF16)

    @pl.when(t >= W_STEPS)
    def _():
        row = mod_row(t - W_STEPS)
        scale = ng_ref[...] * (1.0 + _mod(mod_ref, 1, row))
        shift = _mod(mod_ref, 0, row)
        gate = _mod(mod_ref, 2, row)
        n_chunks = GM_SUB // CHUNK
        for s in range(x_ref.shape[0] // GM_SUB):
            rows = slice(s * GM_SUB, (s + 1) * GM_SUB)
            x = x_ref[rows, :]
            h = (_rms(x) * scale + shift).astype(BF16)
            z = _dot(h, win_s[...])
            vn = (_rms(z[:, C_WIDTH:]) * vg_ref[...]).astype(BF16)
            for g in range(C_GROUPS):
                cols = slice(g * C_CH, (g + 1) * C_CH)
                rhs = jnp.concatenate(
                    [vn[n * CHUNK:(n + 1) * CHUNK, cols] for n in range(n_chunks)], axis=1)
                sp = _dot(ws_s[g], rhs)
                for n in range(n_chunks):
                    crow = slice(n * CHUNK, (n + 1) * CHUNK)
                    grow = slice(s * GM_SUB + n * CHUNK, s * GM_SUB + (n + 1) * CHUNK)
                    sn = sp[:, n * C_CH:(n + 1) * C_CH] + bs_ref[g]
                    gs_ref[grow, cols] = (z[crow, cols] * sn).astype(BF16)
            o_ref[rows, :] = x + gate * _dot(gs_ref[rows, :], wout_s[...])


def _gmlp_call(x, mod_all, layer, mod_row, w, o):
    n_tok = x.shape[0]
    tok = lambda t: jnp.maximum(t - W_STEPS, 0)
    wstep = lambda t: jnp.minimum(t, W_STEPS - 1)
    assert C_GROUPS == W_STEPS
    return pl.pallas_call(
        functools.partial(_gmlp_kernel, mod_row=mod_row),
        grid=(W_STEPS + n_tok // GM_TM,),
        in_specs=[
            pl.BlockSpec((GM_TM, D_MODEL), lambda t: (tok(t), 0)),
            _mod_spec(layer),
            _const_spec((1, D_MODEL)),
            pl.BlockSpec((1, GM_WCH, 2 * C_WIDTH), lambda t: (o, wstep(t), 0)),
            _const_spec((1, C_WIDTH)),
            pl.BlockSpec((1, 1, CHUNK, CHUNK), lambda t: (o, wstep(t), 0, 0)),
            _const_spec((C_GROUPS, CHUNK, C_CH)),
            pl.BlockSpec((1, GM_WCH, D_MODEL), lambda t: (o, wstep(t), 0)),
        ],
        out_specs=pl.BlockSpec((GM_TM, D_MODEL), lambda t: (tok(t), 0)),
        out_shape=jax.ShapeDtypeStruct((n_tok, D_MODEL), F32),
        scratch_shapes=[pltpu.VMEM((D_MODEL, 2 * C_WIDTH), BF16), pltpu.VMEM((C_WIDTH, D_MODEL), BF16),
                        pltpu.VMEM((C_GROUPS, CHUNK, CHUNK), BF16), pltpu.VMEM((GM_TM, C_WIDTH), BF16)],
        compiler_params=_params(1),
        name="gmlp",
    )(x, mod_all, w["norm_mix_g"], w["w_in"], w["vnorm_g"], w["w_s"], w["b_s"], w["w_out"])


def _swap_halves(t):
    return jnp.concatenate([t[..., HALF:], t[..., :HALF]], axis=-1)


def _head_ones():
    lane = np.arange(2 * LANES)
    m = (lane[:, None] // LANES == lane[None, :] // LANES) & (lane[:, None] % LANES < QK_DIM)
    return jnp.asarray(m.astype(np.float32), dtype=BF16)


def _layer0_weights(e, norm_mix_g, ab_w_in, pool_w, pool_scale, q_norm_g, kv_norm_g, w_uq, w_ukv,
                    qn_g, kn_g, ab_w_out):
    w_in = ab_w_in[e]
    o3 = POOL_WIDTH + Q_RANK + KV_RANK
    kr_cols = jnp.concatenate(
        [jnp.zeros((D_MODEL, QK_NOPE), F32), w_in[:, o3:], w_in[:, o3:]], axis=1)
    w_in_p = jnp.concatenate([w_in[:, :o3], kr_cols], axis=1).astype(BF16)
    uq = w_uq[e].reshape(Q_RANK, MLA_HEADS, QK_DIM)
    uq = jnp.concatenate([uq, _swap_halves(uq[:, :, QK_NOPE:])], axis=-1)
    score_scale = QK_DIM ** -0.5 * math.log2(math.e)
    gq = qn_g[e] * score_scale
    gk = kn_g[e]
    ukv = w_ukv[e].reshape(KV_RANK, MLA_HEADS, QK_NOPE + V_DIM)
    w_k = jnp.pad(ukv[:, :, :QK_NOPE], ((0, 0), (0, 0), (0, LANES - QK_NOPE)))
    w_v = ukv[:, :, QK_NOPE:].reshape(KV_RANK, MLA_HEADS // 2, 2, V_DIM)
    zero = jnp.zeros_like(w_v[:, :, 0])
    w_v = jnp.stack([jnp.concatenate([w_v[:, :, 0], zero], axis=-1),
                     jnp.concatenate([zero, w_v[:, :, 1]], axis=-1)], axis=2)
    return {
        "norm_mix_g": norm_mix_g.reshape(1, D_MODEL),
        "w_in": w_in_p,
        "q_norm_g": q_norm_g[e].reshape(1, Q_RANK),
        "kv_norm_g": kv_norm_g[e].reshape(1, KV_RANK),
        "w_uq": uq.reshape(Q_RANK, HEAD_W).astype(BF16),
        "w_k": w_k.reshape(KV_RANK, HEAD_W).astype(BF16),
        "w_vt": w_v.reshape(KV_RANK, HEAD_W).T.astype(BF16),
        "qn_g": jnp.pad(gq, (0, LANES - QK_DIM)).reshape(1, LANES),
        "qn_g_rope": jnp.concatenate([gq, _swap_halves(gq[QK_NOPE:])]).reshape(1, LANES),
        "kn_g": jnp.concatenate([gk, gk[QK_NOPE:]]).reshape(1, LANES),
        "head_ones": _head_ones(),
        "pool_w": pool_w[e].astype(BF16),
        "pool_scale": pool_scale[e].reshape(1, POOL_WIDTH),
        "w_out_pool": ab_w_out[e, :POOL_WIDTH].astype(BF16),
        "w_out_att": ab_w_out[e, POOL_WIDTH:].astype(BF16),
    }


def _rope_tables(seq_len):
    rows = seq_len // GRID_W
    row = np.repeat(np.arange(rows), GRID_W).astype(np.float32)
    col = np.tile(np.arange(GRID_W), rows).astype(np.float32)
    per_axis = QK_ROPE // 2
    inv = (1.0 / (np.float32(ROPE_BASE) ** (np.arange(0, per_axis, 2, dtype=np.float32) / per_axis))
           ).astype(np.float32)
    ang = np.concatenate([row[:, None] * inv, col[:, None] * inv], axis=-1)
    cos, sin = np.cos(ang).astype(np.float32), np.sin(ang).astype(np.float32)
    ones = np.ones((seq_len, QK_NOPE), np.float32)
    z_nope = np.zeros((seq_len, QK_NOPE), np.float32)
    z_half = np.zeros((seq_len, HALF), np.float32)
    c_tab = np.concatenate([ones, cos, cos, cos, cos], axis=-1)
    a_tab = np.concatenate([z_nope, -sin, z_half, -sin, z_half], axis=-1)
    b_tab = np.concatenate([z_nope, z_half, sin, z_half, sin], axis=-1)
    q_tab = np.concatenate([ones, cos, cos, -sin, sin], axis=-1)
    return jnp.asarray(np.stack([c_tab, a_tab, b_tab, q_tab]))


def kernel(x_prompt, x_sample, cache_ckv, cache_krope, c, c_ctx, ada_w, ada_b, norm_mix_g, norm_ffn_g, ffn_wg, ffn_wu, ffn_wd, ab_w_in, pool_w, pool_scale, q_norm_g, kv_norm_g, w_uq, w_ukv, qn_g, kn_g, ab_w_out, gm_w_in, gm_vnorm_g, gm_ws, gm_bs, gm_w_out):
    n_ctx_b, ctx_len, _ = x_prompt.shape
    n_lat_b, lat_len, _ = x_sample.shape
    past = cache_ckv.shape[2]
    assert ctx_len & (ctx_len - 1) == 0 and lat_len & (lat_len - 1) == 0
    assert MOD_ROWS >= 1 + n_lat_b
    assert DEPTH == 2
    n_ctx, n_lat = n_ctx_b * ctx_len, n_lat_b * lat_len
    n_all = n_ctx + n_lat

    ctx = x_prompt.reshape(n_ctx, D_MODEL)
    lat = x_sample.reshape(n_lat, D_MODEL)

    cond = jnp.concatenate(
        [c_ctx[None, :], c, jnp.zeros((MOD_ROWS - 1 - n_lat_b, D_MODEL), F32)], axis=0)
    mod_all = _ada_call(cond, ada_w, ada_b)

    def all_row(tm):
        n_ctx_tiles, per = n_ctx // tm, lat_len // tm
        return lambda j: jnp.where(j < n_ctx_tiles, 0, 1 + (j - n_ctx_tiles) // per)

    w = _layer0_weights(0, norm_mix_g[0], ab_w_in, pool_w, pool_scale, q_norm_g, kv_norm_g,
                        w_uq, w_ukv, qn_g, kn_g, ab_w_out)
    cs = _rope_tables(lat_len)
    yp_c, q_c, k_c, v_c, ckv_c, kr_c = _l0a_call(ctx, mod_all, 0, lambda t: 0, 1024, ctx_len, w, None)
    yp_l, q_l, k_l, v_l, _, _ = _l0a_call(lat, mod_all, 0, lambda t: 1 + t, lat_len, lat_len, w, cs)
    kr_c2 = cache_krope[:, 0].reshape(n_lat_b * past, QK_ROPE)
    kr_pad = jnp.concatenate([jnp.zeros((n_lat_b * past, QK_NOPE), F32), kr_c2, kr_c2], axis=1)
    cache = _kvc_call(cache_ckv[:, 0].reshape(n_lat_b * past, KV_RANK), kr_pad, w, 1024)
    xs = _att_call(ctx, mod_all, 0, lambda b: 0, ctx_len, yp_c, q_c, k_c, v_c, None, w,
                   n_all, 0, None)
    xs = _att_call(lat, mod_all, 0, lambda b: 1 + b, lat_len, yp_l, q_l, k_l, v_l, cache, w,
                   n_all, n_ctx, xs)
    xs = _ffn_call(xs, mod_all, 0, all_row(1024), 1024, norm_ffn_g[0].reshape(1, D_MODEL),
                   ffn_wg, ffn_wu, ffn_wd)

    gw = {
        "norm_mix_g": norm_mix_g[1].reshape(1, D_MODEL),
        "w_in": gm_w_in,
        "vnorm_g": gm_vnorm_g[0].reshape(1, C_WIDTH),
        "w_s": gm_ws,
        "b_s": jnp.broadcast_to(gm_bs[0][:, :, None], (C_GROUPS, CHUNK, C_CH)),
        "w_out": gm_w_out,
    }
    xs = _gmlp_call(xs, mod_all, 1, all_row(GM_TM), gw, 0)
    y_ctx, y_lat = _ffn_call(xs, mod_all, 1, all_row(512), 512, norm_ffn_g[1].reshape(1, D_MODEL),
                             ffn_wg, ffn_wu, ffn_wd, n_first=n_ctx)

    state_ckv = ckv_c.reshape(n_ctx_b, 1, ctx_len, KV_RANK)
    state_krope = kr_c.reshape(n_ctx_b, 1, ctx_len, QK_ROPE)
    return (y_ctx.reshape(n_ctx_b, ctx_len, D_MODEL), y_lat.reshape(n_lat_b, lat_len, D_MODEL),
            state_ckv, state_krope)
```

```python
import functools
import math

import numpy as np
import jax
import jax.numpy as jnp
from jax import lax
from jax.experimental import pallas as pl
from jax.experimental.pallas import tpu as pltpu

D_MODEL = 1024
DEPTH = 2
GRID_W = 64
POOL_WINDOWS = (2, 4, 8, 16)
POOL_GROUPS = 4
POOL_CH = 128
POOL_WIDTH = POOL_GROUPS * POOL_CH
MLA_HEADS = 8
QK_NOPE = 64
QK_ROPE = 32
QK_DIM = QK_NOPE + QK_ROPE
HALF = QK_ROPE // 2
V_DIM = 64
Q_RANK = 384
KV_RANK = 256
MLA_WIDTH = MLA_HEADS * V_DIM
CHUNK = 128
C_GROUPS = 8
C_WIDTH = D_MODEL
C_CH = C_WIDTH // C_GROUPS
D_FF = 2816
ROPE_BASE = 10000.0
EPS = 1e-6

LANES = 128
HEAD_W = MLA_HEADS * LANES
AB_IN_PAD = POOL_WIDTH + Q_RANK + KV_RANK + LANES
MOD_ROWS = 16
VMEM_LIMIT = 58 * 1024 * 1024
W_STEPS = 8
Q_SS_MXU_PAIRS = 2

F32 = jnp.float32
BF16 = jnp.bfloat16


def _rms(x):
    return x * lax.rsqrt(jnp.mean(x * x, axis=-1, keepdims=True) + EPS)


def _dot(a, b):
    return jnp.dot(a, b, preferred_element_type=F32)


def _dot_nt(a, b):
    return lax.dot_general(a, b, (((1,), (1,)), ((), ())), preferred_element_type=F32)


def _dot_tn(a, b):
    return lax.dot_general(a, b, (((0,), (0,)), ((), ())), preferred_element_type=F32)


def _silu(x):
    return x * jax.nn.sigmoid(x)


def _params(n_axes):
    return pltpu.CompilerParams(
        dimension_semantics=("arbitrary",) * n_axes, vmem_limit_bytes=VMEM_LIMIT)


def _const_spec(shape):
    zeros = (0,) * len(shape)
    return pl.BlockSpec(shape, lambda *_: zeros)


def _mod_spec(layer):
    return pl.BlockSpec((1, 6, MOD_ROWS, D_MODEL), lambda *_: (layer, 0, 0, 0))


def _mod(mod_ref, term, row):
    return mod_ref[0, term, pl.ds(row, 1), :]


def _ada_kernel(cond_ref, w_ref, b_ref, o_ref):
    s = _silu(cond_ref[...]).astype(BF16)
    o_ref[0, 0] = _dot(s, w_ref[0].astype(BF16)) + b_ref[0]


def _ada_call(cond, ada_w, ada_b):
    return pl.pallas_call(
        _ada_kernel,
        grid=(DEPTH, 6),
        in_specs=[
            pl.BlockSpec((MOD_ROWS, D_MODEL), lambda i, j: (0, 0)),
            pl.BlockSpec((1, D_MODEL, D_MODEL), lambda i, j: (i, 0, j)),
            pl.BlockSpec((1, 1, D_MODEL), lambda i, j: (i, 0, j)),
        ],
        out_specs=pl.BlockSpec((1, 1, MOD_ROWS, D_MODEL), lambda i, j: (i, j, 0, 0)),
        out_shape=jax.ShapeDtypeStruct((DEPTH, 6, MOD_ROWS, D_MODEL), F32),
        compiler_params=_params(2),
        name="ada_mod",
    )(cond, ada_w, ada_b.reshape(DEPTH, 1, 6 * D_MODEL))


def _rope(t, cs_ref):
    return (t * cs_ref[0]
            + pltpu.roll(t, LANES - HALF, axis=1) * cs_ref[1]
            + pltpu.roll(t, HALF, axis=1) * cs_ref[2])


def _kv_heads(ckv_bf, kr, wk_ref, wv_ref, kng_ref, cs_ref, k_ref, v_ref):
    v_ref[...] = _dot_nt(wv_ref[...], ckv_bf).astype(BF16)
    kpre = _dot(ckv_bf, wk_ref[...])
    kg = kng_ref[...]
    krg = kr * kg
    if cs_ref is not None:
        krg = _rope(krg, cs_ref)
    lane = lax.broadcasted_iota(jnp.int32, kr.shape, 1)
    kr_ss = jnp.sum(jnp.where(lane < QK_DIM, kr * kr, 0.0), axis=-1, keepdims=True)
    for h in range(MLA_HEADS):
        kh = kpre[:, h * LANES:(h + 1) * LANES]
        r = lax.rsqrt((jnp.sum(kh * kh, axis=-1, keepdims=True) + kr_ss) * (1.0 / QK_DIM) + EPS)
        k_ref[:, h * LANES:(h + 1) * LANES] = ((kh * kg + krg) * r).astype(BF16)


def _shift_rows(a, k, pos, seq_len):
    n = a.shape[0]
    r = pltpu.roll(a, k % n, axis=0)
    src = pos - k
    ok = (src >= 0) if k > 0 else (src < seq_len)
    return jnp.where(ok, r, 0.0)


def _pool_group(ug, half, pos, seq_len):
    fw = ug
    step = 1
    while step < half:
        fw = fw + _shift_rows(fw, -step, pos, seq_len)
        step *= 2
    bk = _shift_rows(ug, 1, pos, seq_len)
    step = 1
    while step < half:
        bk = bk + _shift_rows(bk, step, pos, seq_len)
        step *= 2
    cnt = jnp.minimum(pos + half, seq_len) - jnp.maximum(pos - half, 0)
    return (fw + bk) / cnt.astype(F32) - ug


def _l0a_kernel(*refs, seq_len, use_rope, mod_row):
    (x_ref, mod_ref, ng_ref, win_ref, qg_ref, kvg_ref, wuq_ref, wk_ref, wv_ref,
     qng_ref, kng_ref, pw_ref, ps_ref, ones_ref) = refs[:14]
    if use_rope:
        cs_ref = refs[14]
        outs = refs[15:]
    else:
        cs_ref = None
        outs = refs[14:]
    yp_ref, q_ref, k_ref, v_ref, ckv_ref, kr_ref = outs

    tm = x_ref.shape[0]
    row = mod_row(pl.program_id(0))
    scale = ng_ref[...] * (1.0 + _mod(mod_ref, 1, row))
    h = _rms(x_ref[...]) * scale + _mod(mod_ref, 0, row)
    z = _dot(h.astype(BF16), win_ref[...])
    o1, o2, o3 = POOL_WIDTH, POOL_WIDTH + Q_RANK, POOL_WIDTH + Q_RANK + KV_RANK

    ckv = _rms(z[:, o2:o3]) * kvg_ref[...]
    kr = z[:, o3:]
    ckv_ref[...] = ckv
    kr_ref[...] = kr[:, QK_NOPE:QK_DIM]
    _kv_heads(ckv.astype(BF16), kr, wk_ref, wv_ref, kng_ref, cs_ref, k_ref, v_ref)

    cqn = (_rms(z[:, o1:o2]) * qg_ref[...]).astype(BF16)
    qf = _dot(cqn, wuq_ref[...])
    qt = qng_ref[...]
    if use_rope:
        qt = qt * cs_ref[3]
    lane2 = lax.broadcasted_iota(jnp.int32, (tm, 2 * LANES), 1)
    for j in range(MLA_HEADS // 2):
        qp = qf[:, 2 * j * LANES:(2 * j + 2) * LANES]
        sq = qp * qp
        if j < Q_SS_MXU_PAIRS:
            hi = sq.astype(BF16)
            lo = (sq - hi.astype(F32)).astype(BF16)
            ss = _dot(hi, ones_ref[...]) + _dot(lo, ones_ref[...])
            qn = qp * lax.rsqrt(ss * (1.0 / QK_DIM) + EPS)
            parts = [qn[:, :LANES], qn[:, LANES:]]
        else:
            sq = jnp.where((lane2 & (LANES - 1)) < QK_DIM, sq, 0.0)
            parts = []
            for hh in range(2):
                ss = jnp.sum(sq[:, hh * LANES:(hh + 1) * LANES], axis=-1, keepdims=True)
                parts.append(qp[:, hh * LANES:(hh + 1) * LANES]
                             * lax.rsqrt(ss * (1.0 / QK_DIM) + EPS))
        for hh in range(2):
            hd = 2 * j + hh
            q_ref[:, hd * LANES:(hd + 1) * LANES] = (parts[hh] * qt).astype(BF16)

    pos = lax.broadcasted_iota(jnp.int32, (tm, POOL_CH), 0) & (seq_len - 1)
    for g, w in enumerate(POOL_WINDOWS):
        ug = z[:, g * POOL_CH:(g + 1) * POOL_CH]
        p = _pool_group(ug, w // 2, pos, seq_len)
        y = _dot(p.astype(BF16), pw_ref[g]) * ps_ref[:, g * POOL_CH:(g + 1) * POOL_CH]
        yp_ref[:, g * POOL_CH:(g + 1) * POOL_CH] = y.astype(BF16)


def _l0a_call(x, mod_all, layer, mod_row, tm, seq_len, w, cs):
    n_tok = x.shape[0]
    use_rope = cs is not None
    row_spec = lambda width: pl.BlockSpec((tm, width), lambda t: (t, 0))
    in_specs = [
        row_spec(D_MODEL),
        _mod_spec(layer),
        _const_spec((1, D_MODEL)),
        _const_spec((D_MODEL, AB_IN_PAD)),
        _const_spec((1, Q_RANK)),
        _const_spec((1, KV_RANK)),
        _const_spec((Q_RANK, HEAD_W)),
        _const_spec((KV_RANK, HEAD_W)),
        _const_spec((HEAD_W, KV_RANK)),
        _const_spec((1, LANES)),
        _const_spec((1, LANES)),
        _const_spec((POOL_GROUPS, POOL_CH, POOL_CH)),
        _const_spec((1, POOL_WIDTH)),
        _const_spec((2 * LANES, 2 * LANES)),
    ]
    args = [x, mod_all, w["norm_mix_g"], w["w_in"], w["q_norm_g"], w["kv_norm_g"], w["w_uq"],
            w["w_k"], w["w_vt"], w["qn_g_rope"] if use_rope else w["qn_g"], w["kn_g"],
            w["pool_w"], w["pool_scale"], w["head_ones"]]
    if use_rope:
        assert tm == seq_len
        in_specs.append(_const_spec((4, seq_len, LANES)))
        args.append(cs)
    out_shape = [
        jax.ShapeDtypeStruct((n_tok, POOL_WIDTH), BF16),
        jax.ShapeDtypeStruct((n_tok, HEAD_W), BF16),
        jax.ShapeDtypeStruct((n_tok, HEAD_W), BF16),
        jax.ShapeDtypeStruct((HEAD_W, n_tok), BF16),
        jax.ShapeDtypeStruct((n_tok, KV_RANK), F32),
        jax.ShapeDtypeStruct((n_tok, QK_ROPE), F32),
    ]
    out_specs = [row_spec(POOL_WIDTH), row_spec(HEAD_W), row_spec(HEAD_W),
                 pl.BlockSpec((HEAD_W, tm), lambda t: (0, t)),
                 row_spec(KV_RANK), row_spec(QK_ROPE)]
    return pl.pallas_call(
        functools.partial(_l0a_kernel, seq_len=seq_len, use_rope=use_rope, mod_row=mod_row),
        grid=(n_tok // tm,),
        in_specs=in_specs,
        out_specs=out_specs,
        out_shape=out_shape,
        compiler_params=_params(1),
        name="l0_front_rope" if use_rope else "l0_front",
    )(*args)


def _kvc_kernel(ckv_ref, kr_ref, wk_ref, wv_ref, kng_ref, k_ref, v_ref):
    _kv_heads(ckv_ref[...].astype(BF16), kr_ref[...], wk_ref, wv_ref, kng_ref, None, k_ref, v_ref)


def _kvc_call(ckv, kr_pad, w, tm):
    n_tok = ckv.shape[0]
    row_spec = lambda width: pl.BlockSpec((tm, width), lambda t: (t, 0))
    return pl.pallas_call(
        _kvc_kernel,
        grid=(n_tok // tm,),
        in_specs=[row_spec(KV_RANK), row_spec(LANES), _const_spec((KV_RANK, HEAD_W)),
                  _const_spec((HEAD_W, KV_RANK)), _const_spec((1, LANES))],
        out_specs=[row_spec(HEAD_W), pl.BlockSpec((HEAD_W, tm), lambda t: (0, t))],
        out_shape=[jax.ShapeDtypeStruct((n_tok, HEAD_W), BF16),
                   jax.ShapeDtypeStruct((HEAD_W, n_tok), BF16)],
        compiler_params=_params(1),
        name="cache_kv",
    )(ckv, kr_pad, w["w_k"], w["w_vt"], w["kn_g"])


ATT_TQ = 256
ATT_KC = 512
ATT_AHEAD = 6


def _att_kernel(xc_ref, ypc_ref, qc_ref, kc_ref, vc_ref,
                xl_ref, ypl_ref, ql_ref, kl_ref, vl_ref, kp_ref, vp_ref,
                mod_ref, wop_ref, woa_ref, o_ref, ya_ref, *, n_ctx_steps, n_q):
    t = pl.program_id(0)

    @pl.when(t < n_ctx_steps)
    def _():
        _att_tile(xc_ref, ypc_ref, qc_ref, kc_ref, vc_ref, None, None,
                  mod_ref, 0, wop_ref, woa_ref, o_ref, ya_ref)

    @pl.when(t >= n_ctx_steps)
    def _():
        _att_tile(xl_ref, ypl_ref, ql_ref, kl_ref, vl_ref, kp_ref, vp_ref,
                  mod_ref, 1 + (t - n_ctx_steps) // n_q, wop_ref, woa_ref, o_ref, ya_ref)


def _att_tile(x_ref, yp_ref, q_ref, k_ref, v_ref, kc_ref, vc_ref,
              mod_ref, row, wop_ref, woa_ref, o_ref, ya_ref):
    has_cache = kc_ref is not None
    sources = [(k_ref, v_ref)] if not has_cache else [(kc_ref, vc_ref), (k_ref, v_ref)]
    chunks = []
    for ks_ref, vs_ref in sources:
        n_keys = ks_ref.shape[0]
        kc = min(ATT_KC, n_keys)
        assert n_keys % kc == 0
        chunks += [(ks_ref, vs_ref, slice(c * kc, (c + 1) * kc)) for c in range(n_keys // kc)]
    items = [(hd, ch) for hd in range(MLA_HEADS) for ch in chunks]

    def scores(item):
        hd, (ks_ref, _, keys) = item
        sl = slice(hd * LANES, (hd + 1) * LANES)
        return _dot_nt(ks_ref[keys, sl], q_ref[:, sl])

    pending = [scores(it) for it in items[:ATT_AHEAD]]
    state = {}
    for n, (hd, (_, vs_ref, keys)) in enumerate(items):
        if n + ATT_AHEAD < len(items):
            pending.append(scores(items[n + ATT_AHEAD]))
        s = pending.pop(0)
        sl = slice(hd * LANES, (hd + 1) * LANES)
        cmax = jnp.max(s, axis=0, keepdims=True)
        if hd not in state:
            mx = cmax
            p = jnp.exp2(s - mx)
            den = jnp.sum(p, axis=0, keepdims=True)
            acc = _dot(vs_ref[sl, keys], p.astype(BF16))
        else:
            mx, den, acc = state[hd]
            new = jnp.maximum(mx, cmax)
            alpha = jnp.exp2(mx - new)
            p = jnp.exp2(s - new)
            den = alpha * den + jnp.sum(p, axis=0, keepdims=True)
            acc = alpha * acc + _dot(vs_ref[sl, keys], p.astype(BF16))
            mx = new
        state[hd] = (mx, den, acc)
        if hd % 2 == 1 and keys.stop == vs_ref.shape[1] and vs_ref is v_ref:
            j = hd // 2
            pair = state[hd - 1][2] / state[hd - 1][1] + acc / den
            ya_ref[j * LANES:(j + 1) * LANES, :] = pair.astype(BF16)

    y = _dot(yp_ref[...], wop_ref[...]) + _dot_tn(ya_ref[...], woa_ref[...])
    o_ref[...] = x_ref[...] + _mod(mod_ref, 2, row) * y


def _att_call(ctx_in, lat_in, cache, mod_all, layer, ctx_len, lat_len, w):
    assert ctx_len == ATT_TQ
    n_ctx, n_lat = ctx_in[0].shape[0], lat_in[0].shape[0]
    n_c = n_ctx // ATT_TQ
    n_q = lat_len // ATT_TQ
    past = cache[0].shape[0] // (n_lat // lat_len)
    cstep = lambda t: jnp.minimum(t, n_c - 1)
    lstep = lambda t: jnp.maximum(t - n_c, 0)
    crow = lambda width: pl.BlockSpec((ATT_TQ, width), lambda t: (cstep(t), 0))
    lrow = lambda width: pl.BlockSpec((ATT_TQ, width), lambda t: (lstep(t), 0))
    lbat = lambda rows: pl.BlockSpec((rows, HEAD_W), lambda t: (lstep(t) // n_q, 0))
    lbat_t = lambda cols: pl.BlockSpec((HEAD_W, cols), lambda t: (0, lstep(t) // n_q))
    in_specs = [
        crow(D_MODEL), crow(POOL_WIDTH), crow(HEAD_W), crow(HEAD_W),
        pl.BlockSpec((HEAD_W, ATT_TQ), lambda t: (0, cstep(t))),
        lrow(D_MODEL), lrow(POOL_WIDTH), lrow(HEAD_W), lbat(lat_len), lbat_t(lat_len),
        lbat(past), lbat_t(past),
        _mod_spec(layer), _const_spec((POOL_WIDTH, D_MODEL)), _const_spec((MLA_WIDTH, D_MODEL)),
    ]
    return pl.pallas_call(
        functools.partial(_att_kernel, n_ctx_steps=n_c, n_q=n_q),
        grid=(n_c + n_lat // ATT_TQ,),
        in_specs=in_specs,
        out_specs=pl.BlockSpec((ATT_TQ, D_MODEL), lambda t: (t, 0)),
        out_shape=jax.ShapeDtypeStruct((n_ctx + n_lat, D_MODEL), F32),
        scratch_shapes=[pltpu.VMEM((MLA_WIDTH, ATT_TQ), BF16)],
        compiler_params=_params(1),
        name="att_out",
    )(*ctx_in, *lat_in, *cache, mod_all, w["w_out_pool"], w["w_out_att"])


FFN_SUB = 256
FFN_GCH = D_MODEL // W_STEPS
FFN_DCH = D_FF // W_STEPS


def _ffn_kernel(x_ref, mod_ref, ng_ref, wg_ref, wu_ref, wd_ref, *rest, mod_row, split):
    outs, (wg_s, wu_s, wd_s) = rest[:-3], rest[-3:]
    t = pl.program_id(0)

    @pl.when(t < W_STEPS)
    def _():
        r = pl.multiple_of(t * FFN_GCH, FFN_GCH)
        wg_s[pl.ds(r, FFN_GCH), :] = wg_ref[0].astype(BF16)
        wu_s[pl.ds(r, FFN_GCH), :] = wu_ref[0].astype(BF16)
        r = pl.multiple_of(t * FFN_DCH, FFN_DCH)
        wd_s[pl.ds(r, FFN_DCH), :] = wd_ref[0].astype(BF16)

    def tile(o_ref):
        row = mod_row(t - W_STEPS)
        scale = ng_ref[...] * (1.0 + _mod(mod_ref, 4, row))
        shift = _mod(mod_ref, 3, row)
        gate = _mod(mod_ref, 5, row)
        for s in range(x_ref.shape[0] // FFN_SUB):
            rows = slice(s * FFN_SUB, (s + 1) * FFN_SUB)
            x = x_ref[rows, :]
            h = (_rms(x) * scale + shift).astype(BF16)
            g = _dot(h, wg_s[...])
            u = _dot(h, wu_s[...])
            d = _dot((_silu(g) * u).astype(BF16), wd_s[...])
            o_ref[rows, :] = x + gate * d

    if split is None:
        pl.when(t >= W_STEPS)(lambda: tile(outs[0]))
    else:
        pl.when((t >= W_STEPS) & (t < W_STEPS + split))(lambda: tile(outs[0]))
        pl.when(t >= W_STEPS + split)(lambda: tile(outs[1]))


def _ffn_call(x, mod_all, layer, mod_row, tm, ng, wg, wu, wd, n_first=None):
    n_tok = x.shape[0]
    n_tiles = n_tok // tm
    tok = lambda t: jnp.maximum(t - W_STEPS, 0)
    wstep = lambda t: jnp.minimum(t, W_STEPS - 1)
    if n_first is None:
        split = None
        out_specs = pl.BlockSpec((tm, D_MODEL), lambda t: (tok(t), 0))
        out_shape = jax.ShapeDtypeStruct((n_tok, D_MODEL), F32)
    else:
        split = n_first // tm
        out_specs = [
            pl.BlockSpec((tm, D_MODEL), lambda t: (jnp.minimum(tok(t), split - 1), 0)),
            pl.BlockSpec((tm, D_MODEL), lambda t: (jnp.maximum(tok(t) - split, 0), 0)),
        ]
        out_shape = [jax.ShapeDtypeStruct((n_first, D_MODEL), F32),
                     jax.ShapeDtypeStruct((n_tok - n_first, D_MODEL), F32)]
    return pl.pallas_call(
        functools.partial(_ffn_kernel, mod_row=mod_row, split=split),
        grid=(W_STEPS + n_tiles,),
        in_specs=[
            pl.BlockSpec((tm, D_MODEL), lambda t: (tok(t), 0)),
            _mod_spec(layer),
            _const_spec((1, D_MODEL)),
            pl.BlockSpec((1, FFN_GCH, D_FF), lambda t: (layer, wstep(t), 0)),
            pl.BlockSpec((1, FFN_GCH, D_FF), lambda t: (layer, wstep(t), 0)),
            pl.BlockSpec((1, FFN_DCH, D_MODEL), lambda t: (layer, wstep(t), 0)),
        ],
        out_specs=out_specs,
        out_shape=out_shape,
        scratch_shapes=[pltpu.VMEM((D_MODEL, D_FF), BF16), pltpu.VMEM((D_MODEL, D_FF), BF16),
                        pltpu.VMEM((D_FF, D_MODEL), BF16)],
        compiler_params=_params(1),
        name="ffn",
    )(x, mod_all, ng, wg, wu, wd)


GM_TM = 1024
GM_SUB = 256
GM_WCH = D_MODEL // W_STEPS


def _gmlp_kernel(x_ref, mod_ref, ng_ref, win_ref, vg_ref, ws_ref, bs_ref, wout_ref, o_ref,
                 win_s, wout_s, ws_s, gs_ref, *, mod_row):
    t = pl.program_id(0)

    @pl.when(t < W_STEPS)
    def _():
        r = pl.multiple_of(t * GM_WCH, GM_WCH)
        win_s[pl.ds(r, GM_WCH), :] = win_ref[0].astype(BF16)
        wout_s[pl.ds(r, GM_WCH), :] = wout_ref[0].astype(BF16)
        ws_s[t] = ws_ref[0, 0].astype(BF16)

    @pl.when(t >= W_STEPS)
    def _():
        row = mod_row(t - W_STEPS)
        scale = ng_ref[...] * (1.0 + _mod(mod_ref, 1, row))
        shift = _mod(mod_ref, 0, row)
        gate = _mod(mod_ref, 2, row)
        n_chunks = GM_SUB // CHUNK
        for s in range(x_ref.shape[0] // GM_SUB):
            rows = slice(s * GM_SUB, (s + 1) * GM_SUB)
            x = x_ref[rows, :]
            h = (_rms(x) * scale + shift).astype(BF16)
            z = _dot(h, win_s[...])
            vn = (_rms(z[:, C_WIDTH:]) * vg_ref[...]).astype(BF16)
            for g in range(C_GROUPS):
                cols = slice(g * C_CH, (g + 1) * C_CH)
                rhs = jnp.concatenate(
                    [vn[n * CHUNK:(n + 1) * CHUNK, cols] for n in range(n_chunks)], axis=1)
                sp = _dot(ws_s[g], rhs)
                for n in range(n_chunks):
                    crow = slice(n * CHUNK, (n + 1) * CHUNK)
                    grow = slice(s * GM_SUB + n * CHUNK, s * GM_SUB + (n + 1) * CHUNK)
                    sn = sp[:, n * C_CH:(n + 1) * C_CH] + bs_ref[g]
                    gs_ref[grow, cols] = (z[crow, cols] * sn).astype(BF16)
            o_ref[rows, :] = x + gate * _dot(gs_ref[rows, :], wout_s[...])


def _gmlp_call(x, mod_all, layer, mod_row, w, o):
    n_tok = x.shape[0]
    tok = lambda t: jnp.maximum(t - W_STEPS, 0)
    wstep = lambda t: jnp.minimum(t, W_STEPS - 1)
    assert C_GROUPS == W_STEPS
    return pl.pallas_call(
        functools.partial(_gmlp_kernel, mod_row=mod_row),
        grid=(W_STEPS + n_tok // GM_TM,),
        in_specs=[
            pl.BlockSpec((GM_TM, D_MODEL), lambda t: (tok(t), 0)),
            _mod_spec(layer),
            _const_spec((1, D_MODEL)),
            pl.BlockSpec((1, GM_WCH, 2 * C_WIDTH), lambda t: (o, wstep(t), 0)),
            _const_spec((1, C_WIDTH)),
            pl.BlockSpec((1, 1, CHUNK, CHUNK), lambda t: (o, wstep(t), 0, 0)),
            _const_spec((C_GROUPS, CHUNK, C_CH)),
            pl.BlockSpec((1, GM_WCH, D_MODEL), lambda t: (o, wstep(t), 0)),
        ],
        out_specs=pl.BlockSpec((GM_TM, D_MODEL), lambda t: (tok(t), 0)),
        out_shape=jax.ShapeDtypeStruct((n_tok, D_MODEL), F32),
        scratch_shapes=[pltpu.VMEM((D_MODEL, 2 * C_WIDTH), BF16), pltpu.VMEM((C_WIDTH, D_MODEL), BF16),
                        pltpu.VMEM((C_GROUPS, CHUNK, CHUNK), BF16), pltpu.VMEM((GM_TM, C_WIDTH), BF16)],
        compiler_params=_params(1),
        name="gmlp",
    )(x, mod_all, w["norm_mix_g"], w["w_in"], w["vnorm_g"], w["w_s"], w["b_s"], w["w_out"])


def _swap_halves(t):
    return jnp.concatenate([t[..., HALF:], t[..., :HALF]], axis=-1)


def _head_ones():
    lane = np.arange(2 * LANES)
    m = (lane[:, None] // LANES == lane[None, :] // LANES) & (lane[:, None] % LANES < QK_DIM)
    return jnp.asarray(m.astype(np.float32), dtype=BF16)


def _layer0_weights(e, norm_mix_g, ab_w_in, pool_w, pool_scale, q_norm_g, kv_norm_g, w_uq, w_ukv,
                    qn_g, kn_g, ab_w_out):
    w_in = ab_w_in[e]
    o3 = POOL_WIDTH + Q_RANK + KV_RANK
    kr_cols = jnp.concatenate(
        [jnp.zeros((D_MODEL, QK_NOPE), F32), w_in[:, o3:], w_in[:, o3:]], axis=1)
    w_in_p = jnp.concatenate([w_in[:, :o3], kr_cols], axis=1).astype(BF16)
    uq = w_uq[e].reshape(Q_RANK, MLA_HEADS, QK_DIM)
    uq = jnp.concatenate([uq, _swap_halves(uq[:, :, QK_NOPE:])], axis=-1)
    score_scale = QK_DIM ** -0.5 * math.log2(math.e)
    gq = qn_g[e] * score_scale
    gk = kn_g[e]
    ukv = w_ukv[e].reshape(KV_RANK, MLA_HEADS, QK_NOPE + V_DIM)
    w_k = jnp.pad(ukv[:, :, :QK_NOPE], ((0, 0), (0, 0), (0, LANES - QK_NOPE)))
    w_v = ukv[:, :, QK_NOPE:].reshape(KV_RANK, MLA_HEADS // 2, 2, V_DIM)
    zero = jnp.zeros_like(w_v[:, :, 0])
    w_v = jnp.stack([jnp.concatenate([w_v[:, :, 0], zero], axis=-1),
                     jnp.concatenate([zero, w_v[:, :, 1]], axis=-1)], axis=2)
    return {
        "norm_mix_g": norm_mix_g.reshape(1, D_MODEL),
        "w_in": w_in_p,
        "q_norm_g": q_norm_g[e].reshape(1, Q_RANK),
        "kv_norm_g": kv_norm_g[e].reshape(1, KV_RANK),
        "w_uq": uq.reshape(Q_RANK, HEAD_W).astype(BF16),
        "w_k": w_k.reshape(KV_RANK, HEAD_W).astype(BF16),
        "w_vt": w_v.reshape(KV_RANK, HEAD_W).T.astype(BF16),
        "qn_g": jnp.pad(gq, (0, LANES - QK_DIM)).reshape(1, LANES),
        "qn_g_rope": jnp.concatenate([gq, _swap_halves(gq[QK_NOPE:])]).reshape(1, LANES),
        "kn_g": jnp.concatenate([gk, gk[QK_NOPE:]]).reshape(1, LANES),
        "head_ones": _head_ones(),
        "pool_w": pool_w[e].astype(BF16),
        "pool_scale": pool_scale[e].reshape(1, POOL_WIDTH),
        "w_out_pool": ab_w_out[e, :POOL_WIDTH].astype(BF16),
        "w_out_att": ab_w_out[e, POOL_WIDTH:].astype(BF16),
    }


def _rope_tables(seq_len):
    rows = seq_len // GRID_W
    row = np.repeat(np.arange(rows), GRID_W).astype(np.float32)
    col = np.tile(np.arange(GRID_W), rows).astype(np.float32)
    per_axis = QK_ROPE // 2
    inv = (1.0 / (np.float32(ROPE_BASE) ** (np.arange(0, per_axis, 2, dtype=np.float32) / per_axis))
           ).astype(np.float32)
    ang = np.concatenate([row[:, None] * inv, col[:, None] * inv], axis=-1)
    cos, sin = np.cos(ang).astype(np.float32), np.sin(ang).astype(np.float32)
    ones = np.ones((seq_len, QK_NOPE), np.float32)
    z_nope = np.zeros((seq_len, QK_NOPE), np.float32)
    z_half = np.zeros((seq_len, HALF), np.float32)
    c_tab = np.concatenate([ones, cos, cos, cos, cos], axis=-1)
    a_tab = np.concatenate([z_nope, -sin, z_half, -sin, z_half], axis=-1)
    b_tab = np.concatenate([z_nope, z_half, sin, z_half, sin], axis=-1)
    q_tab = np.concatenate([ones, cos, cos, -sin, sin], axis=-1)
    return jnp.asarray(np.stack([c_tab, a_tab, b_tab, q_tab]))


def kernel(x_prompt, x_sample, cache_ckv, cache_krope, c, c_ctx, ada_w, ada_b, norm_mix_g, norm_ffn_g, ffn_wg, ffn_wu, ffn_wd, ab_w_in, pool_w, pool_scale, q_norm_g, kv_norm_g, w_uq, w_ukv, qn_g, kn_g, ab_w_out, gm_w_in, gm_vnorm_g, gm_ws, gm_bs, gm_w_out):
    n_ctx_b, ctx_len, _ = x_prompt.shape
    n_lat_b, lat_len, _ = x_sample.shape
    past = cache_ckv.shape[2]
    assert ctx_len & (ctx_len - 1) == 0 and lat_len & (lat_len - 1) == 0
    assert MOD_ROWS >= 1 + n_lat_b
    assert DEPTH == 2
    n_ctx, n_lat = n_ctx_b * ctx_len, n_lat_b * lat_len
    n_all = n_ctx + n_lat

    ctx = x_prompt.reshape(n_ctx, D_MODEL)
    lat = x_sample.reshape(n_lat, D_MODEL)

    cond = jnp.concatenate(
        [c_ctx[None, :], c, jnp.zeros((MOD_ROWS - 1 - n_lat_b, D_MODEL), F32)], axis=0)
    mod_all = _ada_call(cond, ada_w, ada_b)

    def all_row(tm):
        n_ctx_tiles, per = n_ctx // tm, lat_len // tm
        return lambda j: jnp.where(j < n_ctx_tiles, 0, 1 + (j - n_ctx_tiles) // per)

    w = _layer0_weights(0, norm_mix_g[0], ab_w_in, pool_w, pool_scale, q_norm_g, kv_norm_g,
                        w_uq, w_ukv, qn_g, kn_g, ab_w_out)
    cs = _rope_tables(lat_len)
    yp_c, q_c, k_c, v_c, ckv_c, kr_c = _l0a_call(ctx, mod_all, 0, lambda t: 0, 1024, ctx_len, w, None)
    yp_l, q_l, k_l, v_l, _, _ = _l0a_call(lat, mod_all, 0, lambda t: 1 + t, lat_len, lat_len, w, cs)
    kr_c2 = cache_krope[:, 0].reshape(n_lat_b * past, QK_ROPE)
    kr_pad = jnp.concatenate([jnp.zeros((n_lat_b * past, QK_NOPE), F32), kr_c2, kr_c2], axis=1)
    cache = _kvc_call(cache_ckv[:, 0].reshape(n_lat_b * past, KV_RANK), kr_pad, w, 1024)
    xs = _att_call((ctx, yp_c, q_c, k_c, v_c), (lat, yp_l, q_l, k_l, v_l), cache,
                   mod_all, 0, ctx_len, lat_len, w)
    xs = _ffn_call(xs, mod_all, 0, all_row(1024), 1024, norm_ffn_g[0].reshape(1, D_MODEL),
                   ffn_wg, ffn_wu, ffn_wd)

    gw = {
        "norm_mix_g": norm_mix_g[1].reshape(1, D_MODEL),
        "w_in": gm_w_in,
        "vnorm_g": gm_vnorm_g[0].reshape(1, C_WIDTH),
        "w_s": gm_ws,
        "b_s": jnp.broadcast_to(gm_bs[0][:, :, None], (C_GROUPS, CHUNK, C_CH)),
        "w_out": gm_w_out,
    }
    xs = _gmlp_call(xs, mod_all, 1, all_row(GM_TM), gw, 0)
    y_ctx, y_lat = _ffn_call(xs, mod_all, 1, all_row(512), 512, norm_ffn_g[1].reshape(1, D_MODEL),
                             ffn_wg, ffn_wu, ffn_wd, n_first=n_ctx)

    state_ckv = ckv_c.reshape(n_ctx_b, 1, ctx_len, KV_RANK)
    state_krope = kr_c.reshape(n_ctx_b, 1, ctx_len, QK_ROPE)
    return (y_ctx.reshape(n_ctx_b, ctx_len, D_MODEL), y_lat.reshape(n_lat_b, lat_len, D_MODEL),
            state_ckv, state_krope)
```

```python
import functools
import math

import numpy as np
import jax
import jax.numpy as jnp
from jax import lax
from jax.experimental import pallas as pl
from jax.experimental.pallas import tpu as pltpu

D_MODEL = 1024
DEPTH = 2
GRID_W = 64
POOL_WINDOWS = (2, 4, 8, 16)
POOL_GROUPS = 4
POOL_CH = 128
POOL_WIDTH = POOL_GROUPS * POOL_CH
MLA_HEADS = 8
QK_NOPE = 64
QK_ROPE = 32
QK_DIM = QK_NOPE + QK_ROPE
HALF = QK_ROPE // 2
SQRT_QK = math.sqrt(QK_DIM)
V_DIM = 64
Q_RANK = 384
KV_RANK = 256
MLA_WIDTH = MLA_HEADS * V_DIM
CHUNK = 128
C_GROUPS = 8
C_WIDTH = D_MODEL
C_CH = C_WIDTH // C_GROUPS
D_FF = 2816
ROPE_BASE = 10000.0
EPS = 1e-6

LANES = 128
HEAD_W = MLA_HEADS * LANES
AB_IN_PAD = POOL_WIDTH + Q_RANK + KV_RANK + LANES
MOD_ROWS = 16
VMEM_LIMIT = 58 * 1024 * 1024
W_STEPS = 8
Q_SS_MXU_PAIRS = 2

F32 = jnp.float32
BF16 = jnp.bfloat16


def _rms(x):
    return x * lax.rsqrt(jnp.mean(x * x, axis=-1, keepdims=True) + EPS)


def _dot(a, b):
    return jnp.dot(a, b, preferred_element_type=F32)


def _dot_nt(a, b):
    return lax.dot_general(a, b, (((1,), (1,)), ((), ())), preferred_element_type=F32)


def _dot_tn(a, b):
    return lax.dot_general(a, b, (((0,), (0,)), ((), ())), preferred_element_type=F32)


def _silu(x):
    return x * jax.nn.sigmoid(x)


def _params(n_axes):
    return pltpu.CompilerParams(
        dimension_semantics=("arbitrary",) * n_axes, vmem_limit_bytes=VMEM_LIMIT)


def _const_spec(shape):
    zeros = (0,) * len(shape)
    return pl.BlockSpec(shape, lambda *_: zeros)


def _mod_spec(layer):
    return pl.BlockSpec((1, 6, MOD_ROWS, D_MODEL), lambda *_: (layer, 0, 0, 0))


def _mod(mod_ref, term, row):
    return mod_ref[0, term, pl.ds(row, 1), :]


ADA_TERMS = 2


def _ada_kernel(cond_ref, w_ref, b_ref, o_ref):
    s = _silu(cond_ref[...]).astype(BF16)
    bias = b_ref[pl.ds(pl.program_id(0), 1), :]
    y = _dot(s, w_ref[0].astype(BF16)) + bias
    for k in range(ADA_TERMS):
        o_ref[0, k] = y[:, k * D_MODEL:(k + 1) * D_MODEL]


def _ada_call(cond, ada_w, ada_b):
    width = ADA_TERMS * D_MODEL
    return pl.pallas_call(
        _ada_kernel,
        grid=(DEPTH, 6 // ADA_TERMS),
        in_specs=[
            pl.BlockSpec((MOD_ROWS, D_MODEL), lambda i, j: (0, 0)),
            pl.BlockSpec((1, D_MODEL, width), lambda i, j: (i, 0, j)),
            pl.BlockSpec((DEPTH, width), lambda i, j: (0, j)),
        ],
        out_specs=pl.BlockSpec((1, ADA_TERMS, MOD_ROWS, D_MODEL), lambda i, j: (i, j, 0, 0)),
        out_shape=jax.ShapeDtypeStruct((DEPTH, 6, MOD_ROWS, D_MODEL), F32),
        compiler_params=_params(2),
        name="ada_mod",
    )(cond, ada_w, ada_b)


def _rope(t, cs_ref):
    return (t * cs_ref[0]
            + pltpu.roll(t, LANES - HALF, axis=1) * cs_ref[1]
            + pltpu.roll(t, HALF, axis=1) * cs_ref[2])


def _kv_heads(ckv_bf, kr, wk_ref, wv_ref, kng_ref, cs_ref, k_ref, v_ref):
    v_ref[...] = _dot_nt(wv_ref[...], ckv_bf).astype(BF16)
    kpre = _dot(ckv_bf, wk_ref[...])
    kg = kng_ref[...] * SQRT_QK
    krg = kr * kg
    if cs_ref is not None:
        krg = _rope(krg, cs_ref)
    lane = lax.broadcasted_iota(jnp.int32, kr.shape, 1)
    kr_ss = jnp.sum(jnp.where(lane < QK_DIM, kr * kr, 0.0), axis=-1, keepdims=True) + QK_DIM * EPS
    for h in range(MLA_HEADS):
        kh = kpre[:, h * LANES:(h + 1) * LANES]
        r = lax.rsqrt(jnp.sum(kh * kh, axis=-1, keepdims=True) + kr_ss)
        k_ref[:, h * LANES:(h + 1) * LANES] = ((kh * kg + krg) * r).astype(BF16)


def _shift_rows(a, k, pos, seq_len):
    n = a.shape[0]
    r = pltpu.roll(a, k % n, axis=0)
    src = pos - k
    ok = (src >= 0) if k > 0 else (src < seq_len)
    return jnp.where(ok, r, 0.0)


def _pool_group(ug, half, pos, seq_len):
    fw = ug
    step = 1
    while step < half:
        fw = fw + _shift_rows(fw, -step, pos, seq_len)
        step *= 2
    bk = _shift_rows(ug, 1, pos, seq_len)
    step = 1
    while step < half:
        bk = bk + _shift_rows(bk, step, pos, seq_len)
        step *= 2
    cnt = jnp.minimum(pos + half, seq_len) - jnp.maximum(pos - half, 0)
    return (fw + bk) / cnt.astype(F32) - ug


def _l0a_kernel(*refs, seq_len, use_rope, mod_row):
    (x_ref, mod_ref, ng_ref, win_ref, qg_ref, kvg_ref, wuq_ref, wk_ref, wv_ref,
     qng_ref, kng_ref, pw_ref, ps_ref, ones_ref) = refs[:14]
    if use_rope:
        cs_ref = refs[14]
        outs = refs[15:]
    else:
        cs_ref = None
        outs = refs[14:]
    yp_ref, q_ref, k_ref, v_ref, ckv_ref, kr_ref = outs

    tm = x_ref.shape[0]
    row = mod_row(pl.program_id(0))
    scale = ng_ref[...] * (1.0 + _mod(mod_ref, 1, row))
    h = _rms(x_ref[...]) * scale + _mod(mod_ref, 0, row)
    z = _dot(h.astype(BF16), win_ref[...])
    o1, o2, o3 = POOL_WIDTH, POOL_WIDTH + Q_RANK, POOL_WIDTH + Q_RANK + KV_RANK

    ckv = _rms(z[:, o2:o3]) * kvg_ref[...]
    kr = z[:, o3:]
    ckv_ref[...] = ckv
    kr_ref[...] = kr[:, QK_NOPE:QK_DIM]
    _kv_heads(ckv.astype(BF16), kr, wk_ref, wv_ref, kng_ref, cs_ref, k_ref, v_ref)

    cqn = (_rms(z[:, o1:o2]) * qg_ref[...]).astype(BF16)
    qf = _dot(cqn, wuq_ref[...])
    qt = qng_ref[...] * SQRT_QK
    if use_rope:
        qt = qt * cs_ref[3]
    lane2 = lax.broadcasted_iota(jnp.int32, (tm, 2 * LANES), 1)
    for j in range(MLA_HEADS // 2):
        qp = qf[:, 2 * j * LANES:(2 * j + 2) * LANES]
        sq = qp * qp
        if j < Q_SS_MXU_PAIRS:
            hi = sq.astype(BF16)
            lo = (sq - hi.astype(F32)).astype(BF16)
            ss = _dot(hi, ones_ref[...]) + _dot(lo, ones_ref[...])
            qn = qp * lax.rsqrt(ss + QK_DIM * EPS)
            parts = [qn[:, :LANES], qn[:, LANES:]]
        else:
            sq = jnp.where((lane2 & (LANES - 1)) < QK_DIM, sq, 0.0)
            parts = []
            for hh in range(2):
                ss = jnp.sum(sq[:, hh * LANES:(hh + 1) * LANES], axis=-1, keepdims=True)
                parts.append(qp[:, hh * LANES:(hh + 1) * LANES] * lax.rsqrt(ss + QK_DIM * EPS))
        for hh in range(2):
            hd = 2 * j + hh
            q_ref[:, hd * LANES:(hd + 1) * LANES] = (parts[hh] * qt).astype(BF16)

    pos = lax.broadcasted_iota(jnp.int32, (tm, POOL_CH), 0) & (seq_len - 1)
    for g, w in enumerate(POOL_WINDOWS):
        ug = z[:, g * POOL_CH:(g + 1) * POOL_CH]
        p = _pool_group(ug, w // 2, pos, seq_len)
        y = _dot(p.astype(BF16), pw_ref[g]) * ps_ref[:, g * POOL_CH:(g + 1) * POOL_CH]
        yp_ref[:, g * POOL_CH:(g + 1) * POOL_CH] = y.astype(BF16)


def _l0a_call(x, mod_all, layer, mod_row, tm, seq_len, w, cs):
    n_tok = x.shape[0]
    use_rope = cs is not None
    row_spec = lambda width: pl.BlockSpec((tm, width), lambda t: (t, 0))
    in_specs = [
        row_spec(D_MODEL),
        _mod_spec(layer),
        _const_spec((1, D_MODEL)),
        _const_spec((D_MODEL, AB_IN_PAD)),
        _const_spec((1, Q_RANK)),
        _const_spec((1, KV_RANK)),
        _const_spec((Q_RANK, HEAD_W)),
        _const_spec((KV_RANK, HEAD_W)),
        _const_spec((HEAD_W, KV_RANK)),
        _const_spec((1, LANES)),
        _const_spec((1, LANES)),
        _const_spec((POOL_GROUPS, POOL_CH, POOL_CH)),
        _const_spec((1, POOL_WIDTH)),
        _const_spec((2 * LANES, 2 * LANES)),
    ]
    args = [x, mod_all, w["norm_mix_g"], w["w_in"], w["q_norm_g"], w["kv_norm_g"], w["w_uq"],
            w["w_k"], w["w_vt"], w["qn_g_rope"] if use_rope else w["qn_g"], w["kn_g"],
            w["pool_w"], w["pool_scale"], w["head_ones"]]
    if use_rope:
        assert tm == seq_len
        in_specs.append(_const_spec((4, seq_len, LANES)))
        args.append(cs)
    out_shape = [
        jax.ShapeDtypeStruct((n_tok, POOL_WIDTH), BF16),
        jax.ShapeDtypeStruct((n_tok, HEAD_W), BF16),
        jax.ShapeDtypeStruct((n_tok, HEAD_W), BF16),
        jax.ShapeDtypeStruct((HEAD_W, n_tok), BF16),
        jax.ShapeDtypeStruct((n_tok, KV_RANK), F32),
        jax.ShapeDtypeStruct((n_tok, QK_ROPE), F32),
    ]
    out_specs = [row_spec(POOL_WIDTH), row_spec(HEAD_W), row_spec(HEAD_W),
                 pl.BlockSpec((HEAD_W, tm), lambda t: (0, t)),
                 row_spec(KV_RANK), row_spec(QK_ROPE)]
    return pl.pallas_call(
        functools.partial(_l0a_kernel, seq_len=seq_len, use_rope=use_rope, mod_row=mod_row),
        grid=(n_tok // tm,),
        in_specs=in_specs,
        out_specs=out_specs,
        out_shape=out_shape,
        compiler_params=_params(1),
        name="l0_front_rope" if use_rope else "l0_front",
    )(*args)


def _kvc_kernel(ckv_ref, kr_ref, wk_ref, wv_ref, kng_ref, k_ref, v_ref):
    _kv_heads(ckv_ref[...].astype(BF16), kr_ref[...], wk_ref, wv_ref, kng_ref, None, k_ref, v_ref)


def _kvc_call(ckv, kr_pad, w, tm):
    n_tok = ckv.shape[0]
    row_spec = lambda width: pl.BlockSpec((tm, width), lambda t: (t, 0))
    return pl.pallas_call(
        _kvc_kernel,
        grid=(n_tok // tm,),
        in_specs=[row_spec(KV_RANK), row_spec(LANES), _const_spec((KV_RANK, HEAD_W)),
                  _const_spec((HEAD_W, KV_RANK)), _const_spec((1, LANES))],
        out_specs=[row_spec(HEAD_W), pl.BlockSpec((HEAD_W, tm), lambda t: (0, t))],
        out_shape=[jax.ShapeDtypeStruct((n_tok, HEAD_W), BF16),
                   jax.ShapeDtypeStruct((HEAD_W, n_tok), BF16)],
        compiler_params=_params(1),
        name="cache_kv",
    )(ckv, kr_pad, w["w_k"], w["w_vt"], w["kn_g"])


ATT_TQ = 256
ATT_KC = 512
ATT_AHEAD = 6


def _att_kernel(xc_ref, ypc_ref, qc_ref, kc_ref, vc_ref,
                xl_ref, ypl_ref, ql_ref, kl_ref, vl_ref, kp_ref, vp_ref,
                mod_ref, wop_ref, woa_ref, o_ref, ya_ref, *, n_ctx_steps, n_q):
    t = pl.program_id(0)

    @pl.when(t < n_ctx_steps)
    def _():
        _att_tile(xc_ref, ypc_ref, qc_ref, kc_ref, vc_ref, None, None,
                  mod_ref, 0, wop_ref, woa_ref, o_ref, ya_ref)

    @pl.when(t >= n_ctx_steps)
    def _():
        _att_tile(xl_ref, ypl_ref, ql_ref, kl_ref, vl_ref, kp_ref, vp_ref,
                  mod_ref, 1 + (t - n_ctx_steps) // n_q, wop_ref, woa_ref, o_ref, ya_ref)


def _att_tile(x_ref, yp_ref, q_ref, k_ref, v_ref, kc_ref, vc_ref,
              mod_ref, row, wop_ref, woa_ref, o_ref, ya_ref):
    has_cache = kc_ref is not None
    sources = [(k_ref, v_ref)] if not has_cache else [(kc_ref, vc_ref), (k_ref, v_ref)]
    chunks = []
    for ks_ref, vs_ref in sources:
        n_keys = ks_ref.shape[0]
        kc = min(ATT_KC, n_keys)
        assert n_keys % kc == 0
        chunks += [(ks_ref, vs_ref, slice(c * kc, (c + 1) * kc)) for c in range(n_keys // kc)]
    items = [(hd, ch) for hd in range(MLA_HEADS) for ch in chunks]

    def scores(item):
        hd, (ks_ref, _, keys) = item
        sl = slice(hd * LANES, (hd + 1) * LANES)
        return _dot_nt(ks_ref[keys, sl], q_ref[:, sl])

    pending = [scores(it) for it in items[:ATT_AHEAD]]
    state = {}
    for n, (hd, (_, vs_ref, keys)) in enumerate(items):
        if n + ATT_AHEAD < len(items):
            pending.append(scores(items[n + ATT_AHEAD]))
        s = pending.pop(0)
        sl = slice(hd * LANES, (hd + 1) * LANES)
        cmax = jnp.max(s, axis=0, keepdims=True)
        if hd not in state:
            mx = cmax
            p = jnp.exp2(s - mx)
            den = jnp.sum(p, axis=0, keepdims=True)
            acc = _dot(vs_ref[sl, keys], p.astype(BF16))
        else:
            mx, den, acc = state[hd]
            new = jnp.maximum(mx, cmax)
            alpha = jnp.exp2(mx - new)
            p = jnp.exp2(s - new)
            den = alpha * den + jnp.sum(p, axis=0, keepdims=True)
            acc = alpha * acc + _dot(vs_ref[sl, keys], p.astype(BF16))
            mx = new
        state[hd] = (mx, den, acc)
        if hd % 2 == 1 and keys.stop == vs_ref.shape[1] and vs_ref is v_ref:
            j = hd // 2
            pair = state[hd - 1][2] / state[hd - 1][1] + acc / den
            ya_ref[j * LANES:(j + 1) * LANES, :] = pair.astype(BF16)

    y = _dot(yp_ref[...], wop_ref[...]) + _dot_tn(ya_ref[...], woa_ref[...])
    o_ref[...] = x_ref[...] + _mod(mod_ref, 2, row) * y


def _att_call(ctx_in, lat_in, cache, mod_all, layer, ctx_len, lat_len, w):
    assert ctx_len == ATT_TQ
    n_ctx, n_lat = ctx_in[0].shape[0], lat_in[0].shape[0]
    n_c = n_ctx // ATT_TQ
    n_q = lat_len // ATT_TQ
    past = cache[0].shape[0] // (n_lat // lat_len)
    cstep = lambda t: jnp.minimum(t, n_c - 1)
    lstep = lambda t: jnp.maximum(t - n_c, 0)
    crow = lambda width: pl.BlockSpec((ATT_TQ, width), lambda t: (cstep(t), 0))
    lrow = lambda width: pl.BlockSpec((ATT_TQ, width), lambda t: (lstep(t), 0))
    lbat = lambda rows: pl.BlockSpec((rows, HEAD_W), lambda t: (lstep(t) // n_q, 0))
    lbat_t = lambda cols: pl.BlockSpec((HEAD_W, cols), lambda t: (0, lstep(t) // n_q))
    in_specs = [
        crow(D_MODEL), crow(POOL_WIDTH), crow(HEAD_W), crow(HEAD_W),
        pl.BlockSpec((HEAD_W, ATT_TQ), lambda t: (0, cstep(t))),
        lrow(D_MODEL), lrow(POOL_WIDTH), lrow(HEAD_W), lbat(lat_len), lbat_t(lat_len),
        lbat(past), lbat_t(past),
        _mod_spec(layer), _const_spec((POOL_WIDTH, D_MODEL)), _const_spec((MLA_WIDTH, D_MODEL)),
    ]
    return pl.pallas_call(
        functools.partial(_att_kernel, n_ctx_steps=n_c, n_q=n_q),
        grid=(n_c + n_lat // ATT_TQ,),
        in_specs=in_specs,
        out_specs=pl.BlockSpec((ATT_TQ, D_MODEL), lambda t: (t, 0)),
        out_shape=jax.ShapeDtypeStruct((n_ctx + n_lat, D_MODEL), F32),
        scratch_shapes=[pltpu.VMEM((MLA_WIDTH, ATT_TQ), BF16)],
        compiler_params=_params(1),
        name="att_out",
    )(*ctx_in, *lat_in, *cache, mod_all, w["w_out_pool"], w["w_out_att"])


FFN_SUB = 256
FFN_GCH = D_MODEL // W_STEPS
FFN_DCH = D_FF // W_STEPS


def _ffn_kernel(x_ref, mod_ref, ng_ref, wg_ref, wu_ref, wd_ref, *rest, mod_row, split):
    outs, (wg_s, wu_s, wd_s) = rest[:-3], rest[-3:]
    t = pl.program_id(0)

    @pl.when(t < W_STEPS)
    def _():
        r = pl.multiple_of(t * FFN_GCH, FFN_GCH)
        wg_s[pl.ds(r, FFN_GCH), :] = wg_ref[0].astype(BF16)
        wu_s[pl.ds(r, FFN_GCH), :] = wu_ref[0].astype(BF16)
        r = pl.multiple_of(t * FFN_DCH, FFN_DCH)
        wd_s[pl.ds(r, FFN_DCH), :] = wd_ref[0].astype(BF16)

    def tile(o_ref):
        row = mod_row(t - W_STEPS)
        scale = ng_ref[...] * (1.0 + _mod(mod_ref, 4, row))
        shift = _mod(mod_ref, 3, row)
        gate = _mod(mod_ref, 5, row)
        for s in range(x_ref.shape[0] // FFN_SUB):
            rows = slice(s * FFN_SUB, (s + 1) * FFN_SUB)
            x = x_ref[rows, :]
            h = (_rms(x) * scale + shift).astype(BF16)
            g = _dot(h, wg_s[...])
            u = _dot(h, wu_s[...])
            d = _dot((_silu(g) * u).astype(BF16), wd_s[...])
            o_ref[rows, :] = x + gate * d

    if split is None:
        pl.when(t >= W_STEPS)(lambda: tile(outs[0]))
    else:
        pl.when((t >= W_STEPS) & (t < W_STEPS + split))(lambda: tile(outs[0]))
        pl.when(t >= W_STEPS + split)(lambda: tile(outs[1]))


def _ffn_call(x, mod_all, layer, mod_row, tm, ng, wg, wu, wd, n_first=None):
    n_tok = x.shape[0]
    n_tiles = n_tok // tm
    tok = lambda t: jnp.maximum(t - W_STEPS, 0)
    wstep = lambda t: jnp.minimum(t, W_STEPS - 1)
    if n_first is None:
        split = None
        out_specs = pl.BlockSpec((tm, D_MODEL), lambda t: (tok(t), 0))
        out_shape = jax.ShapeDtypeStruct((n_tok, D_MODEL), F32)
    else:
        split = n_first // tm
        out_specs = [
            pl.BlockSpec((tm, D_MODEL), lambda t: (jnp.minimum(tok(t), split - 1), 0)),
            pl.BlockSpec((tm, D_MODEL), lambda t: (jnp.maximum(tok(t) - split, 0), 0)),
        ]
        out_shape = [jax.ShapeDtypeStruct((n_first, D_MODEL), F32),
                     jax.ShapeDtypeStruct((n_tok - n_first, D_MODEL), F32)]
    return pl.pallas_call(
        functools.partial(_ffn_kernel, mod_row=mod_row, split=split),
        grid=(W_STEPS + n_tiles,),
        in_specs=[
            pl.BlockSpec((tm, D_MODEL), lambda t: (tok(t), 0)),
            _mod_spec(layer),
            _const_spec((1, D_MODEL)),
            pl.BlockSpec((1, FFN_GCH, D_FF), lambda t: (layer, wstep(t), 0)),
            pl.BlockSpec((1, FFN_GCH, D_FF), lambda t: (layer, wstep(t), 0)),
            pl.BlockSpec((1, FFN_DCH, D_MODEL), lambda t: (layer, wstep(t), 0)),
        ],
        out_specs=out_specs,
        out_shape=out_shape,
        scratch_shapes=[pltpu.VMEM((D_MODEL, D_FF), BF16), pltpu.VMEM((D_MODEL, D_FF), BF16),
                        pltpu.VMEM((D_FF, D_MODEL), BF16)],
        compiler_params=_params(1),
        name="ffn",
    )(x, mod_all, ng, wg, wu, wd)


GM_TM = 1024
GM_SUB = 256
GM_WCH = D_MODEL // W_STEPS


def _gmlp_kernel(x_ref, mod_ref, ng_ref, win_ref, vg_ref, ws_ref, bs_ref, wout_ref, o_ref,
                 win_s, wout_s, ws_s, gs_ref, *, mod_row):
    t = pl.program_id(0)

    @pl.when(t < W_STEPS)
    def _():
        r = pl.multiple_of(t * GM_WCH, GM_WCH)
        win_s[pl.ds(r, GM_WCH), :] = win_ref[0].astype(BF16)
        wout_s[pl.ds(r, GM_WCH), :] = wout_ref[0].astype(BF16)
        ws_s[t] = ws_ref[0, 0].astype(BF16)

    @pl.when(t >= W_STEPS)
    def _():
        row = mod_row(t - W_STEPS)
        scale = ng_ref[...] * (1.0 + _mod(mod_ref, 1, row))
        shift = _mod(mod_ref, 0, row)
        gate = _mod(mod_ref, 2, row)
        n_chunks = GM_SUB // CHUNK
        for s in range(x_ref.shape[0] // GM_SUB):
            rows = slice(s * GM_SUB, (s + 1) * GM_SUB)
            x = x_ref[rows, :]
            h = (_rms(x) * scale + shift).astype(BF16)
            z = _dot(h, win_s[...])
            vn = (_rms(z[:, C_WIDTH:]) * vg_ref[...]).astype(BF16)
            for g in range(C_GROUPS):
                cols = slice(g * C_CH, (g + 1) * C_CH)
                rhs = jnp.concatenate(
                    [vn[n * CHUNK:(n + 1) * CHUNK, cols] for n in range(n_chunks)], axis=1)
                sp = _dot(ws_s[g], rhs)
                for n in range(n_chunks):
                    crow = slice(n * CHUNK, (n + 1) * CHUNK)
                    grow = slice(s * GM_SUB + n * CHUNK, s * GM_SUB + (n + 1) * CHUNK)
                    sn = sp[:, n * C_CH:(n + 1) * C_CH] + bs_ref[g]
                    gs_ref[grow, cols] = (z[crow, cols] * sn).astype(BF16)
            o_ref[rows, :] = x + gate * _dot(gs_ref[rows, :], wout_s[...])


def _gmlp_call(x, mod_all, layer, mod_row, w, o):
    n_tok = x.shape[0]
    tok = lambda t: jnp.maximum(t - W_STEPS, 0)
    wstep = lambda t: jnp.minimum(t, W_STEPS - 1)
    assert C_GROUPS == W_STEPS
    return pl.pallas_call(
        functools.partial(_gmlp_kernel, mod_row=mod_row),
        grid=(W_STEPS + n_tok // GM_TM,),
        in_specs=[
            pl.BlockSpec((GM_TM, D_MODEL), lambda t: (tok(t), 0)),
            _mod_spec(layer),
            _const_spec((1, D_MODEL)),
            pl.BlockSpec((1, GM_WCH, 2 * C_WIDTH), lambda t: (o, wstep(t), 0)),
            _const_spec((1, C_WIDTH)),
            pl.BlockSpec((1, 1, CHUNK, CHUNK), lambda t: (o, wstep(t), 0, 0)),
            _const_spec((C_GROUPS, CHUNK, C_CH)),
            pl.BlockSpec((1, GM_WCH, D_MODEL), lambda t: (o, wstep(t), 0)),
        ],
        out_specs=pl.BlockSpec((GM_TM, D_MODEL), lambda t: (tok(t), 0)),
        out_shape=jax.ShapeDtypeStruct((n_tok, D_MODEL), F32),
        scratch_shapes=[pltpu.VMEM((D_MODEL, 2 * C_WIDTH), BF16), pltpu.VMEM((C_WIDTH, D_MODEL), BF16),
                        pltpu.VMEM((C_GROUPS, CHUNK, CHUNK), BF16), pltpu.VMEM((GM_TM, C_WIDTH), BF16)],
        compiler_params=_params(1),
        name="gmlp",
    )(x, mod_all, w["norm_mix_g"], w["w_in"], w["vnorm_g"], w["w_s"], w["b_s"], w["w_out"])


def _swap_halves(t):
    return jnp.concatenate([t[..., HALF:], t[..., :HALF]], axis=-1)


def _head_ones():
    lane = np.arange(2 * LANES)
    m = (lane[:, None] // LANES == lane[None, :] // LANES) & (lane[:, None] % LANES < QK_DIM)
    return jnp.asarray(m.astype(np.float32), dtype=BF16)


def _layer0_weights(e, norm_mix_g, ab_w_in, pool_w, pool_scale, q_norm_g, kv_norm_g, w_uq, w_ukv,
                    qn_g, kn_g, ab_w_out):
    w_in = ab_w_in[e]
    o3 = POOL_WIDTH + Q_RANK + KV_RANK
    kr_cols = jnp.concatenate(
        [jnp.zeros((D_MODEL, QK_NOPE), F32), w_in[:, o3:], w_in[:, o3:]], axis=1)
    w_in_p = jnp.concatenate([w_in[:, :o3], kr_cols], axis=1).astype(BF16)
    uq = w_uq[e].reshape(Q_RANK, MLA_HEADS, QK_DIM)
    uq = jnp.concatenate([uq, _swap_halves(uq[:, :, QK_NOPE:])], axis=-1)
    score_scale = QK_DIM ** -0.5 * math.log2(math.e)
    gq = qn_g[e] * score_scale
    gk = kn_g[e]
    ukv = w_ukv[e].reshape(KV_RANK, MLA_HEADS, QK_NOPE + V_DIM)
    w_k = jnp.pad(ukv[:, :, :QK_NOPE], ((0, 0), (0, 0), (0, LANES - QK_NOPE)))
    w_v = ukv[:, :, QK_NOPE:].reshape(KV_RANK, MLA_HEADS // 2, 2, V_DIM)
    zero = jnp.zeros_like(w_v[:, :, 0])
    w_v = jnp.stack([jnp.concatenate([w_v[:, :, 0], zero], axis=-1),
                     jnp.concatenate([zero, w_v[:, :, 1]], axis=-1)], axis=2)
    return {
        "norm_mix_g": norm_mix_g.reshape(1, D_MODEL),
        "w_in": w_in_p,
        "q_norm_g": q_norm_g[e].reshape(1, Q_RANK),
        "kv_norm_g": kv_norm_g[e].reshape(1, KV_RANK),
        "w_uq": uq.reshape(Q_RANK, HEAD_W).astype(BF16),
        "w_k": w_k.reshape(KV_RANK, HEAD_W).astype(BF16),
        "w_vt": w_v.reshape(KV_RANK, HEAD_W).T.astype(BF16),
        "qn_g": jnp.pad(gq, (0, LANES - QK_DIM)).reshape(1, LANES),
        "qn_g_rope": jnp.concatenate([gq, _swap_halves(gq[QK_NOPE:])]).reshape(1, LANES),
        "kn_g": jnp.concatenate([gk, gk[QK_NOPE:]]).reshape(1, LANES),
        "head_ones": _head_ones(),
        "pool_w": pool_w[e].astype(BF16),
        "pool_scale": pool_scale[e].reshape(1, POOL_WIDTH),
        "w_out_pool": ab_w_out[e, :POOL_WIDTH].astype(BF16),
        "w_out_att": ab_w_out[e, POOL_WIDTH:].astype(BF16),
    }


def _rope_tables(seq_len):
    rows = seq_len // GRID_W
    row = np.repeat(np.arange(rows), GRID_W).astype(np.float32)
    col = np.tile(np.arange(GRID_W), rows).astype(np.float32)
    per_axis = QK_ROPE // 2
    inv = (1.0 / (np.float32(ROPE_BASE) ** (np.arange(0, per_axis, 2, dtype=np.float32) / per_axis))
           ).astype(np.float32)
    ang = np.concatenate([row[:, None] * inv, col[:, None] * inv], axis=-1)
    cos, sin = np.cos(ang).astype(np.float32), np.sin(ang).astype(np.float32)
    ones = np.ones((seq_len, QK_NOPE), np.float32)
    z_nope = np.zeros((seq_len, QK_NOPE), np.float32)
    z_half = np.zeros((seq_len, HALF), np.float32)
    c_tab = np.concatenate([ones, cos, cos, cos, cos], axis=-1)
    a_tab = np.concatenate([z_nope, -sin, z_half, -sin, z_half], axis=-1)
    b_tab = np.concatenate([z_nope, z_half, sin, z_half, sin], axis=-1)
    q_tab = np.concatenate([ones, cos, cos, -sin, sin], axis=-1)
    return jnp.asarray(np.stack([c_tab, a_tab, b_tab, q_tab]))


def kernel(x_prompt, x_sample, cache_ckv, cache_krope, c, c_ctx, ada_w, ada_b, norm_mix_g, norm_ffn_g, ffn_wg, ffn_wu, ffn_wd, ab_w_in, pool_w, pool_scale, q_norm_g, kv_norm_g, w_uq, w_ukv, qn_g, kn_g, ab_w_out, gm_w_in, gm_vnorm_g, gm_ws, gm_bs, gm_w_out):
    n_ctx_b, ctx_len, _ = x_prompt.shape
    n_lat_b, lat_len, _ = x_sample.shape
    past = cache_ckv.shape[2]
    assert ctx_len & (ctx_len - 1) == 0 and lat_len & (lat_len - 1) == 0
    assert MOD_ROWS >= 1 + n_lat_b
    assert DEPTH == 2
    n_ctx, n_lat = n_ctx_b * ctx_len, n_lat_b * lat_len

    ctx = x_prompt.reshape(n_ctx, D_MODEL)
    lat = x_sample.reshape(n_lat, D_MODEL)

    cond = jnp.concatenate(
        [c_ctx[None, :], c, jnp.zeros((MOD_ROWS - 1 - n_lat_b, D_MODEL), F32)], axis=0)
    mod_all = _ada_call(cond, ada_w, ada_b)

    def all_row(tm):
        n_ctx_tiles, per = n_ctx // tm, lat_len // tm
        return lambda j: jnp.where(j < n_ctx_tiles, 0, 1 + (j - n_ctx_tiles) // per)

    w = _layer0_weights(0, norm_mix_g[0], ab_w_in, pool_w, pool_scale, q_norm_g, kv_norm_g,
                        w_uq, w_ukv, qn_g, kn_g, ab_w_out)
    cs = _rope_tables(lat_len)
    yp_c, q_c, k_c, v_c, ckv_c, kr_c = _l0a_call(ctx, mod_all, 0, lambda t: 0, 1024, ctx_len, w, None)
    yp_l, q_l, k_l, v_l, _, _ = _l0a_call(lat, mod_all, 0, lambda t: 1 + t, lat_len, lat_len, w, cs)
    kr_c2 = cache_krope[:, 0].reshape(n_lat_b * past, QK_ROPE)
    kr_pad = jnp.concatenate([jnp.zeros((n_lat_b * past, QK_NOPE), F32), kr_c2, kr_c2], axis=1)
    cache = _kvc_call(cache_ckv[:, 0].reshape(n_lat_b * past, KV_RANK), kr_pad, w, 1024)
    xs = _att_call((ctx, yp_c, q_c, k_c, v_c), (lat, yp_l, q_l, k_l, v_l), cache,
                   mod_all, 0, ctx_len, lat_len, w)
    xs = _ffn_call(xs, mod_all, 0, all_row(1024), 1024, norm_ffn_g[0].reshape(1, D_MODEL),
                   ffn_wg, ffn_wu, ffn_wd)

    gw = {
        "norm_mix_g": norm_mix_g[1].reshape(1, D_MODEL),
        "w_in": gm_w_in,
        "vnorm_g": gm_vnorm_g[0].reshape(1, C_WIDTH),
        "w_s": gm_ws,
        "b_s": jnp.broadcast_to(gm_bs[0][:, :, None], (C_GROUPS, CHUNK, C_CH)),
        "w_out": gm_w_out,
    }
    xs = _gmlp_call(xs, mod_all, 1, all_row(GM_TM), gw, 0)
    y_ctx, y_lat = _ffn_call(xs, mod_all, 1, all_row(512), 512, norm_ffn_g[1].reshape(1, D_MODEL),
                             ffn_wg, ffn_wu, ffn_wd, n_first=n_ctx)

    state_ckv = ckv_c.reshape(n_ctx_b, 1, ctx_len, KV_RANK)
    state_krope = kr_c.reshape(n_ctx_b, 1, ctx_len, QK_ROPE)
    return (y_ctx.reshape(n_ctx_b, ctx_len, D_MODEL), y_lat.reshape(n_lat_b, lat_len, D_MODEL),
            state_ckv, state_krope)
```

```python
import functools
import math

import numpy as np
import jax
import jax.numpy as jnp
from jax import lax
from jax.experimental import pallas as pl
from jax.experimental.pallas import tpu as pltpu

D_MODEL = 1024
DEPTH = 2
GRID_W = 64
POOL_WINDOWS = (2, 4, 8, 16)
POOL_GROUPS = 4
POOL_CH = 128
POOL_WIDTH = POOL_GROUPS * POOL_CH
MLA_HEADS = 8
QK_NOPE = 64
QK_ROPE = 32
QK_DIM = QK_NOPE + QK_ROPE
HALF = QK_ROPE // 2
SQRT_QK = math.sqrt(QK_DIM)
V_DIM = 64
Q_RANK = 384
KV_RANK = 256
MLA_WIDTH = MLA_HEADS * V_DIM
CHUNK = 128
C_GROUPS = 8
C_WIDTH = D_MODEL
C_CH = C_WIDTH // C_GROUPS
D_FF = 2816
ROPE_BASE = 10000.0
EPS = 1e-6

LANES = 128
HEAD_W = MLA_HEADS * LANES
AB_IN_PAD = POOL_WIDTH + Q_RANK + KV_RANK + LANES
MOD_ROWS = 16
VMEM_LIMIT = 58 * 1024 * 1024
W_STEPS = 8
Q_SS_MXU_PAIRS = 2

F32 = jnp.float32
BF16 = jnp.bfloat16


def _rms(x):
    return x * lax.rsqrt(jnp.mean(x * x, axis=-1, keepdims=True) + EPS)


def _dot(a, b):
    return jnp.dot(a, b, preferred_element_type=F32)


def _dot_nt(a, b):
    return lax.dot_general(a, b, (((1,), (1,)), ((), ())), preferred_element_type=F32)


def _dot_tn(a, b):
    return lax.dot_general(a, b, (((0,), (0,)), ((), ())), preferred_element_type=F32)


def _silu(x):
    return x * jax.nn.sigmoid(x)


def _params(n_axes):
    return pltpu.CompilerParams(
        dimension_semantics=("arbitrary",) * n_axes, vmem_limit_bytes=VMEM_LIMIT)


def _const_spec(shape):
    zeros = (0,) * len(shape)
    return pl.BlockSpec(shape, lambda *_: zeros)


def _mod_spec(layer):
    return pl.BlockSpec((1, 6, MOD_ROWS, D_MODEL), lambda *_: (layer, 0, 0, 0))


def _mod(mod_ref, term, row):
    return mod_ref[0, term, pl.ds(row, 1), :]


ADA_TERMS = 2


def _ada_kernel(cond_ref, w_ref, b_ref, o_ref):
    s = _silu(cond_ref[...]).astype(BF16)
    bias = b_ref[pl.ds(pl.program_id(0), 1), :]
    y = _dot(s, w_ref[0].astype(BF16)) + bias
    for k in range(ADA_TERMS):
        o_ref[0, k] = y[:, k * D_MODEL:(k + 1) * D_MODEL]


def _ada_call(cond, ada_w, ada_b):
    width = ADA_TERMS * D_MODEL
    return pl.pallas_call(
        _ada_kernel,
        grid=(DEPTH, 6 // ADA_TERMS),
        in_specs=[
            pl.BlockSpec((MOD_ROWS, D_MODEL), lambda i, j: (0, 0)),
            pl.BlockSpec((1, D_MODEL, width), lambda i, j: (i, 0, j)),
            pl.BlockSpec((DEPTH, width), lambda i, j: (0, j)),
        ],
        out_specs=pl.BlockSpec((1, ADA_TERMS, MOD_ROWS, D_MODEL), lambda i, j: (i, j, 0, 0)),
        out_shape=jax.ShapeDtypeStruct((DEPTH, 6, MOD_ROWS, D_MODEL), F32),
        compiler_params=_params(2),
        name="ada_mod",
    )(cond, ada_w, ada_b)


def _rope(t, cs_ref):
    return (t * cs_ref[0]
            + pltpu.roll(t, LANES - HALF, axis=1) * cs_ref[1]
            + pltpu.roll(t, HALF, axis=1) * cs_ref[2])


def _kv_heads(ckv_bf, kr, wk_ref, wv_ref, kng_ref, cs_ref, k_ref, v_ref):
    v_ref[...] = _dot_nt(wv_ref[...], ckv_bf).astype(BF16)
    kpre = _dot(ckv_bf, wk_ref[...])
    kg = kng_ref[...] * SQRT_QK
    krg = kr * kg
    if cs_ref is not None:
        krg = _rope(krg, cs_ref)
    lane = lax.broadcasted_iota(jnp.int32, kr.shape, 1)
    kr_ss = jnp.sum(jnp.where(lane < QK_DIM, kr * kr, 0.0), axis=-1, keepdims=True) + QK_DIM * EPS
    for h in range(MLA_HEADS):
        kh = kpre[:, h * LANES:(h + 1) * LANES]
        r = lax.rsqrt(jnp.sum(kh * kh, axis=-1, keepdims=True) + kr_ss)
        k_ref[:, h * LANES:(h + 1) * LANES] = ((kh * kg + krg) * r).astype(BF16)


def _shift_rows(a, k, pos, seq_len):
    n = a.shape[0]
    r = pltpu.roll(a, k % n, axis=0)
    src = pos - k
    ok = (src >= 0) if k > 0 else (src < seq_len)
    return jnp.where(ok, r, 0.0)


def _pool_group(ug, half, pos, seq_len):
    fw = ug
    step = 1
    while step < half:
        fw = fw + _shift_rows(fw, -step, pos, seq_len)
        step *= 2
    bk = _shift_rows(ug, 1, pos, seq_len)
    step = 1
    while step < half:
        bk = bk + _shift_rows(bk, step, pos, seq_len)
        step *= 2
    cnt = jnp.minimum(pos + half, seq_len) - jnp.maximum(pos - half, 0)
    return (fw + bk) / cnt.astype(F32) - ug


def _l0a_kernel(*refs, seq_len, use_rope, mod_row):
    (x_ref, mod_ref, ng_ref, win_ref, qg_ref, kvg_ref, wuq_ref, wk_ref, wv_ref,
     qng_ref, kng_ref, pw_ref, ps_ref, ones_ref) = refs[:14]
    if use_rope:
        cs_ref = refs[14]
        outs = refs[15:]
    else:
        cs_ref = None
        outs = refs[14:]
    yp_ref, q_ref, k_ref, v_ref, ckv_ref, kr_ref = outs

    tm = x_ref.shape[0]
    row = mod_row(pl.program_id(0))
    scale = ng_ref[...] * (1.0 + _mod(mod_ref, 1, row))
    h = _rms(x_ref[...]) * scale + _mod(mod_ref, 0, row)
    z = _dot(h.astype(BF16), win_ref[...])
    o1, o2, o3 = POOL_WIDTH, POOL_WIDTH + Q_RANK, POOL_WIDTH + Q_RANK + KV_RANK

    ckv = _rms(z[:, o2:o3]) * kvg_ref[...]
    kr = z[:, o3:]
    ckv_ref[...] = ckv
    kr_ref[...] = kr[:, QK_NOPE:QK_DIM]
    _kv_heads(ckv.astype(BF16), kr, wk_ref, wv_ref, kng_ref, cs_ref, k_ref, v_ref)

    cqn = (_rms(z[:, o1:o2]) * qg_ref[...]).astype(BF16)
    qf = _dot(cqn, wuq_ref[...])
    qt = qng_ref[...] * SQRT_QK
    if use_rope:
        qt = qt * cs_ref[3]
    lane2 = lax.broadcasted_iota(jnp.int32, (tm, 2 * LANES), 1)
    for j in range(MLA_HEADS // 2):
        qp = qf[:, 2 * j * LANES:(2 * j + 2) * LANES]
        sq = qp * qp
        if j < Q_SS_MXU_PAIRS:
            hi = sq.astype(BF16)
            lo = (sq - hi.astype(F32)).astype(BF16)
            ss = _dot(hi, ones_ref[...]) + _dot(lo, ones_ref[...])
            qn = qp * lax.rsqrt(ss + QK_DIM * EPS)
            parts = [qn[:, :LANES], qn[:, LANES:]]
        else:
            sq = jnp.where((lane2 & (LANES - 1)) < QK_DIM, sq, 0.0)
            parts = []
            for hh in range(2):
                ss = jnp.sum(sq[:, hh * LANES:(hh + 1) * LANES], axis=-1, keepdims=True)
                parts.append(qp[:, hh * LANES:(hh + 1) * LANES] * lax.rsqrt(ss + QK_DIM * EPS))
        for hh in range(2):
            hd = 2 * j + hh
            q_ref[:, hd * LANES:(hd + 1) * LANES] = (parts[hh] * qt).astype(BF16)

    pos = lax.broadcasted_iota(jnp.int32, (tm, POOL_CH), 0) & (seq_len - 1)
    for g, w in enumerate(POOL_WINDOWS):
        ug = z[:, g * POOL_CH:(g + 1) * POOL_CH]
        p = _pool_group(ug, w // 2, pos, seq_len)
        y = _dot(p.astype(BF16), pw_ref[g]) * ps_ref[:, g * POOL_CH:(g + 1) * POOL_CH]
        yp_ref[:, g * POOL_CH:(g + 1) * POOL_CH] = y.astype(BF16)


def _l0a_call(x, mod_all, layer, mod_row, tm, seq_len, w, cs):
    n_tok = x.shape[0]
    use_rope = cs is not None
    row_spec = lambda width: pl.BlockSpec((tm, width), lambda t: (t, 0))
    in_specs = [
        row_spec(D_MODEL),
        _mod_spec(layer),
        _const_spec((1, D_MODEL)),
        _const_spec((D_MODEL, AB_IN_PAD)),
        _const_spec((1, Q_RANK)),
        _const_spec((1, KV_RANK)),
        _const_spec((Q_RANK, HEAD_W)),
        _const_spec((KV_RANK, HEAD_W)),
        _const_spec((HEAD_W, KV_RANK)),
        _const_spec((1, LANES)),
        _const_spec((1, LANES)),
        _const_spec((POOL_GROUPS, POOL_CH, POOL_CH)),
        _const_spec((1, POOL_WIDTH)),
        _const_spec((2 * LANES, 2 * LANES)),
    ]
    args = [x, mod_all, w["norm_mix_g"], w["w_in"], w["q_norm_g"], w["kv_norm_g"], w["w_uq"],
            w["w_k"], w["w_vt"], w["qn_g_rope"] if use_rope else w["qn_g"], w["kn_g"],
            w["pool_w"], w["pool_scale"], w["head_ones"]]
    if use_rope:
        assert tm == seq_len
        in_specs.append(_const_spec((4, seq_len, LANES)))
        args.append(cs)
    out_shape = [
        jax.ShapeDtypeStruct((n_tok, POOL_WIDTH), BF16),
        jax.ShapeDtypeStruct((n_tok, HEAD_W), BF16),
        jax.ShapeDtypeStruct((n_tok, HEAD_W), BF16),
        jax.ShapeDtypeStruct((HEAD_W, n_tok), BF16),
        jax.ShapeDtypeStruct((n_tok, KV_RANK), F32),
        jax.ShapeDtypeStruct((n_tok, QK_ROPE), F32),
    ]
    out_specs = [row_spec(POOL_WIDTH), row_spec(HEAD_W), row_spec(HEAD_W),
                 pl.BlockSpec((HEAD_W, tm), lambda t: (0, t)),
                 row_spec(KV_RANK), row_spec(QK_ROPE)]
    return pl.pallas_call(
        functools.partial(_l0a_kernel, seq_len=seq_len, use_rope=use_rope, mod_row=mod_row),
        grid=(n_tok // tm,),
        in_specs=in_specs,
        out_specs=out_specs,
        out_shape=out_shape,
        compiler_params=_params(1),
        name="l0_front_rope" if use_rope else "l0_front",
    )(*args)


def _kvc_kernel(ckv_ref, kr_ref, wk_ref, wv_ref, kng_ref, k_ref, v_ref):
    _kv_heads(ckv_ref[...].astype(BF16), kr_ref[...], wk_ref, wv_ref, kng_ref, None, k_ref, v_ref)


def _kvc_call(ckv, kr_pad, w, tm):
    n_tok = ckv.shape[0]
    row_spec = lambda width: pl.BlockSpec((tm, width), lambda t: (t, 0))
    return pl.pallas_call(
        _kvc_kernel,
        grid=(n_tok // tm,),
        in_specs=[row_spec(KV_RANK), row_spec(LANES), _const_spec((KV_RANK, HEAD_W)),
                  _const_spec((HEAD_W, KV_RANK)), _const_spec((1, LANES))],
        out_specs=[row_spec(HEAD_W), pl.BlockSpec((HEAD_W, tm), lambda t: (0, t))],
        out_shape=[jax.ShapeDtypeStruct((n_tok, HEAD_W), BF16),
                   jax.ShapeDtypeStruct((HEAD_W, n_tok), BF16)],
        compiler_params=_params(1),
        name="cache_kv",
    )(ckv, kr_pad, w["w_k"], w["w_vt"], w["kn_g"])


ATT_TQ = 256
ATT_KC = 512
ATT_AHEAD = 6


def _att_kernel(xc_ref, ypc_ref, qc_ref, kc_ref, vc_ref,
                xl_ref, ypl_ref, ql_ref, kl_ref, vl_ref, kp_ref, vp_ref,
                mod_ref, wop_ref, woa_ref, o_ref, ya_ref, *, n_ctx_steps, n_q):
    t = pl.program_id(0)

    @pl.when(t < n_ctx_steps)
    def _():
        _att_tile(xc_ref, ypc_ref, qc_ref, kc_ref, vc_ref, None, None,
                  mod_ref, 0, wop_ref, woa_ref, o_ref, ya_ref)

    @pl.when(t >= n_ctx_steps)
    def _():
        _att_tile(xl_ref, ypl_ref, ql_ref, kl_ref, vl_ref, kp_ref, vp_ref,
                  mod_ref, 1 + (t - n_ctx_steps) // n_q, wop_ref, woa_ref, o_ref, ya_ref)


def _att_tile(x_ref, yp_ref, q_ref, k_ref, v_ref, kc_ref, vc_ref,
              mod_ref, row, wop_ref, woa_ref, o_ref, ya_ref):
    has_cache = kc_ref is not None
    sources = [(k_ref, v_ref)] if not has_cache else [(kc_ref, vc_ref), (k_ref, v_ref)]
    chunks = []
    for ks_ref, vs_ref in sources:
        n_keys = ks_ref.shape[0]
        kc = min(ATT_KC, n_keys)
        assert n_keys % kc == 0
        chunks += [(ks_ref, vs_ref, slice(c * kc, (c + 1) * kc)) for c in range(n_keys // kc)]
    items = [(hd, ch) for hd in range(MLA_HEADS) for ch in chunks]

    def scores(item):
        hd, (ks_ref, _, keys) = item
        sl = slice(hd * LANES, (hd + 1) * LANES)
        s = _dot_nt(ks_ref[keys, sl], q_ref[:, sl])
        return s, jnp.max(s, axis=0, keepdims=True)

    pending = [scores(it) for it in items[:ATT_AHEAD]]
    state = {}
    for n, (hd, (_, vs_ref, keys)) in enumerate(items):
        if n + ATT_AHEAD < len(items):
            pending.append(scores(items[n + ATT_AHEAD]))
        s, cmax = pending.pop(0)
        sl = slice(hd * LANES, (hd + 1) * LANES)
        if hd not in state:
            mx = cmax
            p = jnp.exp2(s - mx)
            den = jnp.sum(p, axis=0, keepdims=True)
            acc = _dot(vs_ref[sl, keys], p.astype(BF16))
        else:
            mx, den, acc = state[hd]
            new = jnp.maximum(mx, cmax)
            alpha = jnp.exp2(mx - new)
            p = jnp.exp2(s - new)
            den = alpha * den + jnp.sum(p, axis=0, keepdims=True)
            acc = alpha * acc + _dot(vs_ref[sl, keys], p.astype(BF16))
            mx = new
        state[hd] = (mx, den, acc)
        if hd % 2 == 1 and keys.stop == vs_ref.shape[1] and vs_ref is v_ref:
            j = hd // 2
            pair = state[hd - 1][2] / state[hd - 1][1] + acc / den
            ya_ref[j * LANES:(j + 1) * LANES, :] = pair.astype(BF16)

    y = _dot(yp_ref[...], wop_ref[...]) + _dot_tn(ya_ref[...], woa_ref[...])
    o_ref[...] = x_ref[...] + _mod(mod_ref, 2, row) * y


def _att_call(ctx_in, lat_in, cache, mod_all, layer, ctx_len, lat_len, w):
    assert ctx_len == ATT_TQ
    n_ctx, n_lat = ctx_in[0].shape[0], lat_in[0].shape[0]
    n_c = n_ctx // ATT_TQ
    n_q = lat_len // ATT_TQ
    past = cache[0].shape[0] // (n_lat // lat_len)
    cstep = lambda t: jnp.minimum(t, n_c - 1)
    lstep = lambda t: jnp.maximum(t - n_c, 0)
    crow = lambda width: pl.BlockSpec((ATT_TQ, width), lambda t: (cstep(t), 0))
    lrow = lambda width: pl.BlockSpec((ATT_TQ, width), lambda t: (lstep(t), 0))
    lbat = lambda rows: pl.BlockSpec((rows, HEAD_W), lambda t: (lstep(t) // n_q, 0))
    lbat_t = lambda cols: pl.BlockSpec((HEAD_W, cols), lambda t: (0, lstep(t) // n_q))
    in_specs = [
        crow(D_MODEL), crow(POOL_WIDTH), crow(HEAD_W), crow(HEAD_W),
        pl.BlockSpec((HEAD_W, ATT_TQ), lambda t: (0, cstep(t))),
        lrow(D_MODEL), lrow(POOL_WIDTH), lrow(HEAD_W), lbat(lat_len), lbat_t(lat_len),
        lbat(past), lbat_t(past),
        _mod_spec(layer), _const_spec((POOL_WIDTH, D_MODEL)), _const_spec((MLA_WIDTH, D_MODEL)),
    ]
    return pl.pallas_call(
        functools.partial(_att_kernel, n_ctx_steps=n_c, n_q=n_q),
        grid=(n_c + n_lat // ATT_TQ,),
        in_specs=in_specs,
        out_specs=pl.BlockSpec((ATT_TQ, D_MODEL), lambda t: (t, 0)),
        out_shape=jax.ShapeDtypeStruct((n_ctx + n_lat, D_MODEL), F32),
        scratch_shapes=[pltpu.VMEM((MLA_WIDTH, ATT_TQ), BF16)],
        compiler_params=_params(1),
        name="att_out",
    )(*ctx_in, *lat_in, *cache, mod_all, w["w_out_pool"], w["w_out_att"])


FFN_SUB = 256
FFN_GCH = D_MODEL // W_STEPS
FFN_DCH = D_FF // W_STEPS


def _ffn_kernel(x_ref, mod_ref, ng_ref, wg_ref, wu_ref, wd_ref, *rest, mod_row, split):
    outs, (wg_s, wu_s, wd_s) = rest[:-3], rest[-3:]
    t = pl.program_id(0)

    @pl.when(t < W_STEPS)
    def _():
        r = pl.multiple_of(t * FFN_GCH, FFN_GCH)
        wg_s[pl.ds(r, FFN_GCH), :] = wg_ref[0].astype(BF16)
        wu_s[pl.ds(r, FFN_GCH), :] = wu_ref[0].astype(BF16)
        r = pl.multiple_of(t * FFN_DCH, FFN_DCH)
        wd_s[pl.ds(r, FFN_DCH), :] = wd_ref[0].astype(BF16)

    def tile(o_ref):
        row = mod_row(t - W_STEPS)
        scale = ng_ref[...] * (1.0 + _mod(mod_ref, 4, row))
        shift = _mod(mod_ref, 3, row)
        gate = _mod(mod_ref, 5, row)
        for s in range(x_ref.shape[0] // FFN_SUB):
            rows = slice(s * FFN_SUB, (s + 1) * FFN_SUB)
            x = x_ref[rows, :]
            h = (_rms(x) * scale + shift).astype(BF16)
            g = _dot(h, wg_s[...])
            u = _dot(h, wu_s[...])
            d = _dot((_silu(g) * u).astype(BF16), wd_s[...])
            o_ref[rows, :] = x + gate * d

    if split is None:
        pl.when(t >= W_STEPS)(lambda: tile(outs[0]))
    else:
        pl.when((t >= W_STEPS) & (t < W_STEPS + split))(lambda: tile(outs[0]))
        pl.when(t >= W_STEPS + split)(lambda: tile(outs[1]))


def _ffn_call(x, mod_all, layer, mod_row, tm, ng, wg, wu, wd, n_first=None):
    n_tok = x.shape[0]
    n_tiles = n_tok // tm
    tok = lambda t: jnp.maximum(t - W_STEPS, 0)
    wstep = lambda t: jnp.minimum(t, W_STEPS - 1)
    if n_first is None:
        split = None
        out_specs = pl.BlockSpec((tm, D_MODEL), lambda t: (tok(t), 0))
        out_shape = jax.ShapeDtypeStruct((n_tok, D_MODEL), F32)
    else:
        split = n_first // tm
        out_specs = [
            pl.BlockSpec((tm, D_MODEL), lambda t: (jnp.minimum(tok(t), split - 1), 0)),
            pl.BlockSpec((tm, D_MODEL), lambda t: (jnp.maximum(tok(t) - split, 0), 0)),
        ]
        out_shape = [jax.ShapeDtypeStruct((n_first, D_MODEL), F32),
                     jax.ShapeDtypeStruct((n_tok - n_first, D_MODEL), F32)]
    return pl.pallas_call(
        functools.partial(_ffn_kernel, mod_row=mod_row, split=split),
        grid=(W_STEPS + n_tiles,),
        in_specs=[
            pl.BlockSpec((tm, D_MODEL), lambda t: (tok(t), 0)),
            _mod_spec(layer),
            _const_spec((1, D_MODEL)),
            pl.BlockSpec((1, FFN_GCH, D_FF), lambda t: (layer, wstep(t), 0)),
            pl.BlockSpec((1, FFN_GCH, D_FF), lambda t: (layer, wstep(t), 0)),
            pl.BlockSpec((1, FFN_DCH, D_MODEL), lambda t: (layer, wstep(t), 0)),
        ],
        out_specs=out_specs,
        out_shape=out_shape,
        scratch_shapes=[pltpu.VMEM((D_MODEL, D_FF), BF16), pltpu.VMEM((D_MODEL, D_FF), BF16),
                        pltpu.VMEM((D_FF, D_MODEL), BF16)],
        compiler_params=_params(1),
        name="ffn",
    )(x, mod_all, ng, wg, wu, wd)


GM_TM = 1024
GM_SUB = 256
GM_WCH = D_MODEL // W_STEPS


def _gmlp_kernel(x_ref, mod_ref, ng_ref, win_ref, vg_ref, ws_ref, bs_ref, wout_ref, o_ref,
                 win_s, wout_s, ws_s, gs_ref, *, mod_row):
    t = pl.program_id(0)

    @pl.when(t < W_STEPS)
    def _():
        r = pl.multiple_of(t * GM_WCH, GM_WCH)
        win_s[pl.ds(r, GM_WCH), :] = win_ref[0].astype(BF16)
        wout_s[pl.ds(r, GM_WCH), :] = wout_ref[0].astype(BF16)
        ws_s[t] = ws_ref[0, 0].astype(BF16)

    @pl.when(t >= W_STEPS)
    def _():
        row = mod_row(t - W_STEPS)
        scale = ng_ref[...] * (1.0 + _mod(mod_ref, 1, row))
        shift = _mod(mod_ref, 0, row)
        gate = _mod(mod_ref, 2, row)
        n_chunks = GM_SUB // CHUNK
        for s in range(x_ref.shape[0] // GM_SUB):
            rows = slice(s * GM_SUB, (s + 1) * GM_SUB)
            x = x_ref[rows, :]
            h = (_rms(x) * scale + shift).astype(BF16)
            z = _dot(h, win_s[...])
            vn = (_rms(z[:, C_WIDTH:]) * vg_ref[...]).astype(BF16)
            for g in range(C_GROUPS):
                cols = slice(g * C_CH, (g + 1) * C_CH)
                rhs = jnp.concatenate(
                    [vn[n * CHUNK:(n + 1) * CHUNK, cols] for n in range(n_chunks)], axis=1)
                sp = _dot(ws_s[g], rhs)
                for n in range(n_chunks):
                    crow = slice(n * CHUNK, (n + 1) * CHUNK)
                    grow = slice(s * GM_SUB + n * CHUNK, s * GM_SUB + (n + 1) * CHUNK)
                    sn = sp[:, n * C_CH:(n + 1) * C_CH] + bs_ref[g]
                    gs_ref[grow, cols] = (z[crow, cols] * sn).astype(BF16)
            o_ref[rows, :] = x + gate * _dot(gs_ref[rows, :], wout_s[...])


def _gmlp_call(x, mod_all, layer, mod_row, w, o):
    n_tok = x.shape[0]
    tok = lambda t: jnp.maximum(t - W_STEPS, 0)
    wstep = lambda t: jnp.minimum(t, W_STEPS - 1)
    assert C_GROUPS == W_STEPS
    return pl.pallas_call(
        functools.partial(_gmlp_kernel, mod_row=mod_row),
        grid=(W_STEPS + n_tok // GM_TM,),
        in_specs=[
            pl.BlockSpec((GM_TM, D_MODEL), lambda t: (tok(t), 0)),
            _mod_spec(layer),
            _const_spec((1, D_MODEL)),
            pl.BlockSpec((1, GM_WCH, 2 * C_WIDTH), lambda t: (o, wstep(t), 0)),
            _const_spec((1, C_WIDTH)),
            pl.BlockSpec((1, 1, CHUNK, CHUNK), lambda t: (o, wstep(t), 0, 0)),
            _const_spec((C_GROUPS, CHUNK, C_CH)),
            pl.BlockSpec((1, GM_WCH, D_MODEL), lambda t: (o, wstep(t), 0)),
        ],
        out_specs=pl.BlockSpec((GM_TM, D_MODEL), lambda t: (tok(t), 0)),
        out_shape=jax.ShapeDtypeStruct((n_tok, D_MODEL), F32),
        scratch_shapes=[pltpu.VMEM((D_MODEL, 2 * C_WIDTH), BF16), pltpu.VMEM((C_WIDTH, D_MODEL), BF16),
                        pltpu.VMEM((C_GROUPS, CHUNK, CHUNK), BF16), pltpu.VMEM((GM_TM, C_WIDTH), BF16)],
        compiler_params=_params(1),
        name="gmlp",
    )(x, mod_all, w["norm_mix_g"], w["w_in"], w["vnorm_g"], w["w_s"], w["b_s"], w["w_out"])


def _swap_halves(t):
    return jnp.concatenate([t[..., HALF:], t[..., :HALF]], axis=-1)


def _head_ones():
    lane = np.arange(2 * LANES)
    m = (lane[:, None] // LANES == lane[None, :] // LANES) & (lane[:, None] % LANES < QK_DIM)
    return jnp.asarray(m.astype(np.float32), dtype=BF16)


def _layer0_weights(e, norm_mix_g, ab_w_in, pool_w, pool_scale, q_norm_g, kv_norm_g, w_uq, w_ukv,
                    qn_g, kn_g, ab_w_out):
    w_in = ab_w_in[e]
    o3 = POOL_WIDTH + Q_RANK + KV_RANK
    kr_cols = jnp.concatenate(
        [jnp.zeros((D_MODEL, QK_NOPE), F32), w_in[:, o3:], w_in[:, o3:]], axis=1)
    w_in_p = jnp.concatenate([w_in[:, :o3], kr_cols], axis=1).astype(BF16)
    uq = w_uq[e].reshape(Q_RANK, MLA_HEADS, QK_DIM)
    uq = jnp.concatenate([uq, _swap_halves(uq[:, :, QK_NOPE:])], axis=-1)
    score_scale = QK_DIM ** -0.5 * math.log2(math.e)
    gq = qn_g[e] * score_scale
    gk = kn_g[e]
    ukv = w_ukv[e].reshape(KV_RANK, MLA_HEADS, QK_NOPE + V_DIM)
    w_k = jnp.pad(ukv[:, :, :QK_NOPE], ((0, 0), (0, 0), (0, LANES - QK_NOPE)))
    w_v = ukv[:, :, QK_NOPE:].reshape(KV_RANK, MLA_HEADS // 2, 2, V_DIM)
    zero = jnp.zeros_like(w_v[:, :, 0])
    w_v = jnp.stack([jnp.concatenate([w_v[:, :, 0], zero], axis=-1),
                     jnp.concatenate([zero, w_v[:, :, 1]], axis=-1)], axis=2)
    return {
        "norm_mix_g": norm_mix_g.reshape(1, D_MODEL),
        "w_in": w_in_p,
        "q_norm_g": q_norm_g[e].reshape(1, Q_RANK),
        "kv_norm_g": kv_norm_g[e].reshape(1, KV_RANK),
        "w_uq": uq.reshape(Q_RANK, HEAD_W).astype(BF16),
        "w_k": w_k.reshape(KV_RANK, HEAD_W).astype(BF16),
        "w_vt": w_v.reshape(KV_RANK, HEAD_W).T.astype(BF16),
        "qn_g": jnp.pad(gq, (0, LANES - QK_DIM)).reshape(1, LANES),
        "qn_g_rope": jnp.concatenate([gq, _swap_halves(gq[QK_NOPE:])]).reshape(1, LANES),
        "kn_g": jnp.concatenate([gk, gk[QK_NOPE:]]).reshape(1, LANES),
        "head_ones": _head_ones(),
        "pool_w": pool_w[e].astype(BF16),
        "pool_scale": pool_scale[e].reshape(1, POOL_WIDTH),
        "w_out_pool": ab_w_out[e, :POOL_WIDTH].astype(BF16),
        "w_out_att": ab_w_out[e, POOL_WIDTH:].astype(BF16),
    }


def _rope_tables(seq_len):
    rows = seq_len // GRID_W
    row = np.repeat(np.arange(rows), GRID_W).astype(np.float32)
    col = np.tile(np.arange(GRID_W), rows).astype(np.float32)
    per_axis = QK_ROPE // 2
    inv = (1.0 / (np.float32(ROPE_BASE) ** (np.arange(0, per_axis, 2, dtype=np.float32) / per_axis))
           ).astype(np.float32)
    ang = np.concatenate([row[:, None] * inv, col[:, None] * inv], axis=-1)
    cos, sin = np.cos(ang).astype(np.float32), np.sin(ang).astype(np.float32)
    ones = np.ones((seq_len, QK_NOPE), np.float32)
    z_nope = np.zeros((seq_len, QK_NOPE), np.float32)
    z_half = np.zeros((seq_len, HALF), np.float32)
    c_tab = np.concatenate([ones, cos, cos, cos, cos], axis=-1)
    a_tab = np.concatenate([z_nope, -sin, z_half, -sin, z_half], axis=-1)
    b_tab = np.concatenate([z_nope, z_half, sin, z_half, sin], axis=-1)
    q_tab = np.concatenate([ones, cos, cos, -sin, sin], axis=-1)
    return jnp.asarray(np.stack([c_tab, a_tab, b_tab, q_tab]))


def kernel(x_prompt, x_sample, cache_ckv, cache_krope, c, c_ctx, ada_w, ada_b, norm_mix_g, norm_ffn_g, ffn_wg, ffn_wu, ffn_wd, ab_w_in, pool_w, pool_scale, q_norm_g, kv_norm_g, w_uq, w_ukv, qn_g, kn_g, ab_w_out, gm_w_in, gm_vnorm_g, gm_ws, gm_bs, gm_w_out):
    n_ctx_b, ctx_len, _ = x_prompt.shape
    n_lat_b, lat_len, _ = x_sample.shape
    past = cache_ckv.shape[2]
    assert ctx_len & (ctx_len - 1) == 0 and lat_len & (lat_len - 1) == 0
    assert MOD_ROWS >= 1 + n_lat_b
    assert DEPTH == 2
    n_ctx, n_lat = n_ctx_b * ctx_len, n_lat_b * lat_len

    ctx = x_prompt.reshape(n_ctx, D_MODEL)
    lat = x_sample.reshape(n_lat, D_MODEL)

    cond = jnp.concatenate(
        [c_ctx[None, :], c, jnp.zeros((MOD_ROWS - 1 - n_lat_b, D_MODEL), F32)], axis=0)
    mod_all = _ada_call(cond, ada_w, ada_b)

    def all_row(tm):
        n_ctx_tiles, per = n_ctx // tm, lat_len // tm
        return lambda j: jnp.where(j < n_ctx_tiles, 0, 1 + (j - n_ctx_tiles) // per)

    w = _layer0_weights(0, norm_mix_g[0], ab_w_in, pool_w, pool_scale, q_norm_g, kv_norm_g,
                        w_uq, w_ukv, qn_g, kn_g, ab_w_out)
    cs = _rope_tables(lat_len)
    yp_c, q_c, k_c, v_c, ckv_c, kr_c = _l0a_call(ctx, mod_all, 0, lambda t: 0, 1024, ctx_len, w, None)
    yp_l, q_l, k_l, v_l, _, _ = _l0a_call(lat, mod_all, 0, lambda t: 1 + t, lat_len, lat_len, w, cs)
    kr_c2 = cache_krope[:, 0].reshape(n_lat_b * past, QK_ROPE)
    kr_pad = jnp.concatenate([jnp.zeros((n_lat_b * past, QK_NOPE), F32), kr_c2, kr_c2], axis=1)
    cache = _kvc_call(cache_ckv[:, 0].reshape(n_lat_b * past, KV_RANK), kr_pad, w, 1024)
    xs = _att_call((ctx, yp_c, q_c, k_c, v_c), (lat, yp_l, q_l, k_l, v_l), cache,
                   mod_all, 0, ctx_len, lat_len, w)
    xs = _ffn_call(xs, mod_all, 0, all_row(1024), 1024, norm_ffn_g[0].reshape(1, D_MODEL),
                   ffn_wg, ffn_wu, ffn_wd)

    gw = {
        "norm_mix_g": norm_mix_g[1].reshape(1, D_MODEL),
        "w_in": gm_w_in,
        "vnorm_g": gm_vnorm_g[0].reshape(1, C_WIDTH),
        "w_s": gm_ws,
        "b_s": jnp.broadcast_to(gm_bs[0][:, :, None], (C_GROUPS, CHUNK, C_CH)),
        "w_out": gm_w_out,
    }
    xs = _gmlp_call(xs, mod_all, 1, all_row(GM_TM), gw, 0)
    y_ctx, y_lat = _ffn_call(xs, mod_all, 1, all_row(512), 512, norm_ffn_g[1].reshape(1, D_MODEL),
                             ffn_wg, ffn_wu, ffn_wd, n_first=n_ctx)

    state_ckv = ckv_c.reshape(n_ctx_b, 1, ctx_len, KV_RANK)
    state_krope = kr_c.reshape(n_ctx_b, 1, ctx_len, QK_ROPE)
    return (y_ctx.reshape(n_ctx_b, ctx_len, D_MODEL), y_lat.reshape(n_lat_b, lat_len, D_MODEL),
            state_ckv, state_krope)
```

```python
import functools
import math

import numpy as np
import jax
import jax.numpy as jnp
from jax import lax
from jax.experimental import pallas as pl
from jax.experimental.pallas import tpu as pltpu

D_MODEL = 1024
DEPTH = 2
GRID_W = 64
POOL_WINDOWS = (2, 4, 8, 16)
POOL_GROUPS = 4
POOL_CH = 128
POOL_WIDTH = POOL_GROUPS * POOL_CH
MLA_HEADS = 8
QK_NOPE = 64
QK_ROPE = 32
QK_DIM = QK_NOPE + QK_ROPE
HALF = QK_ROPE // 2
SQRT_QK = math.sqrt(QK_DIM)
V_DIM = 64
Q_RANK = 384
KV_RANK = 256
MLA_WIDTH = MLA_HEADS * V_DIM
CHUNK = 128
C_GROUPS = 8
C_WIDTH = D_MODEL
C_CH = C_WIDTH // C_GROUPS
D_FF = 2816
ROPE_BASE = 10000.0
EPS = 1e-6

LANES = 128
HEAD_W = MLA_HEADS * LANES
AB_IN_PAD = POOL_WIDTH + Q_RANK + KV_RANK + LANES
MOD_ROWS = 16
VMEM_LIMIT = 58 * 1024 * 1024
W_STEPS = 8
Q_SS_MXU_PAIRS = 2

F32 = jnp.float32
BF16 = jnp.bfloat16


def _rms(x):
    return x * lax.rsqrt(jnp.mean(x * x, axis=-1, keepdims=True) + EPS)


def _dot(a, b):
    return jnp.dot(a, b, preferred_element_type=F32)


def _dot_nt(a, b):
    return lax.dot_general(a, b, (((1,), (1,)), ((), ())), preferred_element_type=F32)


def _dot_tn(a, b):
    return lax.dot_general(a, b, (((0,), (0,)), ((), ())), preferred_element_type=F32)


def _silu(x):
    return x * jax.nn.sigmoid(x)


def _params(n_axes):
    return pltpu.CompilerParams(
        dimension_semantics=("arbitrary",) * n_axes, vmem_limit_bytes=VMEM_LIMIT)


def _const_spec(shape):
    zeros = (0,) * len(shape)
    return pl.BlockSpec(shape, lambda *_: zeros)


def _mod_spec(layer):
    return pl.BlockSpec((1, 6, MOD_ROWS, D_MODEL), lambda *_: (layer, 0, 0, 0))


def _mod(mod_ref, term, row):
    return mod_ref[0, term, pl.ds(row, 1), :]


ADA_TERMS = 2


def _ada_kernel(cond_ref, w_ref, b_ref, o_ref):
    s = _silu(cond_ref[...]).astype(BF16)
    bias = b_ref[pl.ds(pl.program_id(0), 1), :]
    y = _dot(s, w_ref[0].astype(BF16)) + bias
    for k in range(ADA_TERMS):
        o_ref[0, k] = y[:, k * D_MODEL:(k + 1) * D_MODEL]


def _ada_call(cond, ada_w, ada_b):
    width = ADA_TERMS * D_MODEL
    return pl.pallas_call(
        _ada_kernel,
        grid=(DEPTH, 6 // ADA_TERMS),
        in_specs=[
            pl.BlockSpec((MOD_ROWS, D_MODEL), lambda i, j: (0, 0)),
            pl.BlockSpec((1, D_MODEL, width), lambda i, j: (i, 0, j)),
            pl.BlockSpec((DEPTH, width), lambda i, j: (0, j)),
        ],
        out_specs=pl.BlockSpec((1, ADA_TERMS, MOD_ROWS, D_MODEL), lambda i, j: (i, j, 0, 0)),
        out_shape=jax.ShapeDtypeStruct((DEPTH, 6, MOD_ROWS, D_MODEL), F32),
        compiler_params=_params(2),
        name="ada_mod",
    )(cond, ada_w, ada_b)


def _rope(t, cs_ref):
    return (t * cs_ref[0]
            + pltpu.roll(t, LANES - HALF, axis=1) * cs_ref[1]
            + pltpu.roll(t, HALF, axis=1) * cs_ref[2])


def _kv_matmuls(ckv_bf, wk_ref, wv_ref, v_ref):
    v_ref[...] = _dot_nt(wv_ref[...], ckv_bf).astype(BF16)
    return _dot(ckv_bf, wk_ref[...])


def _k_heads(kpre, kr, kng_ref, cs_ref, k_ref):
    kg = kng_ref[...] * SQRT_QK
    krg = kr * kg
    if cs_ref is not None:
        krg = _rope(krg, cs_ref)
    lane = lax.broadcasted_iota(jnp.int32, kr.shape, 1)
    kr_ss = jnp.sum(jnp.where(lane < QK_DIM, kr * kr, 0.0), axis=-1, keepdims=True) + QK_DIM * EPS
    for h in range(MLA_HEADS):
        kh = kpre[:, h * LANES:(h + 1) * LANES]
        r = lax.rsqrt(jnp.sum(kh * kh, axis=-1, keepdims=True) + kr_ss)
        k_ref[:, h * LANES:(h + 1) * LANES] = ((kh * kg + krg) * r).astype(BF16)


def _shift_rows(a, k, pos, seq_len):
    n = a.shape[0]
    r = pltpu.roll(a, k % n, axis=0)
    src = pos - k
    ok = (src >= 0) if k > 0 else (src < seq_len)
    return jnp.where(ok, r, 0.0)


def _pool_group(ug, half, pos, seq_len):
    fw = ug
    step = 1
    while step < half:
        fw = fw + _shift_rows(fw, -step, pos, seq_len)
        step *= 2
    bk = _shift_rows(ug, 1, pos, seq_len)
    step = 1
    while step < half:
        bk = bk + _shift_rows(bk, step, pos, seq_len)
        step *= 2
    cnt = jnp.minimum(pos + half, seq_len) - jnp.maximum(pos - half, 0)
    return (fw + bk) / cnt.astype(F32) - ug


def _l0a_kernel(*refs, seq_len, use_rope, mod_row):
    (x_ref, mod_ref, ng_ref, win_ref, qg_ref, kvg_ref, wuq_ref, wk_ref, wv_ref,
     qng_ref, kng_ref, pw_ref, ps_ref, ones_ref) = refs[:14]
    if use_rope:
        cs_ref = refs[14]
        outs = refs[15:]
    else:
        cs_ref = None
        outs = refs[14:]
    yp_ref, q_ref, k_ref, v_ref, ckv_ref, kr_ref = outs

    tm = x_ref.shape[0]
    row = mod_row(pl.program_id(0))
    scale = ng_ref[...] * (1.0 + _mod(mod_ref, 1, row))
    h = _rms(x_ref[...]) * scale + _mod(mod_ref, 0, row)
    z = _dot(h.astype(BF16), win_ref[...])
    o1, o2, o3 = POOL_WIDTH, POOL_WIDTH + Q_RANK, POOL_WIDTH + Q_RANK + KV_RANK

    ckv = _rms(z[:, o2:o3]) * kvg_ref[...]
    kr = z[:, o3:]
    ckv_ref[...] = ckv
    kr_ref[...] = kr[:, QK_NOPE:QK_DIM]
    cqn = (_rms(z[:, o1:o2]) * qg_ref[...]).astype(BF16)
    kpre = _kv_matmuls(ckv.astype(BF16), wk_ref, wv_ref, v_ref)
    qf = _dot(cqn, wuq_ref[...])

    qt = qng_ref[...] * SQRT_QK
    if use_rope:
        qt = qt * cs_ref[3]
    lane2 = lax.broadcasted_iota(jnp.int32, (tm, 2 * LANES), 1)
    for j in range(MLA_HEADS // 2):
        qp = qf[:, 2 * j * LANES:(2 * j + 2) * LANES]
        sq = qp * qp
        if j < Q_SS_MXU_PAIRS:
            hi = sq.astype(BF16)
            lo = (sq - hi.astype(F32)).astype(BF16)
            ss = _dot(hi, ones_ref[...]) + _dot(lo, ones_ref[...])
            qn = qp * lax.rsqrt(ss + QK_DIM * EPS)
            parts = [qn[:, :LANES], qn[:, LANES:]]
        else:
            sq = jnp.where((lane2 & (LANES - 1)) < QK_DIM, sq, 0.0)
            parts = []
            for hh in range(2):
                ss = jnp.sum(sq[:, hh * LANES:(hh + 1) * LANES], axis=-1, keepdims=True)
                parts.append(qp[:, hh * LANES:(hh + 1) * LANES] * lax.rsqrt(ss + QK_DIM * EPS))
        for hh in range(2):
            hd = 2 * j + hh
            q_ref[:, hd * LANES:(hd + 1) * LANES] = (parts[hh] * qt).astype(BF16)

    _k_heads(kpre, kr, kng_ref, cs_ref, k_ref)

    pos = lax.broadcasted_iota(jnp.int32, (tm, POOL_CH), 0) & (seq_len - 1)
    for g, w in enumerate(POOL_WINDOWS):
        ug = z[:, g * POOL_CH:(g + 1) * POOL_CH]
        p = _pool_group(ug, w // 2, pos, seq_len)
        y = _dot(p.astype(BF16), pw_ref[g]) * ps_ref[:, g * POOL_CH:(g + 1) * POOL_CH]
        yp_ref[:, g * POOL_CH:(g + 1) * POOL_CH] = y.astype(BF16)


def _l0a_call(x, mod_all, layer, mod_row, tm, seq_len, w, cs):
    n_tok = x.shape[0]
    use_rope = cs is not None
    row_spec = lambda width: pl.BlockSpec((tm, width), lambda t: (t, 0))
    in_specs = [
        row_spec(D_MODEL),
        _mod_spec(layer),
        _const_spec((1, D_MODEL)),
        _const_spec((D_MODEL, AB_IN_PAD)),
        _const_spec((1, Q_RANK)),
        _const_spec((1, KV_RANK)),
        _const_spec((Q_RANK, HEAD_W)),
        _const_spec((KV_RANK, HEAD_W)),
        _const_spec((HEAD_W, KV_RANK)),
        _const_spec((1, LANES)),
        _const_spec((1, LANES)),
        _const_spec((POOL_GROUPS, POOL_CH, POOL_CH)),
        _const_spec((1, POOL_WIDTH)),
        _const_spec((2 * LANES, 2 * LANES)),
    ]
    args = [x, mod_all, w["norm_mix_g"], w["w_in"], w["q_norm_g"], w["kv_norm_g"], w["w_uq"],
            w["w_k"], w["w_vt"], w["qn_g_rope"] if use_rope else w["qn_g"], w["kn_g"],
            w["pool_w"], w["pool_scale"], w["head_ones"]]
    if use_rope:
        assert tm == seq_len
        in_specs.append(_const_spec((4, seq_len, LANES)))
        args.append(cs)
    out_shape = [
        jax.ShapeDtypeStruct((n_tok, POOL_WIDTH), BF16),
        jax.ShapeDtypeStruct((n_tok, HEAD_W), BF16),
        jax.ShapeDtypeStruct((n_tok, HEAD_W), BF16),
        jax.ShapeDtypeStruct((HEAD_W, n_tok), BF16),
        jax.ShapeDtypeStruct((n_tok, KV_RANK), F32),
        jax.ShapeDtypeStruct((n_tok, QK_ROPE), F32),
    ]
    out_specs = [row_spec(POOL_WIDTH), row_spec(HEAD_W), row_spec(HEAD_W),
                 pl.BlockSpec((HEAD_W, tm), lambda t: (0, t)),
                 row_spec(KV_RANK), row_spec(QK_ROPE)]
    return pl.pallas_call(
        functools.partial(_l0a_kernel, seq_len=seq_len, use_rope=use_rope, mod_row=mod_row),
        grid=(n_tok // tm,),
        in_specs=in_specs,
        out_specs=out_specs,
        out_shape=out_shape,
        compiler_params=_params(1),
        name="l0_front_rope" if use_rope else "l0_front",
    )(*args)


def _kvc_kernel(ckv_ref, kr_ref, wk_ref, wv_ref, kng_ref, k_ref, v_ref):
    kpre = _kv_matmuls(ckv_ref[...].astype(BF16), wk_ref, wv_ref, v_ref)
    _k_heads(kpre, kr_ref[...], kng_ref, None, k_ref)


def _kvc_call(ckv, kr_pad, w, tm):
    n_tok = ckv.shape[0]
    row_spec = lambda width: pl.BlockSpec((tm, width), lambda t: (t, 0))
    return pl.pallas_call(
        _kvc_kernel,
        grid=(n_tok // tm,),
        in_specs=[row_spec(KV_RANK), row_spec(LANES), _const_spec((KV_RANK, HEAD_W)),
                  _const_spec((HEAD_W, KV_RANK)), _const_spec((1, LANES))],
        out_specs=[row_spec(HEAD_W), pl.BlockSpec((HEAD_W, tm), lambda t: (0, t))],
        out_shape=[jax.ShapeDtypeStruct((n_tok, HEAD_W), BF16),
                   jax.ShapeDtypeStruct((HEAD_W, n_tok), BF16)],
        compiler_params=_params(1),
        name="cache_kv",
    )(ckv, kr_pad, w["w_k"], w["w_vt"], w["kn_g"])


ATT_TQ = 256
ATT_KC = 512
ATT_AHEAD = 6


def _att_kernel(xc_ref, ypc_ref, qc_ref, kc_ref, vc_ref,
                xl_ref, ypl_ref, ql_ref, kl_ref, vl_ref, kp_ref, vp_ref,
                mod_ref, wop_ref, woa_ref, o_ref, ya_ref, *, n_ctx_steps, n_q):
    t = pl.program_id(0)

    @pl.when(t < n_ctx_steps)
    def _():
        _att_tile(xc_ref, ypc_ref, qc_ref, kc_ref, vc_ref, None, None,
                  mod_ref, 0, wop_ref, woa_ref, o_ref, ya_ref)

    @pl.when(t >= n_ctx_steps)
    def _():
        _att_tile(xl_ref, ypl_ref, ql_ref, kl_ref, vl_ref, kp_ref, vp_ref,
                  mod_ref, 1 + (t - n_ctx_steps) // n_q, wop_ref, woa_ref, o_ref, ya_ref)


def _att_tile(x_ref, yp_ref, q_ref, k_ref, v_ref, kc_ref, vc_ref,
              mod_ref, row, wop_ref, woa_ref, o_ref, ya_ref):
    has_cache = kc_ref is not None
    sources = [(k_ref, v_ref)] if not has_cache else [(kc_ref, vc_ref), (k_ref, v_ref)]
    chunks = []
    for ks_ref, vs_ref in sources:
        n_keys = ks_ref.shape[0]
        kc = min(ATT_KC, n_keys)
        assert n_keys % kc == 0
        chunks += [(ks_ref, vs_ref, slice(c * kc, (c + 1) * kc)) for c in range(n_keys // kc)]
    items = [(hd, ch) for hd in range(MLA_HEADS) for ch in chunks]

    def scores(item):
        hd, (ks_ref, _, keys) = item
        sl = slice(hd * LANES, (hd + 1) * LANES)
        s = _dot_nt(ks_ref[keys, sl], q_ref[:, sl])
        return s, jnp.max(s, axis=0, keepdims=True)

    pending = [scores(it) for it in items[:ATT_AHEAD]]
    state = {}
    for n, (hd, (_, vs_ref, keys)) in enumerate(items):
        if n + ATT_AHEAD < len(items):
            pending.append(scores(items[n + ATT_AHEAD]))
        s, cmax = pending.pop(0)
        sl = slice(hd * LANES, (hd + 1) * LANES)
        if hd not in state:
            mx = cmax
            p = jnp.exp2(s - mx)
            den = jnp.sum(p, axis=0, keepdims=True)
            acc = _dot(vs_ref[sl, keys], p.astype(BF16))
        else:
            mx, den, acc = state[hd]
            new = jnp.maximum(mx, cmax)
            alpha = jnp.exp2(mx - new)
            p = jnp.exp2(s - new)
            den = alpha * den + jnp.sum(p, axis=0, keepdims=True)
            acc = alpha * acc + _dot(vs_ref[sl, keys], p.astype(BF16))
            mx = new
        state[hd] = (mx, den, acc)
        if hd % 2 == 1 and keys.stop == vs_ref.shape[1] and vs_ref is v_ref:
            j = hd // 2
            pair = state[hd - 1][2] / state[hd - 1][1] + acc / den
            ya_ref[j * LANES:(j + 1) * LANES, :] = pair.astype(BF16)

    y = _dot(yp_ref[...], wop_ref[...]) + _dot_tn(ya_ref[...], woa_ref[...])
    o_ref[...] = x_ref[...] + _mod(mod_ref, 2, row) * y


def _att_call(ctx_in, lat_in, cache, mod_all, layer, ctx_len, lat_len, w):
    assert ctx_len == ATT_TQ
    n_ctx, n_lat = ctx_in[0].shape[0], lat_in[0].shape[0]
    n_c = n_ctx // ATT_TQ
    n_q = lat_len // ATT_TQ
    past = cache[0].shape[0] // (n_lat // lat_len)
    cstep = lambda t: jnp.minimum(t, n_c - 1)
    lstep = lambda t: jnp.maximum(t - n_c, 0)
    crow = lambda width: pl.BlockSpec((ATT_TQ, width), lambda t: (cstep(t), 0))
    lrow = lambda width: pl.BlockSpec((ATT_TQ, width), lambda t: (lstep(t), 0))
    lbat = lambda rows: pl.BlockSpec((rows, HEAD_W), lambda t: (lstep(t) // n_q, 0))
    lbat_t = lambda cols: pl.BlockSpec((HEAD_W, cols), lambda t: (0, lstep(t) // n_q))
    in_specs = [
        crow(D_MODEL), crow(POOL_WIDTH), crow(HEAD_W), crow(HEAD_W),
        pl.BlockSpec((HEAD_W, ATT_TQ), lambda t: (0, cstep(t))),
        lrow(D_MODEL), lrow(POOL_WIDTH), lrow(HEAD_W), lbat(lat_len), lbat_t(lat_len),
        lbat(past), lbat_t(past),
        _mod_spec(layer), _const_spec((POOL_WIDTH, D_MODEL)), _const_spec((MLA_WIDTH, D_MODEL)),
    ]
    return pl.pallas_call(
        functools.partial(_att_kernel, n_ctx_steps=n_c, n_q=n_q),
        grid=(n_c + n_lat // ATT_TQ,),
        in_specs=in_specs,
        out_specs=pl.BlockSpec((ATT_TQ, D_MODEL), lambda t: (t, 0)),
        out_shape=jax.ShapeDtypeStruct((n_ctx + n_lat, D_MODEL), F32),
        scratch_shapes=[pltpu.VMEM((MLA_WIDTH, ATT_TQ), BF16)],
        compiler_params=_params(1),
        name="att_out",
    )(*ctx_in, *lat_in, *cache, mod_all, w["w_out_pool"], w["w_out_att"])


FFN_SUB = 256
FFN_GCH = D_MODEL // W_STEPS
FFN_DCH = D_FF // W_STEPS


def _ffn_kernel(x_ref, mod_ref, ng_ref, wg_ref, wu_ref, wd_ref, *rest, mod_row, split):
    outs, (wg_s, wu_s, wd_s) = rest[:-3], rest[-3:]
    t = pl.program_id(0)

    @pl.when(t < W_STEPS)
    def _():
        r = pl.multiple_of(t * FFN_GCH, FFN_GCH)
        wg_s[pl.ds(r, FFN_GCH), :] = wg_ref[0].astype(BF16)
        wu_s[pl.ds(r, FFN_GCH), :] = wu_ref[0].astype(BF16)
        r = pl.multiple_of(t * FFN_DCH, FFN_DCH)
        wd_s[pl.ds(r, FFN_DCH), :] = wd_ref[0].astype(BF16)

    def tile(o_ref):
        row = mod_row(t - W_STEPS)
        scale = ng_ref[...] * (1.0 + _mod(mod_ref, 4, row))
        shift = _mod(mod_ref, 3, row)
        gate = _mod(mod_ref, 5, row)
        for s in range(x_ref.shape[0] // FFN_SUB):
            rows = slice(s * FFN_SUB, (s + 1) * FFN_SUB)
            x = x_ref[rows, :]
            h = (_rms(x) * scale + shift).astype(BF16)
            g = _dot(h, wg_s[...])
            u = _dot(h, wu_s[...])
            d = _dot((_silu(g) * u).astype(BF16), wd_s[...])
            o_ref[rows, :] = x + gate * d

    if split is None:
        pl.when(t >= W_STEPS)(lambda: tile(outs[0]))
    else:
        pl.when((t >= W_STEPS) & (t < W_STEPS + split))(lambda: tile(outs[0]))
        pl.when(t >= W_STEPS + split)(lambda: tile(outs[1]))


def _ffn_call(x, mod_all, layer, mod_row, tm, ng, wg, wu, wd, n_first=None):
    n_tok = x.shape[0]
    n_tiles = n_tok // tm
    tok = lambda t: jnp.maximum(t - W_STEPS, 0)
    wstep = lambda t: jnp.minimum(t, W_STEPS - 1)
    if n_first is None:
        split = None
        out_specs = pl.BlockSpec((tm, D_MODEL), lambda t: (tok(t), 0))
        out_shape = jax.ShapeDtypeStruct((n_tok, D_MODEL), F32)
    else:
        split = n_first // tm
        out_specs = [
            pl.BlockSpec((tm, D_MODEL), lambda t: (jnp.minimum(tok(t), split - 1), 0)),
            pl.BlockSpec((tm, D_MODEL), lambda t: (jnp.maximum(tok(t) - split, 0), 0)),
        ]
        out_shape = [jax.ShapeDtypeStruct((n_first, D_MODEL), F32),
                     jax.ShapeDtypeStruct((n_tok - n_first, D_MODEL), F32)]
    return pl.pallas_call(
        functools.partial(_ffn_kernel, mod_row=mod_row, split=split),
        grid=(W_STEPS + n_tiles,),
        in_specs=[
            pl.BlockSpec((tm, D_MODEL), lambda t: (tok(t), 0)),
            _mod_spec(layer),
            _const_spec((1, D_MODEL)),
            pl.BlockSpec((1, FFN_GCH, D_FF), lambda t: (layer, wstep(t), 0)),
            pl.BlockSpec((1, FFN_GCH, D_FF), lambda t: (layer, wstep(t), 0)),
            pl.BlockSpec((1, FFN_DCH, D_MODEL), lambda t: (layer, wstep(t), 0)),
        ],
        out_specs=out_specs,
        out_shape=out_shape,
        scratch_shapes=[pltpu.VMEM((D_MODEL, D_FF), BF16), pltpu.VMEM((D_MODEL, D_FF), BF16),
                        pltpu.VMEM((D_FF, D_MODEL), BF16)],
        compiler_params=_params(1),
        name="ffn",
    )(x, mod_all, ng, wg, wu, wd)


GM_TM = 1024
GM_SUB = 256
GM_WCH = D_MODEL // W_STEPS


def _gmlp_kernel(x_ref, mod_ref, ng_ref, win_ref, vg_ref, ws_ref, bs_ref, wout_ref, o_ref,
                 win_s, wout_s, ws_s, gs_ref, *, mod_row):
    t = pl.program_id(0)

    @pl.when(t < W_STEPS)
    def _():
        r = pl.multiple_of(t * GM_WCH, GM_WCH)
        win_s[pl.ds(r, GM_WCH), :] = win_ref[0].astype(BF16)
        wout_s[pl.ds(r, GM_WCH), :] = wout_ref[0].astype(BF16)
        ws_s[t] = ws_ref[0, 0].astype(BF16)

    @pl.when(t >= W_STEPS)
    def _():
        row = mod_row(t - W_STEPS)
        scale = ng_ref[...] * (1.0 + _mod(mod_ref, 1, row))
        shift = _mod(mod_ref, 0, row)
        gate = _mod(mod_ref, 2, row)
        n_chunks = GM_SUB // CHUNK
        n_sub = x_ref.shape[0] // GM_SUB

        def project(s):
            x = x_ref[s * GM_SUB:(s + 1) * GM_SUB, :]
            h = (_rms(x) * scale + shift).astype(BF16)
            return x, _dot(h, win_s[...])

        ahead = project(0)
        for s in range(n_sub):
            rows = slice(s * GM_SUB, (s + 1) * GM_SUB)
            x, z = ahead
            if s + 1 < n_sub:
                ahead = project(s + 1)
            vn = (_rms(z[:, C_WIDTH:]) * vg_ref[...]).astype(BF16)
            for g in range(C_GROUPS):
                cols = slice(g * C_CH, (g + 1) * C_CH)
                rhs = jnp.concatenate(
                    [vn[n * CHUNK:(n + 1) * CHUNK, cols] for n in range(n_chunks)], axis=1)
                sp = _dot(ws_s[g], rhs)
                for n in range(n_chunks):
                    crow = slice(n * CHUNK, (n + 1) * CHUNK)
                    grow = slice(s * GM_SUB + n * CHUNK, s * GM_SUB + (n + 1) * CHUNK)
                    sn = sp[:, n * C_CH:(n + 1) * C_CH] + bs_ref[g]
                    gs_ref[grow, cols] = (z[crow, cols] * sn).astype(BF16)
            o_ref[rows, :] = x + gate * _dot(gs_ref[rows, :], wout_s[...])


def _gmlp_call(x, mod_all, layer, mod_row, w, o):
    n_tok = x.shape[0]
    tok = lambda t: jnp.maximum(t - W_STEPS, 0)
    wstep = lambda t: jnp.minimum(t, W_STEPS - 1)
    assert C_GROUPS == W_STEPS
    return pl.pallas_call(
        functools.partial(_gmlp_kernel, mod_row=mod_row),
        grid=(W_STEPS + n_tok // GM_TM,),
        in_specs=[
            pl.BlockSpec((GM_TM, D_MODEL), lambda t: (tok(t), 0)),
            _mod_spec(layer),
            _const_spec((1, D_MODEL)),
            pl.BlockSpec((1, GM_WCH, 2 * C_WIDTH), lambda t: (o, wstep(t), 0)),
            _const_spec((1, C_WIDTH)),
            pl.BlockSpec((1, 1, CHUNK, CHUNK), lambda t: (o, wstep(t), 0, 0)),
            _const_spec((C_GROUPS, CHUNK, C_CH)),
            pl.BlockSpec((1, GM_WCH, D_MODEL), lambda t: (o, wstep(t), 0)),
        ],
        out_specs=pl.BlockSpec((GM_TM, D_MODEL), lambda t: (tok(t), 0)),
        out_shape=jax.ShapeDtypeStruct((n_tok, D_MODEL), F32),
        scratch_shapes=[pltpu.VMEM((D_MODEL, 2 * C_WIDTH), BF16), pltpu.VMEM((C_WIDTH, D_MODEL), BF16),
                        pltpu.VMEM((C_GROUPS, CHUNK, CHUNK), BF16), pltpu.VMEM((GM_TM, C_WIDTH), BF16)],
        compiler_params=_params(1),
        name="gmlp",
    )(x, mod_all, w["norm_mix_g"], w["w_in"], w["vnorm_g"], w["w_s"], w["b_s"], w["w_out"])


def _swap_halves(t):
    return jnp.concatenate([t[..., HALF:], t[..., :HALF]], axis=-1)


def _head_ones():
    lane = np.arange(2 * LANES)
    m = (lane[:, None] // LANES == lane[None, :] // LANES) & (lane[:, None] % LANES < QK_DIM)
    return jnp.asarray(m.astype(np.float32), dtype=BF16)


def _layer0_weights(e, norm_mix_g, ab_w_in, pool_w, pool_scale, q_norm_g, kv_norm_g, w_uq, w_ukv,
                    qn_g, kn_g, ab_w_out):
    w_in = ab_w_in[e]
    o3 = POOL_WIDTH + Q_RANK + KV_RANK
    kr_cols = jnp.concatenate(
        [jnp.zeros((D_MODEL, QK_NOPE), F32), w_in[:, o3:], w_in[:, o3:]], axis=1)
    w_in_p = jnp.concatenate([w_in[:, :o3], kr_cols], axis=1).astype(BF16)
    uq = w_uq[e].reshape(Q_RANK, MLA_HEADS, QK_DIM)
    uq = jnp.concatenate([uq, _swap_halves(uq[:, :, QK_NOPE:])], axis=-1)
    score_scale = QK_DIM ** -0.5 * math.log2(math.e)
    gq = qn_g[e] * score_scale
    gk = kn_g[e]
    ukv = w_ukv[e].reshape(KV_RANK, MLA_HEADS, QK_NOPE + V_DIM)
    w_k = jnp.pad(ukv[:, :, :QK_NOPE], ((0, 0), (0, 0), (0, LANES - QK_NOPE)))
    w_v = ukv[:, :, QK_NOPE:].reshape(KV_RANK, MLA_HEADS // 2, 2, V_DIM)
    zero = jnp.zeros_like(w_v[:, :, 0])
    w_v = jnp.stack([jnp.concatenate([w_v[:, :, 0], zero], axis=-1),
                     jnp.concatenate([zero, w_v[:, :, 1]], axis=-1)], axis=2)
    return {
        "norm_mix_g": norm_mix_g.reshape(1, D_MODEL),
        "w_in": w_in_p,
        "q_norm_g": q_norm_g[e].reshape(1, Q_RANK),
        "kv_norm_g": kv_norm_g[e].reshape(1, KV_RANK),
        "w_uq": uq.reshape(Q_RANK, HEAD_W).astype(BF16),
        "w_k": w_k.reshape(KV_RANK, HEAD_W).astype(BF16),
        "w_vt": w_v.reshape(KV_RANK, HEAD_W).T.astype(BF16),
        "qn_g": jnp.pad(gq, (0, LANES - QK_DIM)).reshape(1, LANES),
        "qn_g_rope": jnp.concatenate([gq, _swap_halves(gq[QK_NOPE:])]).reshape(1, LANES),
        "kn_g": jnp.concatenate([gk, gk[QK_NOPE:]]).reshape(1, LANES),
        "head_ones": _head_ones(),
        "pool_w": pool_w[e].astype(BF16),
        "pool_scale": pool_scale[e].reshape(1, POOL_WIDTH),
        "w_out_pool": ab_w_out[e, :POOL_WIDTH].astype(BF16),
        "w_out_att": ab_w_out[e, POOL_WIDTH:].astype(BF16),
    }


def _rope_tables(seq_len):
    rows = seq_len // GRID_W
    row = np.repeat(np.arange(rows), GRID_W).astype(np.float32)
    col = np.tile(np.arange(GRID_W), rows).astype(np.float32)
    per_axis = QK_ROPE // 2
    inv = (1.0 / (np.float32(ROPE_BASE) ** (np.arange(0, per_axis, 2, dtype=np.float32) / per_axis))
           ).astype(np.float32)
    ang = np.concatenate([row[:, None] * inv, col[:, None] * inv], axis=-1)
    cos, sin = np.cos(ang).astype(np.float32), np.sin(ang).astype(np.float32)
    ones = np.ones((seq_len, QK_NOPE), np.float32)
    z_nope = np.zeros((seq_len, QK_NOPE), np.float32)
    z_half = np.zeros((seq_len, HALF), np.float32)
    c_tab = np.concatenate([ones, cos, cos, cos, cos], axis=-1)
    a_tab = np.concatenate([z_nope, -sin, z_half, -sin, z_half], axis=-1)
    b_tab = np.concatenate([z_nope, z_half, sin, z_half, sin], axis=-1)
    q_tab = np.concatenate([ones, cos, cos, -sin, sin], axis=-1)
    return jnp.asarray(np.stack([c_tab, a_tab, b_tab, q_tab]))


def kernel(x_prompt, x_sample, cache_ckv, cache_krope, c, c_ctx, ada_w, ada_b, norm_mix_g, norm_ffn_g, ffn_wg, ffn_wu, ffn_wd, ab_w_in, pool_w, pool_scale, q_norm_g, kv_norm_g, w_uq, w_ukv, qn_g, kn_g, ab_w_out, gm_w_in, gm_vnorm_g, gm_ws, gm_bs, gm_w_out):
    n_ctx_b, ctx_len, _ = x_prompt.shape
    n_lat_b, lat_len, _ = x_sample.shape
    past = cache_ckv.shape[2]
    assert ctx_len & (ctx_len - 1) == 0 and lat_len & (lat_len - 1) == 0
    assert MOD_ROWS >= 1 + n_lat_b
    assert DEPTH == 2
    n_ctx, n_lat = n_ctx_b * ctx_len, n_lat_b * lat_len

    ctx = x_prompt.reshape(n_ctx, D_MODEL)
    lat = x_sample.reshape(n_lat, D_MODEL)

    cond = jnp.concatenate(
        [c_ctx[None, :], c, jnp.zeros((MOD_ROWS - 1 - n_lat_b, D_MODEL), F32)], axis=0)
    mod_all = _ada_call(cond, ada_w, ada_b)

    def all_row(tm):
        n_ctx_tiles, per = n_ctx // tm, lat_len // tm
        return lambda j: jnp.where(j < n_ctx_tiles, 0, 1 + (j - n_ctx_tiles) // per)

    w = _layer0_weights(0, norm_mix_g[0], ab_w_in, pool_w, pool_scale, q_norm_g, kv_norm_g,
                        w_uq, w_ukv, qn_g, kn_g, ab_w_out)
    cs = _rope_tables(lat_len)
    yp_c, q_c, k_c, v_c, ckv_c, kr_c = _l0a_call(ctx, mod_all, 0, lambda t: 0, 1024, ctx_len, w, None)
    yp_l, q_l, k_l, v_l, _, _ = _l0a_call(lat, mod_all, 0, lambda t: 1 + t, lat_len, lat_len, w, cs)
    kr_c2 = cache_krope[:, 0].reshape(n_lat_b * past, QK_ROPE)
    kr_pad = jnp.concatenate([jnp.zeros((n_lat_b * past, QK_NOPE), F32), kr_c2, kr_c2], axis=1)
    cache = _kvc_call(cache_ckv[:, 0].reshape(n_lat_b * past, KV_RANK), kr_pad, w, 1024)
    xs = _att_call((ctx, yp_c, q_c, k_c, v_c), (lat, yp_l, q_l, k_l, v_l), cache,
                   mod_all, 0, ctx_len, lat_len, w)
    xs = _ffn_call(xs, mod_all, 0, all_row(1024), 1024, norm_ffn_g[0].reshape(1, D_MODEL),
                   ffn_wg, ffn_wu, ffn_wd)

    gw = {
        "norm_mix_g": norm_mix_g[1].reshape(1, D_MODEL),
        "w_in": gm_w_in,
        "vnorm_g": gm_vnorm_g[0].reshape(1, C_WIDTH),
        "w_s": gm_ws,
        "b_s": jnp.broadcast_to(gm_bs[0][:, :, None], (C_GROUPS, CHUNK, C_CH)),
        "w_out": gm_w_out,
    }
    xs = _gmlp_call(xs, mod_all, 1, all_row(GM_TM), gw, 0)
    y_ctx, y_lat = _ffn_call(xs, mod_all, 1, all_row(512), 512, norm_ffn_g[1].reshape(1, D_MODEL),
                             ffn_wg, ffn_wu, ffn_wd, n_first=n_ctx)

    state_ckv = ckv_c.reshape(n_ctx_b, 1, ctx_len, KV_RANK)
    state_krope = kr_c.reshape(n_ctx_b, 1, ctx_len, QK_ROPE)
    return (y_ctx.reshape(n_ctx_b, ctx_len, D_MODEL), y_lat.reshape(n_lat_b, lat_len, D_MODEL),
            state_ckv, state_krope)
```

```python
import functools
import math

import numpy as np
import jax
import jax.numpy as jnp
from jax import lax
from jax.experimental import pallas as pl
from jax.experimental.pallas import tpu as pltpu

D_MODEL = 1024
DEPTH = 2
GRID_W = 64
POOL_WINDOWS = (2, 4, 8, 16)
POOL_GROUPS = 4
POOL_CH = 128
POOL_WIDTH = POOL_GROUPS * POOL_CH
MLA_HEADS = 8
QK_NOPE = 64
QK_ROPE = 32
QK_DIM = QK_NOPE + QK_ROPE
HALF = QK_ROPE // 2
SQRT_QK = math.sqrt(QK_DIM)
V_DIM = 64
Q_RANK = 384
KV_RANK = 256
MLA_WIDTH = MLA_HEADS * V_DIM
CHUNK = 128
C_GROUPS = 8
C_WIDTH = D_MODEL
C_CH = C_WIDTH // C_GROUPS
D_FF = 2816
ROPE_BASE = 10000.0
EPS = 1e-6

LANES = 128
HEAD_W = MLA_HEADS * LANES
AB_IN_PAD = POOL_WIDTH + Q_RANK + KV_RANK + LANES
MOD_ROWS = 16
VMEM_LIMIT = 58 * 1024 * 1024
W_STEPS = 8
Q_SS_MXU_PAIRS = 2

F32 = jnp.float32
BF16 = jnp.bfloat16


def _rms(x):
    return x * lax.rsqrt(jnp.mean(x * x, axis=-1, keepdims=True) + EPS)


def _dot(a, b):
    return jnp.dot(a, b, preferred_element_type=F32)


def _dot_nt(a, b):
    return lax.dot_general(a, b, (((1,), (1,)), ((), ())), preferred_element_type=F32)


def _dot_tn(a, b):
    return lax.dot_general(a, b, (((0,), (0,)), ((), ())), preferred_element_type=F32)


def _silu(x):
    return x * jax.nn.sigmoid(x)


def _params(n_axes):
    return pltpu.CompilerParams(
        dimension_semantics=("arbitrary",) * n_axes, vmem_limit_bytes=VMEM_LIMIT)


def _const_spec(shape):
    zeros = (0,) * len(shape)
    return pl.BlockSpec(shape, lambda *_: zeros)


def _mod_spec(layer):
    return pl.BlockSpec((1, 6, MOD_ROWS, D_MODEL), lambda *_: (layer, 0, 0, 0))


def _mod(mod_ref, term, row):
    return mod_ref[0, term, pl.ds(row, 1), :]


ADA_TERMS = 2


def _ada_kernel(cond_ref, w_ref, b_ref, o_ref):
    s = _silu(cond_ref[...]).astype(BF16)
    bias = b_ref[pl.ds(pl.program_id(0), 1), :]
    y = _dot(s, w_ref[0].astype(BF16)) + bias
    for k in range(ADA_TERMS):
        o_ref[0, k] = y[:, k * D_MODEL:(k + 1) * D_MODEL]


def _ada_call(cond, ada_w, ada_b):
    width = ADA_TERMS * D_MODEL
    return pl.pallas_call(
        _ada_kernel,
        grid=(DEPTH, 6 // ADA_TERMS),
        in_specs=[
            pl.BlockSpec((MOD_ROWS, D_MODEL), lambda i, j: (0, 0)),
            pl.BlockSpec((1, D_MODEL, width), lambda i, j: (i, 0, j)),
            pl.BlockSpec((DEPTH, width), lambda i, j: (0, j)),
        ],
        out_specs=pl.BlockSpec((1, ADA_TERMS, MOD_ROWS, D_MODEL), lambda i, j: (i, j, 0, 0)),
        out_shape=jax.ShapeDtypeStruct((DEPTH, 6, MOD_ROWS, D_MODEL), F32),
        compiler_params=_params(2),
        name="ada_mod",
    )(cond, ada_w, ada_b)


def _rope(t, cs_ref):
    return (t * cs_ref[0]
            + pltpu.roll(t, LANES - HALF, axis=1) * cs_ref[1]
            + pltpu.roll(t, HALF, axis=1) * cs_ref[2])


def _kv_matmuls(ckv_bf, wk_ref, wv_ref, v_ref):
    v_ref[...] = _dot_nt(wv_ref[...], ckv_bf).astype(BF16)
    return _dot(ckv_bf, wk_ref[...])


def _k_heads(kpre, kr, kng_ref, cs_ref, k_ref):
    kg = kng_ref[...] * SQRT_QK
    krg = kr * kg
    if cs_ref is not None:
        krg = _rope(krg, cs_ref)
    lane = lax.broadcasted_iota(jnp.int32, kr.shape, 1)
    kr_ss = jnp.sum(jnp.where(lane < QK_DIM, kr * kr, 0.0), axis=-1, keepdims=True) + QK_DIM * EPS
    for h in range(MLA_HEADS):
        kh = kpre[:, h * LANES:(h + 1) * LANES]
        r = lax.rsqrt(jnp.sum(kh * kh, axis=-1, keepdims=True) + kr_ss)
        k_ref[:, h * LANES:(h + 1) * LANES] = ((kh * kg + krg) * r).astype(BF16)


def _shift_rows(a, k, pos, seq_len):
    n = a.shape[0]
    r = pltpu.roll(a, k % n, axis=0)
    src = pos - k
    ok = (src >= 0) if k > 0 else (src < seq_len)
    return jnp.where(ok, r, 0.0)


def _pool_group(ug, half, pos, seq_len):
    fw = ug
    step = 1
    while step < half:
        fw = fw + _shift_rows(fw, -step, pos, seq_len)
        step *= 2
    bk = _shift_rows(ug, 1, pos, seq_len)
    step = 1
    while step < half:
        bk = bk + _shift_rows(bk, step, pos, seq_len)
        step *= 2
    cnt = jnp.minimum(pos + half, seq_len) - jnp.maximum(pos - half, 0)
    return (fw + bk) / cnt.astype(F32) - ug


def _l0a_kernel(*refs, seq_len, use_rope, mod_row):
    (x_ref, mod_ref, ng_ref, win_ref, qg_ref, kvg_ref, wuq_ref, wk_ref, wv_ref,
     qng_ref, kng_ref, pw_ref, ps_ref, ones_ref) = refs[:14]
    if use_rope:
        cs_ref = refs[14]
        outs = refs[15:]
    else:
        cs_ref = None
        outs = refs[14:]
    yp_ref, q_ref, k_ref, v_ref, ckv_ref, kr_ref = outs

    tm = x_ref.shape[0]
    row = mod_row(pl.program_id(0))
    scale = ng_ref[...] * (1.0 + _mod(mod_ref, 1, row))
    h = _rms(x_ref[...]) * scale + _mod(mod_ref, 0, row)
    z = _dot(h.astype(BF16), win_ref[...])
    o1, o2, o3 = POOL_WIDTH, POOL_WIDTH + Q_RANK, POOL_WIDTH + Q_RANK + KV_RANK

    ckv = _rms(z[:, o2:o3]) * kvg_ref[...]
    kr = z[:, o3:]
    ckv_ref[...] = ckv
    kr_ref[...] = kr[:, QK_NOPE:QK_DIM]
    cqn = (_rms(z[:, o1:o2]) * qg_ref[...]).astype(BF16)
    kpre = _kv_matmuls(ckv.astype(BF16), wk_ref, wv_ref, v_ref)
    qf = _dot(cqn, wuq_ref[...])

    qt = qng_ref[...] * SQRT_QK
    if use_rope:
        qt = qt * cs_ref[3]
    lane2 = lax.broadcasted_iota(jnp.int32, (tm, 2 * LANES), 1)
    for j in range(MLA_HEADS // 2):
        qp = qf[:, 2 * j * LANES:(2 * j + 2) * LANES]
        sq = qp * qp
        if j < Q_SS_MXU_PAIRS:
            hi = sq.astype(BF16)
            lo = (sq - hi.astype(F32)).astype(BF16)
            ss = _dot(hi, ones_ref[...]) + _dot(lo, ones_ref[...])
            qn = qp * lax.rsqrt(ss + QK_DIM * EPS)
            parts = [qn[:, :LANES], qn[:, LANES:]]
        else:
            sq = jnp.where((lane2 & (LANES - 1)) < QK_DIM, sq, 0.0)
            parts = []
            for hh in range(2):
                ss = jnp.sum(sq[:, hh * LANES:(hh + 1) * LANES], axis=-1, keepdims=True)
                parts.append(qp[:, hh * LANES:(hh + 1) * LANES] * lax.rsqrt(ss + QK_DIM * EPS))
        for hh in range(2):
            hd = 2 * j + hh
            q_ref[:, hd * LANES:(hd + 1) * LANES] = (parts[hh] * qt).astype(BF16)

    _k_heads(kpre, kr, kng_ref, cs_ref, k_ref)

    pos = lax.broadcasted_iota(jnp.int32, (tm, POOL_CH), 0) & (seq_len - 1)
    for g, w in enumerate(POOL_WINDOWS):
        ug = z[:, g * POOL_CH:(g + 1) * POOL_CH]
        p = _pool_group(ug, w // 2, pos, seq_len)
        y = _dot(p.astype(BF16), pw_ref[g]) * ps_ref[:, g * POOL_CH:(g + 1) * POOL_CH]
        yp_ref[:, g * POOL_CH:(g + 1) * POOL_CH] = y.astype(BF16)


def _l0a_call(x, mod_all, layer, mod_row, tm, seq_len, w, cs):
    n_tok = x.shape[0]
    use_rope = cs is not None
    row_spec = lambda width: pl.BlockSpec((tm, width), lambda t: (t, 0))
    in_specs = [
        row_spec(D_MODEL),
        _mod_spec(layer),
        _const_spec((1, D_MODEL)),
        _const_spec((D_MODEL, AB_IN_PAD)),
        _const_spec((1, Q_RANK)),
        _const_spec((1, KV_RANK)),
        _const_spec((Q_RANK, HEAD_W)),
        _const_spec((KV_RANK, HEAD_W)),
        _const_spec((HEAD_W, KV_RANK)),
        _const_spec((1, LANES)),
        _const_spec((1, LANES)),
        _const_spec((POOL_GROUPS, POOL_CH, POOL_CH)),
        _const_spec((1, POOL_WIDTH)),
        _const_spec((2 * LANES, 2 * LANES)),
    ]
    args = [x, mod_all, w["norm_mix_g"], w["w_in"], w["q_norm_g"], w["kv_norm_g"], w["w_uq"],
            w["w_k"], w["w_vt"], w["qn_g_rope"] if use_rope else w["qn_g"], w["kn_g"],
            w["pool_w"], w["pool_scale"], w["head_ones"]]
    if use_rope:
        assert tm == seq_len
        in_specs.append(_const_spec((4, seq_len, LANES)))
        args.append(cs)
    out_shape = [
        jax.ShapeDtypeStruct((n_tok, POOL_WIDTH), BF16),
        jax.ShapeDtypeStruct((n_tok, HEAD_W), BF16),
        jax.ShapeDtypeStruct((n_tok, HEAD_W), BF16),
        jax.ShapeDtypeStruct((HEAD_W, n_tok), BF16),
        jax.ShapeDtypeStruct((n_tok, KV_RANK), F32),
        jax.ShapeDtypeStruct((n_tok, QK_ROPE), F32),
    ]
    out_specs = [row_spec(POOL_WIDTH), row_spec(HEAD_W), row_spec(HEAD_W),
                 pl.BlockSpec((HEAD_W, tm), lambda t: (0, t)),
                 row_spec(KV_RANK), row_spec(QK_ROPE)]
    return pl.pallas_call(
        functools.partial(_l0a_kernel, seq_len=seq_len, use_rope=use_rope, mod_row=mod_row),
        grid=(n_tok // tm,),
        in_specs=in_specs,
        out_specs=out_specs,
        out_shape=out_shape,
        compiler_params=_params(1),
        name="l0_front_rope" if use_rope else "l0_front",
    )(*args)


def _kvc_kernel(ckv_ref, kr_ref, wk_ref, wv_ref, kng_ref, k_ref, v_ref):
    kpre = _kv_matmuls(ckv_ref[...].astype(BF16), wk_ref, wv_ref, v_ref)
    _k_heads(kpre, kr_ref[...], kng_ref, None, k_ref)


def _kvc_call(ckv, kr_pad, w, tm):
    n_tok = ckv.shape[0]
    row_spec = lambda width: pl.BlockSpec((tm, width), lambda t: (t, 0))
    return pl.pallas_call(
        _kvc_kernel,
        grid=(n_tok // tm,),
        in_specs=[row_spec(KV_RANK), row_spec(LANES), _const_spec((KV_RANK, HEAD_W)),
                  _const_spec((HEAD_W, KV_RANK)), _const_spec((1, LANES))],
        out_specs=[row_spec(HEAD_W), pl.BlockSpec((HEAD_W, tm), lambda t: (0, t))],
        out_shape=[jax.ShapeDtypeStruct((n_tok, HEAD_W), BF16),
                   jax.ShapeDtypeStruct((HEAD_W, n_tok), BF16)],
        compiler_params=_params(1),
        name="cache_kv",
    )(ckv, kr_pad, w["w_k"], w["w_vt"], w["kn_g"])


ATT_TQ = 256
ATT_KC = 512
ATT_AHEAD = 6
ATT_FFN_CHUNKS = 16


def _ffn_cast_specs(layer, n_chunks, chunk_of):
    idx = lambda t: jnp.clip(chunk_of(t), 0, n_chunks - 1)
    gch, dch = D_MODEL // n_chunks, D_FF // n_chunks
    in_specs = [pl.BlockSpec((1, gch, D_FF), lambda t: (layer, idx(t), 0)),
                pl.BlockSpec((1, gch, D_FF), lambda t: (layer, idx(t), 0)),
                pl.BlockSpec((1, dch, D_MODEL), lambda t: (layer, idx(t), 0))]
    out_specs = [pl.BlockSpec((gch, D_FF), lambda t: (idx(t), 0)),
                 pl.BlockSpec((gch, D_FF), lambda t: (idx(t), 0)),
                 pl.BlockSpec((dch, D_MODEL), lambda t: (idx(t), 0))]
    out_shape = [jax.ShapeDtypeStruct((D_MODEL, D_FF), BF16), jax.ShapeDtypeStruct((D_MODEL, D_FF), BF16),
                 jax.ShapeDtypeStruct((D_FF, D_MODEL), BF16)]
    return in_specs, out_specs, out_shape


def _ffn_cast_rider(chunk, n_chunks, srcs, dsts):
    @pl.when((chunk >= 0) & (chunk < n_chunks))
    def _():
        for src, dst in zip(srcs, dsts):
            dst[...] = src[0].astype(BF16)


def _att_kernel(xc_ref, ypc_ref, qc_ref, kc_ref, vc_ref,
                xl_ref, ypl_ref, ql_ref, kl_ref, vl_ref, kp_ref, vp_ref,
                mod_ref, wop_ref, woa_ref, fg_ref, fu_ref, fd_ref,
                o_ref, fg_out, fu_out, fd_out, ya_ref, *, n_ctx_steps, n_q):
    t = pl.program_id(0)
    _ffn_cast_rider(t - n_ctx_steps, ATT_FFN_CHUNKS, (fg_ref, fu_ref, fd_ref), (fg_out, fu_out, fd_out))

    @pl.when(t < n_ctx_steps)
    def _():
        _att_tile(xc_ref, ypc_ref, qc_ref, kc_ref, vc_ref, None, None,
                  mod_ref, 0, wop_ref, woa_ref, o_ref, ya_ref)

    @pl.when(t >= n_ctx_steps)
    def _():
        _att_tile(xl_ref, ypl_ref, ql_ref, kl_ref, vl_ref, kp_ref, vp_ref,
                  mod_ref, 1 + (t - n_ctx_steps) // n_q, wop_ref, woa_ref, o_ref, ya_ref)


def _att_tile(x_ref, yp_ref, q_ref, k_ref, v_ref, kc_ref, vc_ref,
              mod_ref, row, wop_ref, woa_ref, o_ref, ya_ref):
    has_cache = kc_ref is not None
    sources = [(k_ref, v_ref)] if not has_cache else [(kc_ref, vc_ref), (k_ref, v_ref)]
    chunks = []
    for ks_ref, vs_ref in sources:
        n_keys = ks_ref.shape[0]
        kc = min(ATT_KC, n_keys)
        assert n_keys % kc == 0
        chunks += [(ks_ref, vs_ref, slice(c * kc, (c + 1) * kc)) for c in range(n_keys // kc)]
    items = [(hd, ch) for hd in range(MLA_HEADS) for ch in chunks]

    def scores(item):
        hd, (ks_ref, _, keys) = item
        sl = slice(hd * LANES, (hd + 1) * LANES)
        s = _dot_nt(ks_ref[keys, sl], q_ref[:, sl])
        return s, jnp.max(s, axis=0, keepdims=True)

    pending = [scores(it) for it in items[:ATT_AHEAD]]
    state = {}
    for n, (hd, (_, vs_ref, keys)) in enumerate(items):
        if n + ATT_AHEAD < len(items):
            pending.append(scores(items[n + ATT_AHEAD]))
        s, cmax = pending.pop(0)
        sl = slice(hd * LANES, (hd + 1) * LANES)
        if hd not in state:
            mx = cmax
            p = jnp.exp2(s - mx)
            den = jnp.sum(p, axis=0, keepdims=True)
            acc = _dot(vs_ref[sl, keys], p.astype(BF16))
        else:
            mx, den, acc = state[hd]
            new = jnp.maximum(mx, cmax)
            alpha = jnp.exp2(mx - new)
            p = jnp.exp2(s - new)
            den = alpha * den + jnp.sum(p, axis=0, keepdims=True)
            acc = alpha * acc + _dot(vs_ref[sl, keys], p.astype(BF16))
            mx = new
        state[hd] = (mx, den, acc)
        if hd % 2 == 1 and keys.stop == vs_ref.shape[1] and vs_ref is v_ref:
            j = hd // 2
            pair = state[hd - 1][2] / state[hd - 1][1] + acc / den
            ya_ref[j * LANES:(j + 1) * LANES, :] = pair.astype(BF16)

    y = _dot(yp_ref[...], wop_ref[...]) + _dot_tn(ya_ref[...], woa_ref[...])
    o_ref[...] = x_ref[...] + _mod(mod_ref, 2, row) * y


def _att_call(ctx_in, lat_in, cache, mod_all, layer, ctx_len, lat_len, w, ffn_w):
    assert ctx_len == ATT_TQ
    n_ctx, n_lat = ctx_in[0].shape[0], lat_in[0].shape[0]
    n_c = n_ctx // ATT_TQ
    n_q = lat_len // ATT_TQ
    past = cache[0].shape[0] // (n_lat // lat_len)
    cstep = lambda t: jnp.minimum(t, n_c - 1)
    lstep = lambda t: jnp.maximum(t - n_c, 0)
    crow = lambda width: pl.BlockSpec((ATT_TQ, width), lambda t: (cstep(t), 0))
    lrow = lambda width: pl.BlockSpec((ATT_TQ, width), lambda t: (lstep(t), 0))
    lbat = lambda rows: pl.BlockSpec((rows, HEAD_W), lambda t: (lstep(t) // n_q, 0))
    lbat_t = lambda cols: pl.BlockSpec((HEAD_W, cols), lambda t: (0, lstep(t) // n_q))
    in_specs = [
        crow(D_MODEL), crow(POOL_WIDTH), crow(HEAD_W), crow(HEAD_W),
        pl.BlockSpec((HEAD_W, ATT_TQ), lambda t: (0, cstep(t))),
        lrow(D_MODEL), lrow(POOL_WIDTH), lrow(HEAD_W), lbat(lat_len), lbat_t(lat_len),
        lbat(past), lbat_t(past),
        _mod_spec(layer), _const_spec((POOL_WIDTH, D_MODEL)), _const_spec((MLA_WIDTH, D_MODEL)),
    ]
    rider_in, rider_out, rider_shape = _ffn_cast_specs(layer, ATT_FFN_CHUNKS, lambda t: t - n_c)
    outs = pl.pallas_call(
        functools.partial(_att_kernel, n_ctx_steps=n_c, n_q=n_q),
        grid=(n_c + n_lat // ATT_TQ,),
        in_specs=in_specs + rider_in,
        out_specs=[pl.BlockSpec((ATT_TQ, D_MODEL), lambda t: (t, 0))] + rider_out,
        out_shape=[jax.ShapeDtypeStruct((n_ctx + n_lat, D_MODEL), F32)] + rider_shape,
        scratch_shapes=[pltpu.VMEM((MLA_WIDTH, ATT_TQ), BF16)],
        compiler_params=_params(1),
        name="att_out",
    )(*ctx_in, *lat_in, *cache, mod_all, w["w_out_pool"], w["w_out_att"], *ffn_w)
    return outs[0], outs[1:]


FFN_TM = 1024
FFN_SUB = 256


def _ffn_kernel(x_ref, mod_ref, ng_ref, wg_ref, wu_ref, wd_ref, *outs, mod_row, split):
    t = pl.program_id(0)

    def tile(o_ref):
        row = mod_row(t)
        scale = ng_ref[...] * (1.0 + _mod(mod_ref, 4, row))
        shift = _mod(mod_ref, 3, row)
        gate = _mod(mod_ref, 5, row)
        for s in range(x_ref.shape[0] // FFN_SUB):
            rows = slice(s * FFN_SUB, (s + 1) * FFN_SUB)
            x = x_ref[rows, :]
            h = (_rms(x) * scale + shift).astype(BF16)
            g = _dot(h, wg_ref[...])
            u = _dot(h, wu_ref[...])
            d = _dot((_silu(g) * u).astype(BF16), wd_ref[...])
            o_ref[rows, :] = x + gate * d

    if split is None:
        tile(outs[0])
    else:
        pl.when(t < split)(lambda: tile(outs[0]))
        pl.when(t >= split)(lambda: tile(outs[1]))


def _ffn_call(x, mod_all, layer, mod_row, ng, wg, wu, wd, n_first=None):
    n_tok = x.shape[0]
    tm = FFN_TM
    resident = lambda shape: pl.BlockSpec(shape, lambda t: (0, 0), pipeline_mode=pl.Buffered(1))
    if n_first is None:
        split = None
        out_specs = pl.BlockSpec((tm, D_MODEL), lambda t: (t, 0))
        out_shape = jax.ShapeDtypeStruct((n_tok, D_MODEL), F32)
    else:
        split = n_first // tm
        out_specs = [
            pl.BlockSpec((tm, D_MODEL), lambda t: (jnp.minimum(t, split - 1), 0)),
            pl.BlockSpec((tm, D_MODEL), lambda t: (jnp.maximum(t - split, 0), 0)),
        ]
        out_shape = [jax.ShapeDtypeStruct((n_first, D_MODEL), F32),
                     jax.ShapeDtypeStruct((n_tok - n_first, D_MODEL), F32)]
    return pl.pallas_call(
        functools.partial(_ffn_kernel, mod_row=mod_row, split=split),
        grid=(n_tok // tm,),
        in_specs=[
            pl.BlockSpec((tm, D_MODEL), lambda t: (t, 0)),
            _mod_spec(layer),
            _const_spec((1, D_MODEL)),
            resident((D_MODEL, D_FF)), resident((D_MODEL, D_FF)), resident((D_FF, D_MODEL)),
        ],
        out_specs=out_specs,
        out_shape=out_shape,
        compiler_params=_params(1),
        name="ffn",
    )(x, mod_all, ng, wg, wu, wd)


GM_TM = 1024
GM_SUB = 256
GM_WCH = D_MODEL // W_STEPS
GM_FFN_CHUNKS = 8


def _gmlp_kernel(x_ref, mod_ref, ng_ref, win_ref, vg_ref, ws_ref, bs_ref, wout_ref,
                 fg_ref, fu_ref, fd_ref, o_ref, fg_out, fu_out, fd_out,
                 win_s, wout_s, ws_s, gs_ref, *, mod_row):
    t = pl.program_id(0)
    _ffn_cast_rider(t - W_STEPS, GM_FFN_CHUNKS, (fg_ref, fu_ref, fd_ref), (fg_out, fu_out, fd_out))

    @pl.when(t < W_STEPS)
    def _():
        r = pl.multiple_of(t * GM_WCH, GM_WCH)
        win_s[pl.ds(r, GM_WCH), :] = win_ref[0].astype(BF16)
        wout_s[pl.ds(r, GM_WCH), :] = wout_ref[0].astype(BF16)
        ws_s[t] = ws_ref[0, 0].astype(BF16)

    @pl.when(t >= W_STEPS)
    def _():
        row = mod_row(t - W_STEPS)
        scale = ng_ref[...] * (1.0 + _mod(mod_ref, 1, row))
        shift = _mod(mod_ref, 0, row)
        gate = _mod(mod_ref, 2, row)
        n_chunks = GM_SUB // CHUNK
        n_sub = x_ref.shape[0] // GM_SUB

        def project(s):
            x = x_ref[s * GM_SUB:(s + 1) * GM_SUB, :]
            h = (_rms(x) * scale + shift).astype(BF16)
            return x, _dot(h, win_s[...])

        ahead = project(0)
        for s in range(n_sub):
            rows = slice(s * GM_SUB, (s + 1) * GM_SUB)
            x, z = ahead
            if s + 1 < n_sub:
                ahead = project(s + 1)
            vn = (_rms(z[:, C_WIDTH:]) * vg_ref[...]).astype(BF16)
            for g in range(C_GROUPS):
                cols = slice(g * C_CH, (g + 1) * C_CH)
                rhs = jnp.concatenate(
                    [vn[n * CHUNK:(n + 1) * CHUNK, cols] for n in range(n_chunks)], axis=1)
                sp = _dot(ws_s[g], rhs)
                for n in range(n_chunks):
                    crow = slice(n * CHUNK, (n + 1) * CHUNK)
                    grow = slice(s * GM_SUB + n * CHUNK, s * GM_SUB + (n + 1) * CHUNK)
                    sn = sp[:, n * C_CH:(n + 1) * C_CH] + bs_ref[g]
                    gs_ref[grow, cols] = (z[crow, cols] * sn).astype(BF16)
            o_ref[rows, :] = x + gate * _dot(gs_ref[rows, :], wout_s[...])


def _gmlp_call(x, mod_all, layer, mod_row, w, o, ffn_w):
    n_tok = x.shape[0]
    n_tiles = n_tok // GM_TM
    tok = lambda t: jnp.maximum(t - W_STEPS, 0)
    wstep = lambda t: jnp.minimum(t, W_STEPS - 1)
    assert C_GROUPS == W_STEPS and n_tiles >= GM_FFN_CHUNKS
    rider_in, rider_out, rider_shape = _ffn_cast_specs(layer, GM_FFN_CHUNKS, lambda t: t - W_STEPS)
    outs = pl.pallas_call(
        functools.partial(_gmlp_kernel, mod_row=mod_row),
        grid=(W_STEPS + n_tiles,),
        in_specs=[
            pl.BlockSpec((GM_TM, D_MODEL), lambda t: (tok(t), 0)),
            _mod_spec(layer),
            _const_spec((1, D_MODEL)),
            pl.BlockSpec((1, GM_WCH, 2 * C_WIDTH), lambda t: (o, wstep(t), 0)),
            _const_spec((1, C_WIDTH)),
            pl.BlockSpec((1, 1, CHUNK, CHUNK), lambda t: (o, wstep(t), 0, 0)),
            _const_spec((C_GROUPS, CHUNK, C_CH)),
            pl.BlockSpec((1, GM_WCH, D_MODEL), lambda t: (o, wstep(t), 0)),
        ] + rider_in,
        out_specs=[pl.BlockSpec((GM_TM, D_MODEL), lambda t: (tok(t), 0))] + rider_out,
        out_shape=[jax.ShapeDtypeStruct((n_tok, D_MODEL), F32)] + rider_shape,
        scratch_shapes=[pltpu.VMEM((D_MODEL, 2 * C_WIDTH), BF16), pltpu.VMEM((C_WIDTH, D_MODEL), BF16),
                        pltpu.VMEM((C_GROUPS, CHUNK, CHUNK), BF16), pltpu.VMEM((GM_TM, C_WIDTH), BF16)],
        compiler_params=_params(1),
        name="gmlp",
    )(x, mod_all, w["norm_mix_g"], w["w_in"], w["vnorm_g"], w["w_s"], w["b_s"], w["w_out"], *ffn_w)
    return outs[0], outs[1:]


def _swap_halves(t):
    return jnp.concatenate([t[..., HALF:], t[..., :HALF]], axis=-1)


def _head_ones():
    lane = np.arange(2 * LANES)
    m = (lane[:, None] // LANES == lane[None, :] // LANES) & (lane[:, None] % LANES < QK_DIM)
    return jnp.asarray(m.astype(np.float32), dtype=BF16)


def _layer0_weights(e, norm_mix_g, ab_w_in, pool_w, pool_scale, q_norm_g, kv_norm_g, w_uq, w_ukv,
                    qn_g, kn_g, ab_w_out):
    w_in = ab_w_in[e]
    o3 = POOL_WIDTH + Q_RANK + KV_RANK
    kr_cols = jnp.concatenate(
        [jnp.zeros((D_MODEL, QK_NOPE), F32), w_in[:, o3:], w_in[:, o3:]], axis=1)
    w_in_p = jnp.concatenate([w_in[:, :o3], kr_cols], axis=1).astype(BF16)
    uq = w_uq[e].reshape(Q_RANK, MLA_HEADS, QK_DIM)
    uq = jnp.concatenate([uq, _swap_halves(uq[:, :, QK_NOPE:])], axis=-1)
    score_scale = QK_DIM ** -0.5 * math.log2(math.e)
    gq = qn_g[e] * score_scale
    gk = kn_g[e]
    ukv = w_ukv[e].reshape(KV_RANK, MLA_HEADS, QK_NOPE + V_DIM)
    w_k = jnp.pad(ukv[:, :, :QK_NOPE], ((0, 0), (0, 0), (0, LANES - QK_NOPE)))
    w_v = ukv[:, :, QK_NOPE:].reshape(KV_RANK, MLA_HEADS // 2, 2, V_DIM)
    zero = jnp.zeros_like(w_v[:, :, 0])
    w_v = jnp.stack([jnp.concatenate([w_v[:, :, 0], zero], axis=-1),
                     jnp.concatenate([zero, w_v[:, :, 1]], axis=-1)], axis=2)
    return {
        "norm_mix_g": norm_mix_g.reshape(1, D_MODEL),
        "w_in": w_in_p,
        "q_norm_g": q_norm_g[e].reshape(1, Q_RANK),
        "kv_norm_g": kv_norm_g[e].reshape(1, KV_RANK),
        "w_uq": uq.reshape(Q_RANK, HEAD_W).astype(BF16),
        "w_k": w_k.reshape(KV_RANK, HEAD_W).astype(BF16),
        "w_vt": w_v.reshape(KV_RANK, HEAD_W).T.astype(BF16),
        "qn_g": jnp.pad(gq, (0, LANES - QK_DIM)).reshape(1, LANES),
        "qn_g_rope": jnp.concatenate([gq, _swap_halves(gq[QK_NOPE:])]).reshape(1, LANES),
        "kn_g": jnp.concatenate([gk, gk[QK_NOPE:]]).reshape(1, LANES),
        "head_ones": _head_ones(),
        "pool_w": pool_w[e].astype(BF16),
        "pool_scale": pool_scale[e].reshape(1, POOL_WIDTH),
        "w_out_pool": ab_w_out[e, :POOL_WIDTH].astype(BF16),
        "w_out_att": ab_w_out[e, POOL_WIDTH:].astype(BF16),
    }


def _rope_tables(seq_len):
    rows = seq_len // GRID_W
    row = np.repeat(np.arange(rows), GRID_W).astype(np.float32)
    col = np.tile(np.arange(GRID_W), rows).astype(np.float32)
    per_axis = QK_ROPE // 2
    inv = (1.0 / (np.float32(ROPE_BASE) ** (np.arange(0, per_axis, 2, dtype=np.float32) / per_axis))
           ).astype(np.float32)
    ang = np.concatenate([row[:, None] * inv, col[:, None] * inv], axis=-1)
    cos, sin = np.cos(ang).astype(np.float32), np.sin(ang).astype(np.float32)
    ones = np.ones((seq_len, QK_NOPE), np.float32)
    z_nope = np.zeros((seq_len, QK_NOPE), np.float32)
    z_half = np.zeros((seq_len, HALF), np.float32)
    c_tab = np.concatenate([ones, cos, cos, cos, cos], axis=-1)
    a_tab = np.concatenate([z_nope, -sin, z_half, -sin, z_half], axis=-1)
    b_tab = np.concatenate([z_nope, z_half, sin, z_half, sin], axis=-1)
    q_tab = np.concatenate([ones, cos, cos, -sin, sin], axis=-1)
    return jnp.asarray(np.stack([c_tab, a_tab, b_tab, q_tab]))


def kernel(x_prompt, x_sample, cache_ckv, cache_krope, c, c_ctx, ada_w, ada_b, norm_mix_g, norm_ffn_g, ffn_wg, ffn_wu, ffn_wd, ab_w_in, pool_w, pool_scale, q_norm_g, kv_norm_g, w_uq, w_ukv, qn_g, kn_g, ab_w_out, gm_w_in, gm_vnorm_g, gm_ws, gm_bs, gm_w_out):
    n_ctx_b, ctx_len, _ = x_prompt.shape
    n_lat_b, lat_len, _ = x_sample.shape
    past = cache_ckv.shape[2]
    assert ctx_len & (ctx_len - 1) == 0 and lat_len & (lat_len - 1) == 0
    assert MOD_ROWS >= 1 + n_lat_b
    assert DEPTH == 2
    n_ctx, n_lat = n_ctx_b * ctx_len, n_lat_b * lat_len

    ctx = x_prompt.reshape(n_ctx, D_MODEL)
    lat = x_sample.reshape(n_lat, D_MODEL)

    cond = jnp.concatenate(
        [c_ctx[None, :], c, jnp.zeros((MOD_ROWS - 1 - n_lat_b, D_MODEL), F32)], axis=0)
    mod_all = _ada_call(cond, ada_w, ada_b)

    def all_row(tm):
        n_ctx_tiles, per = n_ctx // tm, lat_len // tm
        return lambda j: jnp.where(j < n_ctx_tiles, 0, 1 + (j - n_ctx_tiles) // per)

    w = _layer0_weights(0, norm_mix_g[0], ab_w_in, pool_w, pool_scale, q_norm_g, kv_norm_g,
                        w_uq, w_ukv, qn_g, kn_g, ab_w_out)
    cs = _rope_tables(lat_len)
    yp_c, q_c, k_c, v_c, ckv_c, kr_c = _l0a_call(ctx, mod_all, 0, lambda t: 0, 1024, ctx_len, w, None)
    yp_l, q_l, k_l, v_l, _, _ = _l0a_call(lat, mod_all, 0, lambda t: 1 + t, lat_len, lat_len, w, cs)
    kr_c2 = cache_krope[:, 0].reshape(n_lat_b * past, QK_ROPE)
    kr_pad = jnp.concatenate([jnp.zeros((n_lat_b * past, QK_NOPE), F32), kr_c2, kr_c2], axis=1)
    cache = _kvc_call(cache_ckv[:, 0].reshape(n_lat_b * past, KV_RANK), kr_pad, w, 1024)
    ffn_w = (ffn_wg, ffn_wu, ffn_wd)
    xs, ffn0_w = _att_call((ctx, yp_c, q_c, k_c, v_c), (lat, yp_l, q_l, k_l, v_l), cache,
                           mod_all, 0, ctx_len, lat_len, w, ffn_w)
    xs = _ffn_call(xs, mod_all, 0, all_row(FFN_TM), norm_ffn_g[0].reshape(1, D_MODEL), *ffn0_w)

    gw = {
        "norm_mix_g": norm_mix_g[1].reshape(1, D_MODEL),
        "w_in": gm_w_in,
        "vnorm_g": gm_vnorm_g[0].reshape(1, C_WIDTH),
        "w_s": gm_ws,
        "b_s": jnp.broadcast_to(gm_bs[0][:, :, None], (C_GROUPS, CHUNK, C_CH)),
        "w_out": gm_w_out,
    }
    xs, ffn1_w = _gmlp_call(xs, mod_all, 1, all_row(GM_TM), gw, 0, ffn_w)
    y_ctx, y_lat = _ffn_call(xs, mod_all, 1, all_row(FFN_TM), norm_ffn_g[1].reshape(1, D_MODEL),
                             *ffn1_w, n_first=n_ctx)

    state_ckv = ckv_c.reshape(n_ctx_b, 1, ctx_len, KV_RANK)
    state_krope = kr_c.reshape(n_ctx_b, 1, ctx_len, QK_ROPE)
    return (y_ctx.reshape(n_ctx_b, ctx_len, D_MODEL), y_lat.reshape(n_lat_b, lat_len, D_MODEL),
            state_ckv, state_krope)
```

```python
import functools
import math

import numpy as np
import jax
import jax.numpy as jnp
from jax import lax
from jax.experimental import pallas as pl
from jax.experimental.pallas import tpu as pltpu

D_MODEL = 1024
DEPTH = 2
GRID_W = 64
POOL_WINDOWS = (2, 4, 8, 16)
POOL_GROUPS = 4
POOL_CH = 128
POOL_WIDTH = POOL_GROUPS * POOL_CH
MLA_HEADS = 8
QK_NOPE = 64
QK_ROPE = 32
QK_DIM = QK_NOPE + QK_ROPE
HALF = QK_ROPE // 2
SQRT_QK = math.sqrt(QK_DIM)
V_DIM = 64
Q_RANK = 384
KV_RANK = 256
MLA_WIDTH = MLA_HEADS * V_DIM
CHUNK = 128
C_GROUPS = 8
C_WIDTH = D_MODEL
C_CH = C_WIDTH // C_GROUPS
D_FF = 2816
ROPE_BASE = 10000.0
EPS = 1e-6

LANES = 128
HEAD_W = MLA_HEADS * LANES
AB_IN_PAD = POOL_WIDTH + Q_RANK + KV_RANK + LANES
MOD_ROWS = 16
VMEM_LIMIT = 58 * 1024 * 1024
W_STEPS = 8
Q_SS_MXU_PAIRS = 2

F32 = jnp.float32
BF16 = jnp.bfloat16


def _rms(x):
    return x * lax.rsqrt(jnp.mean(x * x, axis=-1, keepdims=True) + EPS)


def _dot(a, b):
    return jnp.dot(a, b, preferred_element_type=F32)


def _dot_nt(a, b):
    return lax.dot_general(a, b, (((1,), (1,)), ((), ())), preferred_element_type=F32)


def _dot_tn(a, b):
    return lax.dot_general(a, b, (((0,), (0,)), ((), ())), preferred_element_type=F32)


def _silu(x):
    return x * jax.nn.sigmoid(x)


def _params(n_axes):
    return pltpu.CompilerParams(
        dimension_semantics=("arbitrary",) * n_axes, vmem_limit_bytes=VMEM_LIMIT)


def _const_spec(shape):
    zeros = (0,) * len(shape)
    return pl.BlockSpec(shape, lambda *_: zeros)


def _mod_spec(layer):
    return pl.BlockSpec((1, 6, MOD_ROWS, D_MODEL), lambda *_: (layer, 0, 0, 0))


def _mod(mod_ref, term, row):
    return mod_ref[0, term, pl.ds(row, 1), :]


ADA_TERMS = 2


def _ada_kernel(cond_ref, w_ref, b_ref, o_ref):
    s = _silu(cond_ref[...]).astype(BF16)
    bias = b_ref[pl.ds(pl.program_id(0), 1), :]
    y = _dot(s, w_ref[0].astype(BF16)) + bias
    for k in range(ADA_TERMS):
        o_ref[0, k] = y[:, k * D_MODEL:(k + 1) * D_MODEL]


def _ada_call(cond, ada_w, ada_b):
    width = ADA_TERMS * D_MODEL
    return pl.pallas_call(
        _ada_kernel,
        grid=(DEPTH, 6 // ADA_TERMS),
        in_specs=[
            pl.BlockSpec((MOD_ROWS, D_MODEL), lambda i, j: (0, 0)),
            pl.BlockSpec((1, D_MODEL, width), lambda i, j: (i, 0, j)),
            pl.BlockSpec((DEPTH, width), lambda i, j: (0, j)),
        ],
        out_specs=pl.BlockSpec((1, ADA_TERMS, MOD_ROWS, D_MODEL), lambda i, j: (i, j, 0, 0)),
        out_shape=jax.ShapeDtypeStruct((DEPTH, 6, MOD_ROWS, D_MODEL), F32),
        compiler_params=_params(2),
        name="ada_mod",
    )(cond, ada_w, ada_b)


def _rope(t, cs_ref):
    return (t * cs_ref[0]
            + pltpu.roll(t, LANES - HALF, axis=1) * cs_ref[1]
            + pltpu.roll(t, HALF, axis=1) * cs_ref[2])


def _kv_matmuls(ckv_bf, wk_ref, wv_ref, v_ref):
    v_ref[...] = _dot_nt(wv_ref[...], ckv_bf).astype(BF16)
    return _dot(ckv_bf, wk_ref[...])


def _k_heads(kpre, kr, kng_ref, cs_ref, k_ref):
    kg = kng_ref[...] * SQRT_QK
    krg = kr * kg
    if cs_ref is not None:
        krg = _rope(krg, cs_ref)
    lane = lax.broadcasted_iota(jnp.int32, kr.shape, 1)
    kr_ss = jnp.sum(jnp.where(lane < QK_DIM, kr * kr, 0.0), axis=-1, keepdims=True) + QK_DIM * EPS
    for h in range(MLA_HEADS):
        kh = kpre[:, h * LANES:(h + 1) * LANES]
        r = lax.rsqrt(jnp.sum(kh * kh, axis=-1, keepdims=True) + kr_ss)
        k_ref[:, h * LANES:(h + 1) * LANES] = ((kh * kg + krg) * r).astype(BF16)


def _shift_rows(a, k, pos, seq_len):
    n = a.shape[0]
    r = pltpu.roll(a, k % n, axis=0)
    src = pos - k
    ok = (src >= 0) if k > 0 else (src < seq_len)
    return jnp.where(ok, r, 0.0)


def _pool_group(ug, half, pos, seq_len):
    fw = ug
    step = 1
    while step < half:
        fw = fw + _shift_rows(fw, -step, pos, seq_len)
        step *= 2
    bk = _shift_rows(ug, 1, pos, seq_len)
    step = 1
    while step < half:
        bk = bk + _shift_rows(bk, step, pos, seq_len)
        step *= 2
    cnt = jnp.minimum(pos + half, seq_len) - jnp.maximum(pos - half, 0)
    return (fw + bk) / cnt.astype(F32) - ug


def _l0a_kernel(*refs, seq_len, use_rope, mod_row):
    (x_ref, mod_ref, ng_ref, win_ref, qg_ref, kvg_ref, wuq_ref, wk_ref, wv_ref,
     qng_ref, kng_ref, pw_ref, ps_ref, ones_ref) = refs[:14]
    if use_rope:
        cs_ref = refs[14]
        outs = refs[15:]
    else:
        cs_ref = None
        outs = refs[14:]
    yp_ref, q_ref, k_ref, v_ref, ckv_ref, kr_ref = outs

    tm = x_ref.shape[0]
    row = mod_row(pl.program_id(0))
    scale = ng_ref[...] * (1.0 + _mod(mod_ref, 1, row))
    h = _rms(x_ref[...]) * scale + _mod(mod_ref, 0, row)
    z = _dot(h.astype(BF16), win_ref[...])
    o1, o2, o3 = POOL_WIDTH, POOL_WIDTH + Q_RANK, POOL_WIDTH + Q_RANK + KV_RANK

    ckv = _rms(z[:, o2:o3]) * kvg_ref[...]
    kr = z[:, o3:]
    ckv_ref[...] = ckv
    kr_ref[...] = kr[:, QK_NOPE:QK_DIM]
    cqn = (_rms(z[:, o1:o2]) * qg_ref[...]).astype(BF16)
    kpre = _kv_matmuls(ckv.astype(BF16), wk_ref, wv_ref, v_ref)
    qf = _dot(cqn, wuq_ref[...])

    qt = qng_ref[...] * SQRT_QK
    if use_rope:
        qt = qt * cs_ref[3]
    lane2 = lax.broadcasted_iota(jnp.int32, (tm, 2 * LANES), 1)
    for j in range(MLA_HEADS // 2):
        qp = qf[:, 2 * j * LANES:(2 * j + 2) * LANES]
        sq = qp * qp
        if j < Q_SS_MXU_PAIRS:
            hi = sq.astype(BF16)
            lo = (sq - hi.astype(F32)).astype(BF16)
            ss = _dot(hi, ones_ref[...]) + _dot(lo, ones_ref[...])
            qn = qp * lax.rsqrt(ss + QK_DIM * EPS)
            parts = [qn[:, :LANES], qn[:, LANES:]]
        else:
            sq = jnp.where((lane2 & (LANES - 1)) < QK_DIM, sq, 0.0)
            parts = []
            for hh in range(2):
                ss = jnp.sum(sq[:, hh * LANES:(hh + 1) * LANES], axis=-1, keepdims=True)
                parts.append(qp[:, hh * LANES:(hh + 1) * LANES] * lax.rsqrt(ss + QK_DIM * EPS))
        for hh in range(2):
            hd = 2 * j + hh
            q_ref[:, hd * LANES:(hd + 1) * LANES] = (parts[hh] * qt).astype(BF16)

    _k_heads(kpre, kr, kng_ref, cs_ref, k_ref)

    pos = lax.broadcasted_iota(jnp.int32, (tm, POOL_CH), 0) & (seq_len - 1)
    for g, w in enumerate(POOL_WINDOWS):
        ug = z[:, g * POOL_CH:(g + 1) * POOL_CH]
        p = _pool_group(ug, w // 2, pos, seq_len)
        y = _dot(p.astype(BF16), pw_ref[g]) * ps_ref[:, g * POOL_CH:(g + 1) * POOL_CH]
        yp_ref[:, g * POOL_CH:(g + 1) * POOL_CH] = y.astype(BF16)


def _l0a_call(x, mod_all, layer, mod_row, tm, seq_len, w, cs):
    n_tok = x.shape[0]
    use_rope = cs is not None
    row_spec = lambda width: pl.BlockSpec((tm, width), lambda t: (t, 0))
    in_specs = [
        row_spec(D_MODEL),
        _mod_spec(layer),
        _const_spec((1, D_MODEL)),
        _const_spec((D_MODEL, AB_IN_PAD)),
        _const_spec((1, Q_RANK)),
        _const_spec((1, KV_RANK)),
        _const_spec((Q_RANK, HEAD_W)),
        _const_spec((KV_RANK, HEAD_W)),
        _const_spec((HEAD_W, KV_RANK)),
        _const_spec((1, LANES)),
        _const_spec((1, LANES)),
        _const_spec((POOL_GROUPS, POOL_CH, POOL_CH)),
        _const_spec((1, POOL_WIDTH)),
        _const_spec((2 * LANES, 2 * LANES)),
    ]
    args = [x, mod_all, w["norm_mix_g"], w["w_in"], w["q_norm_g"], w["kv_norm_g"], w["w_uq"],
            w["w_k"], w["w_vt"], w["qn_g_rope"] if use_rope else w["qn_g"], w["kn_g"],
            w["pool_w"], w["pool_scale"], w["head_ones"]]
    if use_rope:
        assert tm == seq_len
        in_specs.append(_const_spec((4, seq_len, LANES)))
        args.append(cs)
    out_shape = [
        jax.ShapeDtypeStruct((n_tok, POOL_WIDTH), BF16),
        jax.ShapeDtypeStruct((n_tok, HEAD_W), BF16),
        jax.ShapeDtypeStruct((n_tok, HEAD_W), BF16),
        jax.ShapeDtypeStruct((HEAD_W, n_tok), BF16),
        jax.ShapeDtypeStruct((n_tok, KV_RANK), F32),
        jax.ShapeDtypeStruct((n_tok, QK_ROPE), F32),
    ]
    out_specs = [row_spec(POOL_WIDTH), row_spec(HEAD_W), row_spec(HEAD_W),
                 pl.BlockSpec((HEAD_W, tm), lambda t: (0, t)),
                 row_spec(KV_RANK), row_spec(QK_ROPE)]
    return pl.pallas_call(
        functools.partial(_l0a_kernel, seq_len=seq_len, use_rope=use_rope, mod_row=mod_row),
        grid=(n_tok // tm,),
        in_specs=in_specs,
        out_specs=out_specs,
        out_shape=out_shape,
        compiler_params=_params(1),
        name="l0_front_rope" if use_rope else "l0_front",
    )(*args)


def _kvc_kernel(ckv_ref, kr_ref, wk_ref, wv_ref, kng_ref, k_ref, v_ref):
    kpre = _kv_matmuls(ckv_ref[...].astype(BF16), wk_ref, wv_ref, v_ref)
    _k_heads(kpre, kr_ref[...], kng_ref, None, k_ref)


def _kvc_call(ckv, kr_pad, w, tm):
    n_tok = ckv.shape[0]
    row_spec = lambda width: pl.BlockSpec((tm, width), lambda t: (t, 0))
    return pl.pallas_call(
        _kvc_kernel,
        grid=(n_tok // tm,),
        in_specs=[row_spec(KV_RANK), row_spec(LANES), _const_spec((KV_RANK, HEAD_W)),
                  _const_spec((HEAD_W, KV_RANK)), _const_spec((1, LANES))],
        out_specs=[row_spec(HEAD_W), pl.BlockSpec((HEAD_W, tm), lambda t: (0, t))],
        out_shape=[jax.ShapeDtypeStruct((n_tok, HEAD_W), BF16),
                   jax.ShapeDtypeStruct((HEAD_W, n_tok), BF16)],
        compiler_params=_params(1),
        name="cache_kv",
    )(ckv, kr_pad, w["w_k"], w["w_vt"], w["kn_g"])


ATT_TQ = 256
ATT_UNITS = 2
ATT_KC = 512
ATT_AHEAD = 6


def _att_kernel(xc_ref, ypc_ref, qc_ref, kc_ref, vc_ref,
                xl_ref, ypl_ref, ql_ref, kl_ref, vl_ref, kp_ref, vp_ref,
                mod_ref, wop_ref, woa_ref, o_ref, ya_ref, *, n_ctx_steps, steps_per_batch):
    t = pl.program_id(0)

    def rows(u):
        return slice(u * ATT_TQ, (u + 1) * ATT_TQ)

    @pl.when(t < n_ctx_steps)
    def _():
        units = [(rows(u), [(kc_ref, vc_ref, rows(u))]) for u in range(ATT_UNITS)]
        _att_tile(units, xc_ref, ypc_ref, qc_ref, mod_ref, 0, wop_ref, woa_ref, o_ref, ya_ref)

    @pl.when(t >= n_ctx_steps)
    def _():
        srcs = [(kp_ref, vp_ref, slice(0, kp_ref.shape[0])), (kl_ref, vl_ref, slice(0, kl_ref.shape[0]))]
        units = [(rows(u), srcs) for u in range(ATT_UNITS)]
        row = 1 + (t - n_ctx_steps) // steps_per_batch
        _att_tile(units, xl_ref, ypl_ref, ql_ref, mod_ref, row, wop_ref, woa_ref, o_ref, ya_ref)


def _att_tile(units, x_ref, yp_ref, q_ref, mod_ref, row, wop_ref, woa_ref, o_ref, ya_ref):
    items = []
    for u, (qrows, sources) in enumerate(units):
        chunks = []
        for ks_ref, vs_ref, krange in sources:
            n_keys = krange.stop - krange.start
            kc = min(ATT_KC, n_keys)
            assert n_keys % kc == 0
            chunks += [(ks_ref, vs_ref, slice(krange.start + c * kc, krange.start + (c + 1) * kc))
                       for c in range(n_keys // kc)]
        items += [(u, qrows, hd, ch, ch is chunks[-1]) for hd in range(MLA_HEADS) for ch in chunks]

    def scores(item):
        _, qrows, hd, (ks_ref, _, keys), _ = item
        sl = slice(hd * LANES, (hd + 1) * LANES)
        s = _dot_nt(ks_ref[keys, sl], q_ref[qrows, sl])
        return s, jnp.max(s, axis=0, keepdims=True)

    pending = [scores(it) for it in items[:ATT_AHEAD]]
    state = {}
    gate = _mod(mod_ref, 2, row)
    for n, (u, qrows, hd, (_, vs_ref, keys), last_chunk) in enumerate(items):
        if n + ATT_AHEAD < len(items):
            pending.append(scores(items[n + ATT_AHEAD]))
        s, cmax = pending.pop(0)
        sl = slice(hd * LANES, (hd + 1) * LANES)
        if (u, hd) not in state:
            mx = cmax
            p = jnp.exp2(s - mx)
            den = jnp.sum(p, axis=0, keepdims=True)
            acc = _dot(vs_ref[sl, keys], p.astype(BF16))
        else:
            mx, den, acc = state[u, hd]
            new = jnp.maximum(mx, cmax)
            alpha = jnp.exp2(mx - new)
            p = jnp.exp2(s - new)
            den = alpha * den + jnp.sum(p, axis=0, keepdims=True)
            acc = alpha * acc + _dot(vs_ref[sl, keys], p.astype(BF16))
            mx = new
        state[u, hd] = (mx, den, acc)
        if hd % 2 == 1 and last_chunk:
            j = hd // 2
            pair = state[u, hd - 1][2] / state[u, hd - 1][1] + acc / den
            ya_ref[j * LANES:(j + 1) * LANES, qrows] = pair.astype(BF16)
        if hd == MLA_HEADS - 1 and last_chunk:
            y = _dot(yp_ref[qrows, :], wop_ref[...]) + _dot_tn(ya_ref[:, qrows], woa_ref[...])
            o_ref[qrows, :] = x_ref[qrows, :] + gate * y


def _att_call(ctx_in, lat_in, cache, mod_all, layer, ctx_len, lat_len, w):
    assert ctx_len == ATT_TQ
    tb = ATT_TQ * ATT_UNITS
    n_ctx, n_lat = ctx_in[0].shape[0], lat_in[0].shape[0]
    assert n_ctx % tb == 0 and lat_len % tb == 0
    n_c = n_ctx // tb
    per_batch = lat_len // tb
    past = cache[0].shape[0] // (n_lat // lat_len)
    cstep = lambda t: jnp.minimum(t, n_c - 1)
    lstep = lambda t: jnp.maximum(t - n_c, 0)
    crow = lambda width: pl.BlockSpec((tb, width), lambda t: (cstep(t), 0))
    lrow = lambda width: pl.BlockSpec((tb, width), lambda t: (lstep(t), 0))
    lbat = lambda rows: pl.BlockSpec((rows, HEAD_W), lambda t: (lstep(t) // per_batch, 0))
    lbat_t = lambda cols: pl.BlockSpec((HEAD_W, cols), lambda t: (0, lstep(t) // per_batch))
    in_specs = [
        crow(D_MODEL), crow(POOL_WIDTH), crow(HEAD_W), crow(HEAD_W),
        pl.BlockSpec((HEAD_W, tb), lambda t: (0, cstep(t))),
        lrow(D_MODEL), lrow(POOL_WIDTH), lrow(HEAD_W), lbat(lat_len), lbat_t(lat_len),
        lbat(past), lbat_t(past),
        _mod_spec(layer), _const_spec((POOL_WIDTH, D_MODEL)), _const_spec((MLA_WIDTH, D_MODEL)),
    ]
    return pl.pallas_call(
        functools.partial(_att_kernel, n_ctx_steps=n_c, steps_per_batch=per_batch),
        grid=(n_c + n_lat // tb,),
        in_specs=in_specs,
        out_specs=pl.BlockSpec((tb, D_MODEL), lambda t: (t, 0)),
        out_shape=jax.ShapeDtypeStruct((n_ctx + n_lat, D_MODEL), F32),
        scratch_shapes=[pltpu.VMEM((MLA_WIDTH, tb), BF16)],
        compiler_params=_params(1),
        name="att_out",
    )(*ctx_in, *lat_in, *cache, mod_all, w["w_out_pool"], w["w_out_att"])


FFN_SUB = 256
FFN_GCH = D_MODEL // W_STEPS
FFN_DCH = D_FF // W_STEPS


def _ffn_kernel(x_ref, mod_ref, ng_ref, wg_ref, wu_ref, wd_ref, *rest, mod_row, split):
    outs, (wg_s, wu_s, wd_s) = rest[:-3], rest[-3:]
    t = pl.program_id(0)

    @pl.when(t < W_STEPS)
    def _():
        r = pl.multiple_of(t * FFN_GCH, FFN_GCH)
        wg_s[pl.ds(r, FFN_GCH), :] = wg_ref[0].astype(BF16)
        wu_s[pl.ds(r, FFN_GCH), :] = wu_ref[0].astype(BF16)
        r = pl.multiple_of(t * FFN_DCH, FFN_DCH)
        wd_s[pl.ds(r, FFN_DCH), :] = wd_ref[0].astype(BF16)

    def tile(o_ref):
        row = mod_row(t - W_STEPS)
        scale = ng_ref[...] * (1.0 + _mod(mod_ref, 4, row))
        shift = _mod(mod_ref, 3, row)
        gate = _mod(mod_ref, 5, row)
        for s in range(x_ref.shape[0] // FFN_SUB):
            rows = slice(s * FFN_SUB, (s + 1) * FFN_SUB)
            x = x_ref[rows, :]
            h = (_rms(x) * scale + shift).astype(BF16)
            g = _dot(h, wg_s[...])
            u = _dot(h, wu_s[...])
            d = _dot((_silu(g) * u).astype(BF16), wd_s[...])
            o_ref[rows, :] = x + gate * d

    if split is None:
        pl.when(t >= W_STEPS)(lambda: tile(outs[0]))
    else:
        pl.when((t >= W_STEPS) & (t < W_STEPS + split))(lambda: tile(outs[0]))
        pl.when(t >= W_STEPS + split)(lambda: tile(outs[1]))


def _ffn_call(x, mod_all, layer, mod_row, tm, ng, wg, wu, wd, n_first=None):
    n_tok = x.shape[0]
    n_tiles = n_tok // tm
    tok = lambda t: jnp.maximum(t - W_STEPS, 0)
    wstep = lambda t: jnp.minimum(t, W_STEPS - 1)
    if n_first is None:
        split = None
        out_specs = pl.BlockSpec((tm, D_MODEL), lambda t: (tok(t), 0))
        out_shape = jax.ShapeDtypeStruct((n_tok, D_MODEL), F32)
    else:
        split = n_first // tm
        out_specs = [
            pl.BlockSpec((tm, D_MODEL), lambda t: (jnp.minimum(tok(t), split - 1), 0)),
            pl.BlockSpec((tm, D_MODEL), lambda t: (jnp.maximum(tok(t) - split, 0), 0)),
        ]
        out_shape = [jax.ShapeDtypeStruct((n_first, D_MODEL), F32),
                     jax.ShapeDtypeStruct((n_tok - n_first, D_MODEL), F32)]
    return pl.pallas_call(
        functools.partial(_ffn_kernel, mod_row=mod_row, split=split),
        grid=(W_STEPS + n_tiles,),
        in_specs=[
            pl.BlockSpec((tm, D_MODEL), lambda t: (tok(t), 0)),
            _mod_spec(layer),
            _const_spec((1, D_MODEL)),
            pl.BlockSpec((1, FFN_GCH, D_FF), lambda t: (layer, wstep(t), 0)),
            pl.BlockSpec((1, FFN_GCH, D_FF), lambda t: (layer, wstep(t), 0)),
            pl.BlockSpec((1, FFN_DCH, D_MODEL), lambda t: (layer, wstep(t), 0)),
        ],
        out_specs=out_specs,
        out_shape=out_shape,
        scratch_shapes=[pltpu.VMEM((D_MODEL, D_FF), BF16), pltpu.VMEM((D_MODEL, D_FF), BF16),
                        pltpu.VMEM((D_FF, D_MODEL), BF16)],
        compiler_params=_params(1),
        name="ffn",
    )(x, mod_all, ng, wg, wu, wd)


GM_TM = 1024
GM_SUB = 256
GM_WCH = D_MODEL // W_STEPS


def _gmlp_kernel(x_ref, mod_ref, ng_ref, win_ref, vg_ref, ws_ref, bs_ref, wout_ref, o_ref,
                 win_s, wout_s, ws_s, gs_ref, *, mod_row):
    t = pl.program_id(0)

    @pl.when(t < W_STEPS)
    def _():
        r = pl.multiple_of(t * GM_WCH, GM_WCH)
        win_s[pl.ds(r, GM_WCH), :] = win_ref[0].astype(BF16)
        wout_s[pl.ds(r, GM_WCH), :] = wout_ref[0].astype(BF16)
        ws_s[t] = ws_ref[0, 0].astype(BF16)

    @pl.when(t >= W_STEPS)
    def _():
        row = mod_row(t - W_STEPS)
        scale = ng_ref[...] * (1.0 + _mod(mod_ref, 1, row))
        shift = _mod(mod_ref, 0, row)
        gate = _mod(mod_ref, 2, row)
        n_chunks = GM_SUB // CHUNK
        n_sub = x_ref.shape[0] // GM_SUB

        def project(s):
            x = x_ref[s * GM_SUB:(s + 1) * GM_SUB, :]
            h = (_rms(x) * scale + shift).astype(BF16)
            return x, _dot(h, win_s[...])

        ahead = project(0)
        for s in range(n_sub):
            rows = slice(s * GM_SUB, (s + 1) * GM_SUB)
            x, z = ahead
            if s + 1 < n_sub:
                ahead = project(s + 1)
            vn = (_rms(z[:, C_WIDTH:]) * vg_ref[...]).astype(BF16)
            for g in range(C_GROUPS):
                cols = slice(g * C_CH, (g + 1) * C_CH)
                rhs = jnp.concatenate(
                    [vn[n * CHUNK:(n + 1) * CHUNK, cols] for n in range(n_chunks)], axis=1)
                sp = _dot(ws_s[g], rhs)
                for n in range(n_chunks):
                    crow = slice(n * CHUNK, (n + 1) * CHUNK)
                    grow = slice(s * GM_SUB + n * CHUNK, s * GM_SUB + (n + 1) * CHUNK)
                    sn = sp[:, n * C_CH:(n + 1) * C_CH] + bs_ref[g]
                    gs_ref[grow, cols] = (z[crow, cols] * sn).astype(BF16)
            o_ref[rows, :] = x + gate * _dot(gs_ref[rows, :], wout_s[...])


def _gmlp_call(x, mod_all, layer, mod_row, w, o):
    n_tok = x.shape[0]
    tok = lambda t: jnp.maximum(t - W_STEPS, 0)
    wstep = lambda t: jnp.minimum(t, W_STEPS - 1)
    assert C_GROUPS == W_STEPS
    return pl.pallas_call(
        functools.partial(_gmlp_kernel, mod_row=mod_row),
        grid=(W_STEPS + n_tok // GM_TM,),
        in_specs=[
            pl.BlockSpec((GM_TM, D_MODEL), lambda t: (tok(t), 0)),
            _mod_spec(layer),
            _const_spec((1, D_MODEL)),
            pl.BlockSpec((1, GM_WCH, 2 * C_WIDTH), lambda t: (o, wstep(t), 0)),
            _const_spec((1, C_WIDTH)),
            pl.BlockSpec((1, 1, CHUNK, CHUNK), lambda t: (o, wstep(t), 0, 0)),
            _const_spec((C_GROUPS, CHUNK, C_CH)),
            pl.BlockSpec((1, GM_WCH, D_MODEL), lambda t: (o, wstep(t), 0)),
        ],
        out_specs=pl.BlockSpec((GM_TM, D_MODEL), lambda t: (tok(t), 0)),
        out_shape=jax.ShapeDtypeStruct((n_tok, D_MODEL), F32),
        scratch_shapes=[pltpu.VMEM((D_MODEL, 2 * C_WIDTH), BF16), pltpu.VMEM((C_WIDTH, D_MODEL), BF16),
                        pltpu.VMEM((C_GROUPS, CHUNK, CHUNK), BF16), pltpu.VMEM((GM_TM, C_WIDTH), BF16)],
        compiler_params=_params(1),
        name="gmlp",
    )(x, mod_all, w["norm_mix_g"], w["w_in"], w["vnorm_g"], w["w_s"], w["b_s"], w["w_out"])


def _swap_halves(t):
    return jnp.concatenate([t[..., HALF:], t[..., :HALF]], axis=-1)


def _head_ones():
    lane = np.arange(2 * LANES)
    m = (lane[:, None] // LANES == lane[None, :] // LANES) & (lane[:, None] % LANES < QK_DIM)
    return jnp.asarray(m.astype(np.float32), dtype=BF16)


def _layer0_weights(e, norm_mix_g, ab_w_in, pool_w, pool_scale, q_norm_g, kv_norm_g, w_uq, w_ukv,
                    qn_g, kn_g, ab_w_out):
    w_in = ab_w_in[e]
    o3 = POOL_WIDTH + Q_RANK + KV_RANK
    kr_cols = jnp.concatenate(
        [jnp.zeros((D_MODEL, QK_NOPE), F32), w_in[:, o3:], w_in[:, o3:]], axis=1)
    w_in_p = jnp.concatenate([w_in[:, :o3], kr_cols], axis=1).astype(BF16)
    uq = w_uq[e].reshape(Q_RANK, MLA_HEADS, QK_DIM)
    uq = jnp.concatenate([uq, _swap_halves(uq[:, :, QK_NOPE:])], axis=-1)
    score_scale = QK_DIM ** -0.5 * math.log2(math.e)
    gq = qn_g[e] * score_scale
    gk = kn_g[e]
    ukv = w_ukv[e].reshape(KV_RANK, MLA_HEADS, QK_NOPE + V_DIM)
    w_k = jnp.pad(ukv[:, :, :QK_NOPE], ((0, 0), (0, 0), (0, LANES - QK_NOPE)))
    w_v = ukv[:, :, QK_NOPE:].reshape(KV_RANK, MLA_HEADS // 2, 2, V_DIM)
    zero = jnp.zeros_like(w_v[:, :, 0])
    w_v = jnp.stack([jnp.concatenate([w_v[:, :, 0], zero], axis=-1),
                     jnp.concatenate([zero, w_v[:, :, 1]], axis=-1)], axis=2)
    return {
        "norm_mix_g": norm_mix_g.reshape(1, D_MODEL),
        "w_in": w_in_p,
        "q_norm_g": q_norm_g[e].reshape(1, Q_RANK),
        "kv_norm_g": kv_norm_g[e].reshape(1, KV_RANK),
        "w_uq": uq.reshape(Q_RANK, HEAD_W).astype(BF16),
        "w_k": w_k.reshape(KV_RANK, HEAD_W).astype(BF16),
        "w_vt": w_v.reshape(KV_RANK, HEAD_W).T.astype(BF16),
        "qn_g": jnp.pad(gq, (0, LANES - QK_DIM)).reshape(1, LANES),
        "qn_g_rope": jnp.concatenate([gq, _swap_halves(gq[QK_NOPE:])]).reshape(1, LANES),
        "kn_g": jnp.concatenate([gk, gk[QK_NOPE:]]).reshape(1, LANES),
        "head_ones": _head_ones(),
        "pool_w": pool_w[e].astype(BF16),
        "pool_scale": pool_scale[e].reshape(1, POOL_WIDTH),
        "w_out_pool": ab_w_out[e, :POOL_WIDTH].astype(BF16),
        "w_out_att": ab_w_out[e, POOL_WIDTH:].astype(BF16),
    }


def _rope_tables(seq_len):
    rows = seq_len // GRID_W
    row = np.repeat(np.arange(rows), GRID_W).astype(np.float32)
    col = np.tile(np.arange(GRID_W), rows).astype(np.float32)
    per_axis = QK_ROPE // 2
    inv = (1.0 / (np.float32(ROPE_BASE) ** (np.arange(0, per_axis, 2, dtype=np.float32) / per_axis))
           ).astype(np.float32)
    ang = np.concatenate([row[:, None] * inv, col[:, None] * inv], axis=-1)
    cos, sin = np.cos(ang).astype(np.float32), np.sin(ang).astype(np.float32)
    ones = np.ones((seq_len, QK_NOPE), np.float32)
    z_nope = np.zeros((seq_len, QK_NOPE), np.float32)
    z_half = np.zeros((seq_len, HALF), np.float32)
    c_tab = np.concatenate([ones, cos, cos, cos, cos], axis=-1)
    a_tab = np.concatenate([z_nope, -sin, z_half, -sin, z_half], axis=-1)
    b_tab = np.concatenate([z_nope, z_half, sin, z_half, sin], axis=-1)
    q_tab = np.concatenate([ones, cos, cos, -sin, sin], axis=-1)
    return jnp.asarray(np.stack([c_tab, a_tab, b_tab, q_tab]))


def kernel(x_prompt, x_sample, cache_ckv, cache_krope, c, c_ctx, ada_w, ada_b, norm_mix_g, norm_ffn_g, ffn_wg, ffn_wu, ffn_wd, ab_w_in, pool_w, pool_scale, q_norm_g, kv_norm_g, w_uq, w_ukv, qn_g, kn_g, ab_w_out, gm_w_in, gm_vnorm_g, gm_ws, gm_bs, gm_w_out):
    n_ctx_b, ctx_len, _ = x_prompt.shape
    n_lat_b, lat_len, _ = x_sample.shape
    past = cache_ckv.shape[2]
    assert ctx_len & (ctx_len - 1) == 0 and lat_len & (lat_len - 1) == 0
    assert MOD_ROWS >= 1 + n_lat_b
    assert DEPTH == 2
    n_ctx, n_lat = n_ctx_b * ctx_len, n_lat_b * lat_len

    ctx = x_prompt.reshape(n_ctx, D_MODEL)
    lat = x_sample.reshape(n_lat, D_MODEL)

    cond = jnp.concatenate(
        [c_ctx[None, :], c, jnp.zeros((MOD_ROWS - 1 - n_lat_b, D_MODEL), F32)], axis=0)
    mod_all = _ada_call(cond, ada_w, ada_b)

    def all_row(tm):
        n_ctx_tiles, per = n_ctx // tm, lat_len // tm
        return lambda j: jnp.where(j < n_ctx_tiles, 0, 1 + (j - n_ctx_tiles) // per)

    w = _layer0_weights(0, norm_mix_g[0], ab_w_in, pool_w, pool_scale, q_norm_g, kv_norm_g,
                        w_uq, w_ukv, qn_g, kn_g, ab_w_out)
    cs = _rope_tables(lat_len)
    yp_c, q_c, k_c, v_c, ckv_c, kr_c = _l0a_call(ctx, mod_all, 0, lambda t: 0, 1024, ctx_len, w, None)
    yp_l, q_l, k_l, v_l, _, _ = _l0a_call(lat, mod_all, 0, lambda t: 1 + t, lat_len, lat_len, w, cs)
    kr_c2 = cache_krope[:, 0].reshape(n_lat_b * past, QK_ROPE)
    kr_pad = jnp.concatenate([jnp.zeros((n_lat_b * past, QK_NOPE), F32), kr_c2, kr_c2], axis=1)
    cache = _kvc_call(cache_ckv[:, 0].reshape(n_lat_b * past, KV_RANK), kr_pad, w, 1024)
    xs = _att_call((ctx, yp_c, q_c, k_c, v_c), (lat, yp_l, q_l, k_l, v_l), cache,
                   mod_all, 0, ctx_len, lat_len, w)
    xs = _ffn_call(xs, mod_all, 0, all_row(1024), 1024, norm_ffn_g[0].reshape(1, D_MODEL),
                   ffn_wg, ffn_wu, ffn_wd)

    gw = {
        "norm_mix_g": norm_mix_g[1].reshape(1, D_MODEL),
        "w_in": gm_w_in,
        "vnorm_g": gm_vnorm_g[0].reshape(1, C_WIDTH),
        "w_s": gm_ws,
        "b_s": jnp.broadcast_to(gm_bs[0][:, :, None], (C_GROUPS, CHUNK, C_CH)),
        "w_out": gm_w_out,
    }
    xs = _gmlp_call(xs, mod_all, 1, all_row(GM_TM), gw, 0)
    y_ctx, y_lat = _ffn_call(xs, mod_all, 1, all_row(512), 512, norm_ffn_g[1].reshape(1, D_MODEL),
                             ffn_wg, ffn_wu, ffn_wd, n_first=n_ctx)

    state_ckv = ckv_c.reshape(n_ctx_b, 1, ctx_len, KV_RANK)
    state_krope = kr_c.reshape(n_ctx_b, 1, ctx_len, QK_ROPE)
    return (y_ctx.reshape(n_ctx_b, ctx_len, D_MODEL), y_lat.reshape(n_lat_b, lat_len, D_MODEL),
            state_ckv, state_krope)
```

```python
import functools
import math

import numpy as np
import jax
import jax.numpy as jnp
from jax import lax
from jax.experimental import pallas as pl
from jax.experimental.pallas import tpu as pltpu

D_MODEL = 1024
DEPTH = 2
GRID_W = 64
POOL_WINDOWS = (2, 4, 8, 16)
POOL_GROUPS = 4
POOL_CH = 128
POOL_WIDTH = POOL_GROUPS * POOL_CH
MLA_HEADS = 8
QK_NOPE = 64
QK_ROPE = 32
QK_DIM = QK_NOPE + QK_ROPE
HALF = QK_ROPE // 2
SQRT_QK = math.sqrt(QK_DIM)
V_DIM = 64
Q_RANK = 384
KV_RANK = 256
MLA_WIDTH = MLA_HEADS * V_DIM
CHUNK = 128
C_GROUPS = 8
C_WIDTH = D_MODEL
C_CH = C_WIDTH // C_GROUPS
D_FF = 2816
ROPE_BASE = 10000.0
EPS = 1e-6

LANES = 128
HEAD_W = MLA_HEADS * LANES
AB_IN_PAD = POOL_WIDTH + Q_RANK + KV_RANK + LANES
MOD_ROWS = 16
VMEM_LIMIT = 58 * 1024 * 1024
W_STEPS = 8
Q_SS_MXU_PAIRS = 2

F32 = jnp.float32
BF16 = jnp.bfloat16


def _rms(x):
    return x * lax.rsqrt(jnp.mean(x * x, axis=-1, keepdims=True) + EPS)


def _dot(a, b):
    return jnp.dot(a, b, preferred_element_type=F32)


def _dot_nt(a, b):
    return lax.dot_general(a, b, (((1,), (1,)), ((), ())), preferred_element_type=F32)


def _dot_tn(a, b):
    return lax.dot_general(a, b, (((0,), (0,)), ((), ())), preferred_element_type=F32)


def _silu(x):
    return x * jax.nn.sigmoid(x)


def _params(n_axes):
    return pltpu.CompilerParams(
        dimension_semantics=("arbitrary",) * n_axes, vmem_limit_bytes=VMEM_LIMIT)


def _const_spec(shape):
    zeros = (0,) * len(shape)
    return pl.BlockSpec(shape, lambda *_: zeros)


def _mod_spec(layer):
    return pl.BlockSpec((1, 6, MOD_ROWS, D_MODEL), lambda *_: (layer, 0, 0, 0))


def _mod(mod_ref, term, row):
    return mod_ref[0, term, pl.ds(row, 1), :]


ADA_TERMS = 2


def _ada_kernel(cond_ref, w_ref, b_ref, o_ref):
    s = _silu(cond_ref[...]).astype(BF16)
    bias = b_ref[pl.ds(pl.program_id(0), 1), :]
    y = _dot(s, w_ref[0].astype(BF16)) + bias
    for k in range(ADA_TERMS):
        o_ref[0, k] = y[:, k * D_MODEL:(k + 1) * D_MODEL]


def _ada_call(cond, ada_w, ada_b):
    width = ADA_TERMS * D_MODEL
    return pl.pallas_call(
        _ada_kernel,
        grid=(DEPTH, 6 // ADA_TERMS),
        in_specs=[
            pl.BlockSpec((MOD_ROWS, D_MODEL), lambda i, j: (0, 0)),
            pl.BlockSpec((1, D_MODEL, width), lambda i, j: (i, 0, j)),
            pl.BlockSpec((DEPTH, width), lambda i, j: (0, j)),
        ],
        out_specs=pl.BlockSpec((1, ADA_TERMS, MOD_ROWS, D_MODEL), lambda i, j: (i, j, 0, 0)),
        out_shape=jax.ShapeDtypeStruct((DEPTH, 6, MOD_ROWS, D_MODEL), F32),
        compiler_params=_params(2),
        name="ada_mod",
    )(cond, ada_w, ada_b)


def _rope(t, cs_ref):
    return (t * cs_ref[0]
            + pltpu.roll(t, LANES - HALF, axis=1) * cs_ref[1]
            + pltpu.roll(t, HALF, axis=1) * cs_ref[2])


def _kv_matmuls(ckv_bf, wk_ref, wv_ref, v_ref):
    v_ref[...] = _dot_nt(wv_ref[...], ckv_bf).astype(BF16)
    return _dot(ckv_bf, wk_ref[...])


def _k_heads(kpre, kr, kng_ref, cs_ref, k_ref):
    kg = kng_ref[...] * SQRT_QK
    krg = kr * kg
    if cs_ref is not None:
        krg = _rope(krg, cs_ref)
    lane = lax.broadcasted_iota(jnp.int32, kr.shape, 1)
    kr_ss = jnp.sum(jnp.where(lane < QK_DIM, kr * kr, 0.0), axis=-1, keepdims=True) + QK_DIM * EPS
    for h in range(MLA_HEADS):
        kh = kpre[:, h * LANES:(h + 1) * LANES]
        r = lax.rsqrt(jnp.sum(kh * kh, axis=-1, keepdims=True) + kr_ss)
        k_ref[:, h * LANES:(h + 1) * LANES] = ((kh * kg + krg) * r).astype(BF16)


def _shift_rows(a, k, pos, seq_len):
    n = a.shape[0]
    r = pltpu.roll(a, k % n, axis=0)
    src = pos - k
    ok = (src >= 0) if k > 0 else (src < seq_len)
    return jnp.where(ok, r, 0.0)


def _pool_group(ug, half, pos, seq_len):
    fw = ug
    step = 1
    while step < half:
        fw = fw + _shift_rows(fw, -step, pos, seq_len)
        step *= 2
    bk = _shift_rows(ug, 1, pos, seq_len)
    step = 1
    while step < half:
        bk = bk + _shift_rows(bk, step, pos, seq_len)
        step *= 2
    cnt = jnp.minimum(pos + half, seq_len) - jnp.maximum(pos - half, 0)
    return (fw + bk) / cnt.astype(F32) - ug


def _l0a_kernel(*refs, seq_len, use_rope, mod_row):
    (x_ref, mod_ref, ng_ref, win_ref, qg_ref, kvg_ref, wuq_ref, wk_ref, wv_ref,
     qng_ref, kng_ref, pw_ref, ps_ref, ones_ref) = refs[:14]
    if use_rope:
        cs_ref = refs[14]
        outs = refs[15:]
    else:
        cs_ref = None
        outs = refs[14:]
    yp_ref, q_ref, k_ref, v_ref, ckv_ref, kr_ref = outs

    tm = x_ref.shape[0]
    row = mod_row(pl.program_id(0))
    scale = ng_ref[...] * (1.0 + _mod(mod_ref, 1, row))
    h = _rms(x_ref[...]) * scale + _mod(mod_ref, 0, row)
    z = _dot(h.astype(BF16), win_ref[...])
    o1, o2, o3 = POOL_WIDTH, POOL_WIDTH + Q_RANK, POOL_WIDTH + Q_RANK + KV_RANK

    ckv = _rms(z[:, o2:o3]) * kvg_ref[...]
    kr = z[:, o3:]
    ckv_ref[...] = ckv
    kr_ref[...] = kr[:, QK_NOPE:QK_DIM]
    cqn = (_rms(z[:, o1:o2]) * qg_ref[...]).astype(BF16)
    kpre = _kv_matmuls(ckv.astype(BF16), wk_ref, wv_ref, v_ref)
    qf = _dot(cqn, wuq_ref[...])

    qt = qng_ref[...] * SQRT_QK
    if use_rope:
        qt = qt * cs_ref[3]
    lane2 = lax.broadcasted_iota(jnp.int32, (tm, 2 * LANES), 1)
    for j in range(MLA_HEADS // 2):
        qp = qf[:, 2 * j * LANES:(2 * j + 2) * LANES]
        sq = qp * qp
        if j < Q_SS_MXU_PAIRS:
            hi = sq.astype(BF16)
            lo = (sq - hi.astype(F32)).astype(BF16)
            ss = _dot(hi, ones_ref[...]) + _dot(lo, ones_ref[...])
            qn = qp * lax.rsqrt(ss + QK_DIM * EPS)
            parts = [qn[:, :LANES], qn[:, LANES:]]
        else:
            sq = jnp.where((lane2 & (LANES - 1)) < QK_DIM, sq, 0.0)
            parts = []
            for hh in range(2):
                ss = jnp.sum(sq[:, hh * LANES:(hh + 1) * LANES], axis=-1, keepdims=True)
                parts.append(qp[:, hh * LANES:(hh + 1) * LANES] * lax.rsqrt(ss + QK_DIM * EPS))
        for hh in range(2):
            hd = 2 * j + hh
            q_ref[:, hd * LANES:(hd + 1) * LANES] = (parts[hh] * qt).astype(BF16)

    _k_heads(kpre, kr, kng_ref, cs_ref, k_ref)

    pos = lax.broadcasted_iota(jnp.int32, (tm, POOL_CH), 0) & (seq_len - 1)
    for g, w in enumerate(POOL_WINDOWS):
        ug = z[:, g * POOL_CH:(g + 1) * POOL_CH]
        p = _pool_group(ug, w // 2, pos, seq_len)
        y = _dot(p.astype(BF16), pw_ref[g]) * ps_ref[:, g * POOL_CH:(g + 1) * POOL_CH]
        yp_ref[:, g * POOL_CH:(g + 1) * POOL_CH] = y.astype(BF16)


def _l0a_call(x, mod_all, layer, mod_row, tm, seq_len, w, cs):
    n_tok = x.shape[0]
    use_rope = cs is not None
    row_spec = lambda width: pl.BlockSpec((tm, width), lambda t: (t, 0))
    in_specs = [
        row_spec(D_MODEL),
        _mod_spec(layer),
        _const_spec((1, D_MODEL)),
        _const_spec((D_MODEL, AB_IN_PAD)),
        _const_spec((1, Q_RANK)),
        _const_spec((1, KV_RANK)),
        _const_spec((Q_RANK, HEAD_W)),
        _const_spec((KV_RANK, HEAD_W)),
        _const_spec((HEAD_W, KV_RANK)),
        _const_spec((1, LANES)),
        _const_spec((1, LANES)),
        _const_spec((POOL_GROUPS, POOL_CH, POOL_CH)),
        _const_spec((1, POOL_WIDTH)),
        _const_spec((2 * LANES, 2 * LANES)),
    ]
    args = [x, mod_all, w["norm_mix_g"], w["w_in"], w["q_norm_g"], w["kv_norm_g"], w["w_uq"],
            w["w_k"], w["w_vt"], w["qn_g_rope"] if use_rope else w["qn_g"], w["kn_g"],
            w["pool_w"], w["pool_scale"], w["head_ones"]]
    if use_rope:
        assert tm == seq_len
        in_specs.append(_const_spec((4, seq_len, LANES)))
        args.append(cs)
    out_shape = [
        jax.ShapeDtypeStruct((n_tok, POOL_WIDTH), BF16),
        jax.ShapeDtypeStruct((n_tok, HEAD_W), BF16),
        jax.ShapeDtypeStruct((n_tok, HEAD_W), BF16),
        jax.ShapeDtypeStruct((HEAD_W, n_tok), BF16),
        jax.ShapeDtypeStruct((n_tok, KV_RANK), F32),
        jax.ShapeDtypeStruct((n_tok, QK_ROPE), F32),
    ]
    out_specs = [row_spec(POOL_WIDTH), row_spec(HEAD_W), row_spec(HEAD_W),
                 pl.BlockSpec((HEAD_W, tm), lambda t: (0, t)),
                 row_spec(KV_RANK), row_spec(QK_ROPE)]
    return pl.pallas_call(
        functools.partial(_l0a_kernel, seq_len=seq_len, use_rope=use_rope, mod_row=mod_row),
        grid=(n_tok // tm,),
        in_specs=in_specs,
        out_specs=out_specs,
        out_shape=out_shape,
        compiler_params=_params(1),
        name="l0_front_rope" if use_rope else "l0_front",
    )(*args)


def _kvc_kernel(ckv_ref, kr_ref, wk_ref, wv_ref, kng_ref, k_ref, v_ref):
    kpre = _kv_matmuls(ckv_ref[...].astype(BF16), wk_ref, wv_ref, v_ref)
    _k_heads(kpre, kr_ref[...], kng_ref, None, k_ref)


def _kvc_call(ckv, kr_pad, w, tm):
    n_tok = ckv.shape[0]
    row_spec = lambda width: pl.BlockSpec((tm, width), lambda t: (t, 0))
    return pl.pallas_call(
        _kvc_kernel,
        grid=(n_tok // tm,),
        in_specs=[row_spec(KV_RANK), row_spec(LANES), _const_spec((KV_RANK, HEAD_W)),
                  _const_spec((HEAD_W, KV_RANK)), _const_spec((1, LANES))],
        out_specs=[row_spec(HEAD_W), pl.BlockSpec((HEAD_W, tm), lambda t: (0, t))],
        out_shape=[jax.ShapeDtypeStruct((n_tok, HEAD_W), BF16),
                   jax.ShapeDtypeStruct((HEAD_W, n_tok), BF16)],
        compiler_params=_params(1),
        name="cache_kv",
    )(ckv, kr_pad, w["w_k"], w["w_vt"], w["kn_g"])


ATT_TQ = 256
ATT_UNITS = 2
ATT_KC = 512
ATT_AHEAD = 8


def _att_kernel(xc_ref, ypc_ref, qc_ref, kc_ref, vc_ref,
                xl_ref, ypl_ref, ql_ref, kl_ref, vl_ref, kp_ref, vp_ref,
                mod_ref, wop_ref, woa_ref, o_ref, ya_ref, *, n_ctx_steps, steps_per_batch):
    t = pl.program_id(0)

    def rows(u):
        return slice(u * ATT_TQ, (u + 1) * ATT_TQ)

    @pl.when(t < n_ctx_steps)
    def _():
        units = [(rows(u), [(kc_ref, vc_ref, rows(u))]) for u in range(ATT_UNITS)]
        _att_tile(units, xc_ref, ypc_ref, qc_ref, mod_ref, 0, wop_ref, woa_ref, o_ref, ya_ref)

    @pl.when(t >= n_ctx_steps)
    def _():
        srcs = [(kp_ref, vp_ref, slice(0, kp_ref.shape[0])), (kl_ref, vl_ref, slice(0, kl_ref.shape[0]))]
        units = [(rows(u), srcs) for u in range(ATT_UNITS)]
        row = 1 + (t - n_ctx_steps) // steps_per_batch
        _att_tile(units, xl_ref, ypl_ref, ql_ref, mod_ref, row, wop_ref, woa_ref, o_ref, ya_ref)


def _att_tile(units, x_ref, yp_ref, q_ref, mod_ref, row, wop_ref, woa_ref, o_ref, ya_ref):
    items = []
    for u, (qrows, sources) in enumerate(units):
        chunks = []
        for ks_ref, vs_ref, krange in sources:
            n_keys = krange.stop - krange.start
            kc = min(ATT_KC, n_keys)
            assert n_keys % kc == 0
            chunks += [(ks_ref, vs_ref, slice(krange.start + c * kc, krange.start + (c + 1) * kc))
                       for c in range(n_keys // kc)]
        items += [(u, qrows, hd, ch, ch is chunks[-1]) for hd in range(MLA_HEADS) for ch in chunks]

    def scores(item):
        _, qrows, hd, (ks_ref, _, keys), _ = item
        sl = slice(hd * LANES, (hd + 1) * LANES)
        s = _dot_nt(ks_ref[keys, sl], q_ref[qrows, sl])
        return s, jnp.max(s, axis=0, keepdims=True)

    pending = [scores(it) for it in items[:ATT_AHEAD]]
    state = {}
    gate = _mod(mod_ref, 2, row)
    for n, (u, qrows, hd, (_, vs_ref, keys), last_chunk) in enumerate(items):
        if n + ATT_AHEAD < len(items):
            pending.append(scores(items[n + ATT_AHEAD]))
        s, cmax = pending.pop(0)
        sl = slice(hd * LANES, (hd + 1) * LANES)
        if (u, hd) not in state:
            mx = cmax
            p = jnp.exp2(s - mx)
            den = jnp.sum(p, axis=0, keepdims=True)
            acc = _dot(vs_ref[sl, keys], p.astype(BF16))
        else:
            mx, den, acc = state[u, hd]
            new = jnp.maximum(mx, cmax)
            alpha = jnp.exp2(mx - new)
            p = jnp.exp2(s - new)
            den = alpha * den + jnp.sum(p, axis=0, keepdims=True)
            acc = alpha * acc + _dot(vs_ref[sl, keys], p.astype(BF16))
            mx = new
        state[u, hd] = (mx, den, acc)
        if hd % 2 == 1 and last_chunk:
            j = hd // 2
            pair = state[u, hd - 1][2] / state[u, hd - 1][1] + acc / den
            ya_ref[j * LANES:(j + 1) * LANES, qrows] = pair.astype(BF16)
        if hd == MLA_HEADS - 1 and last_chunk:
            y = _dot(yp_ref[qrows, :], wop_ref[...]) + _dot_tn(ya_ref[:, qrows], woa_ref[...])
            o_ref[qrows, :] = x_ref[qrows, :] + gate * y


def _att_call(ctx_in, lat_in, cache, mod_all, layer, ctx_len, lat_len, w):
    assert ctx_len == ATT_TQ
    tb = ATT_TQ * ATT_UNITS
    n_ctx, n_lat = ctx_in[0].shape[0], lat_in[0].shape[0]
    assert n_ctx % tb == 0 and lat_len % tb == 0
    n_c = n_ctx // tb
    per_batch = lat_len // tb
    past = cache[0].shape[0] // (n_lat // lat_len)
    cstep = lambda t: jnp.minimum(t, n_c - 1)
    lstep = lambda t: jnp.maximum(t - n_c, 0)
    crow = lambda width: pl.BlockSpec((tb, width), lambda t: (cstep(t), 0))
    lrow = lambda width: pl.BlockSpec((tb, width), lambda t: (lstep(t), 0))
    lbat = lambda rows: pl.BlockSpec((rows, HEAD_W), lambda t: (lstep(t) // per_batch, 0))
    lbat_t = lambda cols: pl.BlockSpec((HEAD_W, cols), lambda t: (0, lstep(t) // per_batch))
    in_specs = [
        crow(D_MODEL), crow(POOL_WIDTH), crow(HEAD_W), crow(HEAD_W),
        pl.BlockSpec((HEAD_W, tb), lambda t: (0, cstep(t))),
        lrow(D_MODEL), lrow(POOL_WIDTH), lrow(HEAD_W), lbat(lat_len), lbat_t(lat_len),
        lbat(past), lbat_t(past),
        _mod_spec(layer), _const_spec((POOL_WIDTH, D_MODEL)), _const_spec((MLA_WIDTH, D_MODEL)),
    ]
    return pl.pallas_call(
        functools.partial(_att_kernel, n_ctx_steps=n_c, steps_per_batch=per_batch),
        grid=(n_c + n_lat // tb,),
        in_specs=in_specs,
        out_specs=pl.BlockSpec((tb, D_MODEL), lambda t: (t, 0)),
        out_shape=jax.ShapeDtypeStruct((n_ctx + n_lat, D_MODEL), F32),
        scratch_shapes=[pltpu.VMEM((MLA_WIDTH, tb), BF16)],
        compiler_params=_params(1),
        name="att_out",
    )(*ctx_in, *lat_in, *cache, mod_all, w["w_out_pool"], w["w_out_att"])


FFN_SUB = 256
FFN_GCH = D_MODEL // W_STEPS
FFN_DCH = D_FF // W_STEPS


def _ffn_kernel(x_ref, mod_ref, ng_ref, wg_ref, wu_ref, wd_ref, *rest, mod_row, split):
    outs, (wg_s, wu_s, wd_s) = rest[:-3], rest[-3:]
    t = pl.program_id(0)

    @pl.when(t < W_STEPS)
    def _():
        r = pl.multiple_of(t * FFN_GCH, FFN_GCH)
        wg_s[pl.ds(r, FFN_GCH), :] = wg_ref[0].astype(BF16)
        wu_s[pl.ds(r, FFN_GCH), :] = wu_ref[0].astype(BF16)
        r = pl.multiple_of(t * FFN_DCH, FFN_DCH)
        wd_s[pl.ds(r, FFN_DCH), :] = wd_ref[0].astype(BF16)

    def tile(o_ref):
        row = mod_row(t - W_STEPS)
        scale = ng_ref[...] * (1.0 + _mod(mod_ref, 4, row))
        shift = _mod(mod_ref, 3, row)
        gate = _mod(mod_ref, 5, row)
        for s in range(x_ref.shape[0] // FFN_SUB):
            rows = slice(s * FFN_SUB, (s + 1) * FFN_SUB)
            x = x_ref[rows, :]
            h = (_rms(x) * scale + shift).astype(BF16)
            g = _dot(h, wg_s[...])
            u = _dot(h, wu_s[...])
            d = _dot((_silu(g) * u).astype(BF16), wd_s[...])
            o_ref[rows, :] = x + gate * d

    if split is None:
        pl.when(t >= W_STEPS)(lambda: tile(outs[0]))
    else:
        pl.when((t >= W_STEPS) & (t < W_STEPS + split))(lambda: tile(outs[0]))
        pl.when(t >= W_STEPS + split)(lambda: tile(outs[1]))


def _ffn_call(x, mod_all, layer, mod_row, tm, ng, wg, wu, wd, n_first=None):
    n_tok = x.shape[0]
    n_tiles = n_tok // tm
    tok = lambda t: jnp.maximum(t - W_STEPS, 0)
    wstep = lambda t: jnp.minimum(t, W_STEPS - 1)
    if n_first is None:
        split = None
        out_specs = pl.BlockSpec((tm, D_MODEL), lambda t: (tok(t), 0))
        out_shape = jax.ShapeDtypeStruct((n_tok, D_MODEL), F32)
    else:
        split = n_first // tm
        out_specs = [
            pl.BlockSpec((tm, D_MODEL), lambda t: (jnp.minimum(tok(t), split - 1), 0)),
            pl.BlockSpec((tm, D_MODEL), lambda t: (jnp.maximum(tok(t) - split, 0), 0)),
        ]
        out_shape = [jax.ShapeDtypeStruct((n_first, D_MODEL), F32),
                     jax.ShapeDtypeStruct((n_tok - n_first, D_MODEL), F32)]
    return pl.pallas_call(
        functools.partial(_ffn_kernel, mod_row=mod_row, split=split),
        grid=(W_STEPS + n_tiles,),
        in_specs=[
            pl.BlockSpec((tm, D_MODEL), lambda t: (tok(t), 0)),
            _mod_spec(layer),
            _const_spec((1, D_MODEL)),
            pl.BlockSpec((1, FFN_GCH, D_FF), lambda t: (layer, wstep(t), 0)),
            pl.BlockSpec((1, FFN_GCH, D_FF), lambda t: (layer, wstep(t), 0)),
            pl.BlockSpec((1, FFN_DCH, D_MODEL), lambda t: (layer, wstep(t), 0)),
        ],
        out_specs=out_specs,
        out_shape=out_shape,
        scratch_shapes=[pltpu.VMEM((D_MODEL, D_FF), BF16), pltpu.VMEM((D_MODEL, D_FF), BF16),
                        pltpu.VMEM((D_FF, D_MODEL), BF16)],
        compiler_params=_params(1),
        name="ffn",
    )(x, mod_all, ng, wg, wu, wd)


GM_TM = 1024
GM_SUB = 256
GM_WCH = D_MODEL // W_STEPS


def _gmlp_kernel(x_ref, mod_ref, ng_ref, win_ref, vg_ref, ws_ref, bs_ref, wout_ref, o_ref,
                 win_s, wout_s, ws_s, gs_ref, *, mod_row):
    t = pl.program_id(0)

    @pl.when(t < W_STEPS)
    def _():
        r = pl.multiple_of(t * GM_WCH, GM_WCH)
        win_s[pl.ds(r, GM_WCH), :] = win_ref[0].astype(BF16)
        wout_s[pl.ds(r, GM_WCH), :] = wout_ref[0].astype(BF16)
        ws_s[t] = ws_ref[0, 0].astype(BF16)

    @pl.when(t >= W_STEPS)
    def _():
        row = mod_row(t - W_STEPS)
        scale = ng_ref[...] * (1.0 + _mod(mod_ref, 1, row))
        shift = _mod(mod_ref, 0, row)
        gate = _mod(mod_ref, 2, row)
        n_chunks = GM_SUB // CHUNK
        n_sub = x_ref.shape[0] // GM_SUB

        def project(s):
            x = x_ref[s * GM_SUB:(s + 1) * GM_SUB, :]
            h = (_rms(x) * scale + shift).astype(BF16)
            return x, _dot(h, win_s[...])

        ahead = project(0)
        for s in range(n_sub):
            rows = slice(s * GM_SUB, (s + 1) * GM_SUB)
            x, z = ahead
            if s + 1 < n_sub:
                ahead = project(s + 1)
            vn = (_rms(z[:, C_WIDTH:]) * vg_ref[...]).astype(BF16)
            for g in range(C_GROUPS):
                cols = slice(g * C_CH, (g + 1) * C_CH)
                rhs = jnp.concatenate(
                    [vn[n * CHUNK:(n + 1) * CHUNK, cols] for n in range(n_chunks)], axis=1)
                sp = _dot(ws_s[g], rhs)
                for n in range(n_chunks):
                    crow = slice(n * CHUNK, (n + 1) * CHUNK)
                    grow = slice(s * GM_SUB + n * CHUNK, s * GM_SUB + (n + 1) * CHUNK)
                    sn = sp[:, n * C_CH:(n + 1) * C_CH] + bs_ref[g]
                    gs_ref[grow, cols] = (z[crow, cols] * sn).astype(BF16)
            o_ref[rows, :] = x + gate * _dot(gs_ref[rows, :], wout_s[...])


def _gmlp_call(x, mod_all, layer, mod_row, w, o):
    n_tok = x.shape[0]
    tok = lambda t: jnp.maximum(t - W_STEPS, 0)
    wstep = lambda t: jnp.minimum(t, W_STEPS - 1)
    assert C_GROUPS == W_STEPS
    return pl.pallas_call(
        functools.partial(_gmlp_kernel, mod_row=mod_row),
        grid=(W_STEPS + n_tok // GM_TM,),
        in_specs=[
            pl.BlockSpec((GM_TM, D_MODEL), lambda t: (tok(t), 0)),
            _mod_spec(layer),
            _const_spec((1, D_MODEL)),
            pl.BlockSpec((1, GM_WCH, 2 * C_WIDTH), lambda t: (o, wstep(t), 0)),
            _const_spec((1, C_WIDTH)),
            pl.BlockSpec((1, 1, CHUNK, CHUNK), lambda t: (o, wstep(t), 0, 0)),
            _const_spec((C_GROUPS, CHUNK, C_CH)),
            pl.BlockSpec((1, GM_WCH, D_MODEL), lambda t: (o, wstep(t), 0)),
        ],
        out_specs=pl.BlockSpec((GM_TM, D_MODEL), lambda t: (tok(t), 0)),
        out_shape=jax.ShapeDtypeStruct((n_tok, D_MODEL), F32),
        scratch_shapes=[pltpu.VMEM((D_MODEL, 2 * C_WIDTH), BF16), pltpu.VMEM((C_WIDTH, D_MODEL), BF16),
                        pltpu.VMEM((C_GROUPS, CHUNK, CHUNK), BF16), pltpu.VMEM((GM_TM, C_WIDTH), BF16)],
        compiler_params=_params(1),
        name="gmlp",
    )(x, mod_all, w["norm_mix_g"], w["w_in"], w["vnorm_g"], w["w_s"], w["b_s"], w["w_out"])


def _swap_halves(t):
    return jnp.concatenate([t[..., HALF:], t[..., :HALF]], axis=-1)


def _head_ones():
    lane = np.arange(2 * LANES)
    m = (lane[:, None] // LANES == lane[None, :] // LANES) & (lane[:, None] % LANES < QK_DIM)
    return jnp.asarray(m.astype(np.float32), dtype=BF16)


def _layer0_weights(e, norm_mix_g, ab_w_in, pool_w, pool_scale, q_norm_g, kv_norm_g, w_uq, w_ukv,
                    qn_g, kn_g, ab_w_out):
    w_in = ab_w_in[e]
    o3 = POOL_WIDTH + Q_RANK + KV_RANK
    kr_cols = jnp.concatenate(
        [jnp.zeros((D_MODEL, QK_NOPE), F32), w_in[:, o3:], w_in[:, o3:]], axis=1)
    w_in_p = jnp.concatenate([w_in[:, :o3], kr_cols], axis=1).astype(BF16)
    uq = w_uq[e].reshape(Q_RANK, MLA_HEADS, QK_DIM)
    uq = jnp.concatenate([uq, _swap_halves(uq[:, :, QK_NOPE:])], axis=-1)
    score_scale = QK_DIM ** -0.5 * math.log2(math.e)
    gq = qn_g[e] * score_scale
    gk = kn_g[e]
    ukv = w_ukv[e].reshape(KV_RANK, MLA_HEADS, QK_NOPE + V_DIM)
    w_k = jnp.pad(ukv[:, :, :QK_NOPE], ((0, 0), (0, 0), (0, LANES - QK_NOPE)))
    w_v = ukv[:, :, QK_NOPE:].reshape(KV_RANK, MLA_HEADS // 2, 2, V_DIM)
    zero = jnp.zeros_like(w_v[:, :, 0])
    w_v = jnp.stack([jnp.concatenate([w_v[:, :, 0], zero], axis=-1),
                     jnp.concatenate([zero, w_v[:, :, 1]], axis=-1)], axis=2)
    return {
        "norm_mix_g": norm_mix_g.reshape(1, D_MODEL),
        "w_in": w_in_p,
        "q_norm_g": q_norm_g[e].reshape(1, Q_RANK),
        "kv_norm_g": kv_norm_g[e].reshape(1, KV_RANK),
        "w_uq": uq.reshape(Q_RANK, HEAD_W).astype(BF16),
        "w_k": w_k.reshape(KV_RANK, HEAD_W).astype(BF16),
        "w_vt": w_v.reshape(KV_RANK, HEAD_W).T.astype(BF16),
        "qn_g": jnp.pad(gq, (0, LANES - QK_DIM)).reshape(1, LANES),
        "qn_g_rope": jnp.concatenate([gq, _swap_halves(gq[QK_NOPE:])]).reshape(1, LANES),
        "kn_g": jnp.concatenate([gk, gk[QK_NOPE:]]).reshape(1, LANES),
        "head_ones": _head_ones(),
        "pool_w": pool_w[e].astype(BF16),
        "pool_scale": pool_scale[e].reshape(1, POOL_WIDTH),
        "w_out_pool": ab_w_out[e, :POOL_WIDTH].astype(BF16),
        "w_out_att": ab_w_out[e, POOL_WIDTH:].astype(BF16),
    }


def _rope_tables(seq_len):
    rows = seq_len // GRID_W
    row = np.repeat(np.arange(rows), GRID_W).astype(np.float32)
    col = np.tile(np.arange(GRID_W), rows).astype(np.float32)
    per_axis = QK_ROPE // 2
    inv = (1.0 / (np.float32(ROPE_BASE) ** (np.arange(0, per_axis, 2, dtype=np.float32) / per_axis))
           ).astype(np.float32)
    ang = np.concatenate([row[:, None] * inv, col[:, None] * inv], axis=-1)
    cos, sin = np.cos(ang).astype(np.float32), np.sin(ang).astype(np.float32)
    ones = np.ones((seq_len, QK_NOPE), np.float32)
    z_nope = np.zeros((seq_len, QK_NOPE), np.float32)
    z_half = np.zeros((seq_len, HALF), np.float32)
    c_tab = np.concatenate([ones, cos, cos, cos, cos], axis=-1)
    a_tab = np.concatenate([z_nope, -sin, z_half, -sin, z_half], axis=-1)
    b_tab = np.concatenate([z_nope, z_half, sin, z_half, sin], axis=-1)
    q_tab = np.concatenate([ones, cos, cos, -sin, sin], axis=-1)
    return jnp.asarray(np.stack([c_tab, a_tab, b_tab, q_tab]))


def kernel(x_prompt, x_sample, cache_ckv, cache_krope, c, c_ctx, ada_w, ada_b, norm_mix_g, norm_ffn_g, ffn_wg, ffn_wu, ffn_wd, ab_w_in, pool_w, pool_scale, q_norm_g, kv_norm_g, w_uq, w_ukv, qn_g, kn_g, ab_w_out, gm_w_in, gm_vnorm_g, gm_ws, gm_bs, gm_w_out):
    n_ctx_b, ctx_len, _ = x_prompt.shape
    n_lat_b, lat_len, _ = x_sample.shape
    past = cache_ckv.shape[2]
    assert ctx_len & (ctx_len - 1) == 0 and lat_len & (lat_len - 1) == 0
    assert MOD_ROWS >= 1 + n_lat_b
    assert DEPTH == 2
    n_ctx, n_lat = n_ctx_b * ctx_len, n_lat_b * lat_len

    ctx = x_prompt.reshape(n_ctx, D_MODEL)
    lat = x_sample.reshape(n_lat, D_MODEL)

    cond = jnp.concatenate(
        [c_ctx[None, :], c, jnp.zeros((MOD_ROWS - 1 - n_lat_b, D_MODEL), F32)], axis=0)
    mod_all = _ada_call(cond, ada_w, ada_b)

    def all_row(tm):
        n_ctx_tiles, per = n_ctx // tm, lat_len // tm
        return lambda j: jnp.where(j < n_ctx_tiles, 0, 1 + (j - n_ctx_tiles) // per)

    w = _layer0_weights(0, norm_mix_g[0], ab_w_in, pool_w, pool_scale, q_norm_g, kv_norm_g,
                        w_uq, w_ukv, qn_g, kn_g, ab_w_out)
    cs = _rope_tables(lat_len)
    yp_c, q_c, k_c, v_c, ckv_c, kr_c = _l0a_call(ctx, mod_all, 0, lambda t: 0, 1024, ctx_len, w, None)
    yp_l, q_l, k_l, v_l, _, _ = _l0a_call(lat, mod_all, 0, lambda t: 1 + t, lat_len, lat_len, w, cs)
    kr_c2 = cache_krope[:, 0].reshape(n_lat_b * past, QK_ROPE)
    kr_pad = jnp.concatenate([jnp.zeros((n_lat_b * past, QK_NOPE), F32), kr_c2, kr_c2], axis=1)
    cache = _kvc_call(cache_ckv[:, 0].reshape(n_lat_b * past, KV_RANK), kr_pad, w, 1024)
    xs = _att_call((ctx, yp_c, q_c, k_c, v_c), (lat, yp_l, q_l, k_l, v_l), cache,
                   mod_all, 0, ctx_len, lat_len, w)
    xs = _ffn_call(xs, mod_all, 0, all_row(1024), 1024, norm_ffn_g[0].reshape(1, D_MODEL),
                   ffn_wg, ffn_wu, ffn_wd)

    gw = {
        "norm_mix_g": norm_mix_g[1].reshape(1, D_MODEL),
        "w_in": gm_w_in,
        "vnorm_g": gm_vnorm_g[0].reshape(1, C_WIDTH),
        "w_s": gm_ws,
        "b_s": jnp.broadcast_to(gm_bs[0][:, :, None], (C_GROUPS, CHUNK, C_CH)),
        "w_out": gm_w_out,
    }
    xs = _gmlp_call(xs, mod_all, 1, all_row(GM_TM), gw, 0)
    y_ctx, y_lat = _ffn_call(xs, mod_all, 1, all_row(512), 512, norm_ffn_g[1].reshape(1, D_MODEL),
                             ffn_wg, ffn_wu, ffn_wd, n_first=n_ctx)

    state_ckv = ckv_c.reshape(n_ctx_b, 1, ctx_len, KV_RANK)
    state_krope = kr_c.reshape(n_ctx_b, 1, ctx_len, QK_ROPE)
    return (y_ctx.reshape(n_ctx_b, ctx_len, D_MODEL), y_lat.reshape(n_lat_b, lat_len, D_MODEL),
            state_ckv, state_krope)
```

```python
import functools
import math

import numpy as np
import jax
import jax.numpy as jnp
from jax import lax
from jax.experimental import pallas as pl
from jax.experimental.pallas import tpu as pltpu

D_MODEL = 1024
DEPTH = 2
GRID_W = 64
POOL_WINDOWS = (2, 4, 8, 16)
POOL_GROUPS = 4
POOL_CH = 128
POOL_WIDTH = POOL_GROUPS * POOL_CH
MLA_HEADS = 8
QK_NOPE = 64
QK_ROPE = 32
QK_DIM = QK_NOPE + QK_ROPE
HALF = QK_ROPE // 2
SQRT_QK = math.sqrt(QK_DIM)
V_DIM = 64
Q_RANK = 384
KV_RANK = 256
MLA_WIDTH = MLA_HEADS * V_DIM
CHUNK = 128
C_GROUPS = 8
C_WIDTH = D_MODEL
C_CH = C_WIDTH // C_GROUPS
D_FF = 2816
ROPE_BASE = 10000.0
EPS = 1e-6

LANES = 128
HEAD_W = MLA_HEADS * LANES
AB_IN_PAD = POOL_WIDTH + Q_RANK + KV_RANK + LANES
MOD_ROWS = 16
VMEM_LIMIT = 60 * 1024 * 1024
W_STEPS = 8
Q_SS_MXU_PAIRS = 2

F32 = jnp.float32
BF16 = jnp.bfloat16


def _rms(x):
    return x * lax.rsqrt(jnp.mean(x * x, axis=-1, keepdims=True) + EPS)


def _dot(a, b):
    return jnp.dot(a, b, preferred_element_type=F32)


def _dot_nt(a, b):
    return lax.dot_general(a, b, (((1,), (1,)), ((), ())), preferred_element_type=F32)


def _dot_tn(a, b):
    return lax.dot_general(a, b, (((0,), (0,)), ((), ())), preferred_element_type=F32)


def _silu(x):
    return x * jax.nn.sigmoid(x)


def _params(n_axes):
    return pltpu.CompilerParams(
        dimension_semantics=("arbitrary",) * n_axes, vmem_limit_bytes=VMEM_LIMIT)


def _const_spec(shape):
    zeros = (0,) * len(shape)
    return pl.BlockSpec(shape, lambda *_: zeros)


def _mod_spec(layer):
    return pl.BlockSpec((1, 6, MOD_ROWS, D_MODEL), lambda *_: (layer, 0, 0, 0))


def _mod(mod_ref, term, row):
    return mod_ref[0, term, pl.ds(row, 1), :]


ADA_TERMS = 2


def _ada_kernel(cond_ref, w_ref, b_ref, o_ref):
    s = _silu(cond_ref[...]).astype(BF16)
    bias = b_ref[pl.ds(pl.program_id(0), 1), :]
    y = _dot(s, w_ref[0].astype(BF16)) + bias
    for k in range(ADA_TERMS):
        o_ref[0, k] = y[:, k * D_MODEL:(k + 1) * D_MODEL]


def _ada_call(cond, ada_w, ada_b):
    width = ADA_TERMS * D_MODEL
    return pl.pallas_call(
        _ada_kernel,
        grid=(DEPTH, 6 // ADA_TERMS),
        in_specs=[
            pl.BlockSpec((MOD_ROWS, D_MODEL), lambda i, j: (0, 0)),
            pl.BlockSpec((1, D_MODEL, width), lambda i, j: (i, 0, j)),
            pl.BlockSpec((DEPTH, width), lambda i, j: (0, j)),
        ],
        out_specs=pl.BlockSpec((1, ADA_TERMS, MOD_ROWS, D_MODEL), lambda i, j: (i, j, 0, 0)),
        out_shape=jax.ShapeDtypeStruct((DEPTH, 6, MOD_ROWS, D_MODEL), F32),
        compiler_params=_params(2),
        name="ada_mod",
    )(cond, ada_w, ada_b)


def _rope(t, cs_ref):
    return (t * cs_ref[0]
            + pltpu.roll(t, LANES - HALF, axis=1) * cs_ref[1]
            + pltpu.roll(t, HALF, axis=1) * cs_ref[2])


def _kv_matmuls(ckv_bf, wk_ref, wv_ref, v_ref):
    v_ref[...] = _dot_nt(wv_ref[...], ckv_bf).astype(BF16)
    return _dot(ckv_bf, wk_ref[...])


def _k_heads(kpre, kr, kng_ref, cs_ref, k_ref):
    kg = kng_ref[...] * SQRT_QK
    krg = kr * kg
    if cs_ref is not None:
        krg = _rope(krg, cs_ref)
    lane = lax.broadcasted_iota(jnp.int32, kr.shape, 1)
    kr_ss = jnp.sum(jnp.where(lane < QK_DIM, kr * kr, 0.0), axis=-1, keepdims=True) + QK_DIM * EPS
    for h in range(MLA_HEADS):
        kh = kpre[:, h * LANES:(h + 1) * LANES]
        r = lax.rsqrt(jnp.sum(kh * kh, axis=-1, keepdims=True) + kr_ss)
        k_ref[:, h * LANES:(h + 1) * LANES] = ((kh * kg + krg) * r).astype(BF16)


def _shift_rows(a, k, pos, seq_len):
    n = a.shape[0]
    r = pltpu.roll(a, k % n, axis=0)
    src = pos - k
    ok = (src >= 0) if k > 0 else (src < seq_len)
    return jnp.where(ok, r, 0.0)


def _pool_group(ug, half, pos, seq_len):
    fw = ug
    step = 1
    while step < half:
        fw = fw + _shift_rows(fw, -step, pos, seq_len)
        step *= 2
    bk = _shift_rows(ug, 1, pos, seq_len)
    step = 1
    while step < half:
        bk = bk + _shift_rows(bk, step, pos, seq_len)
        step *= 2
    cnt = jnp.minimum(pos + half, seq_len) - jnp.maximum(pos - half, 0)
    return (fw + bk) / cnt.astype(F32) - ug


def _l0a_kernel(*refs, seq_len, use_rope, mod_row):
    (x_ref, mod_ref, ng_ref, win_ref, qg_ref, kvg_ref, wuq_ref, wk_ref, wv_ref,
     qng_ref, kng_ref, pw_ref, ps_ref, ones_ref) = refs[:14]
    if use_rope:
        cs_ref = refs[14]
        outs = refs[15:]
    else:
        cs_ref = None
        outs = refs[14:]
    yp_ref, q_ref, k_ref, v_ref, ckv_ref, kr_ref = outs

    tm = x_ref.shape[0]
    row = mod_row(pl.program_id(0))
    scale = ng_ref[...] * (1.0 + _mod(mod_ref, 1, row))
    h = _rms(x_ref[...]) * scale + _mod(mod_ref, 0, row)
    z = _dot(h.astype(BF16), win_ref[...])
    o1, o2, o3 = POOL_WIDTH, POOL_WIDTH + Q_RANK, POOL_WIDTH + Q_RANK + KV_RANK

    ckv = _rms(z[:, o2:o3]) * kvg_ref[...]
    kr = z[:, o3:]
    ckv_ref[...] = ckv
    kr_ref[...] = kr[:, QK_NOPE:QK_DIM]
    cqn = (_rms(z[:, o1:o2]) * qg_ref[...]).astype(BF16)
    kpre = _kv_matmuls(ckv.astype(BF16), wk_ref, wv_ref, v_ref)
    qf = _dot(cqn, wuq_ref[...])

    qt = qng_ref[...] * SQRT_QK
    if use_rope:
        qt = qt * cs_ref[3]
    lane2 = lax.broadcasted_iota(jnp.int32, (tm, 2 * LANES), 1)
    for j in range(MLA_HEADS // 2):
        qp = qf[:, 2 * j * LANES:(2 * j + 2) * LANES]
        sq = qp * qp
        if j < Q_SS_MXU_PAIRS:
            hi = sq.astype(BF16)
            lo = (sq - hi.astype(F32)).astype(BF16)
            ss = _dot(hi, ones_ref[...]) + _dot(lo, ones_ref[...])
            qn = qp * lax.rsqrt(ss + QK_DIM * EPS)
            parts = [qn[:, :LANES], qn[:, LANES:]]
        else:
            sq = jnp.where((lane2 & (LANES - 1)) < QK_DIM, sq, 0.0)
            parts = []
            for hh in range(2):
                ss = jnp.sum(sq[:, hh * LANES:(hh + 1) * LANES], axis=-1, keepdims=True)
                parts.append(qp[:, hh * LANES:(hh + 1) * LANES] * lax.rsqrt(ss + QK_DIM * EPS))
        for hh in range(2):
            hd = 2 * j + hh
            q_ref[:, hd * LANES:(hd + 1) * LANES] = (parts[hh] * qt).astype(BF16)

    _k_heads(kpre, kr, kng_ref, cs_ref, k_ref)

    pos = lax.broadcasted_iota(jnp.int32, (tm, POOL_CH), 0) & (seq_len - 1)
    for g, w in enumerate(POOL_WINDOWS):
        ug = z[:, g * POOL_CH:(g + 1) * POOL_CH]
        p = _pool_group(ug, w // 2, pos, seq_len)
        y = _dot(p.astype(BF16), pw_ref[g]) * ps_ref[:, g * POOL_CH:(g + 1) * POOL_CH]
        yp_ref[:, g * POOL_CH:(g + 1) * POOL_CH] = y.astype(BF16)


def _l0a_call(x, mod_all, layer, mod_row, tm, seq_len, w, cs):
    n_tok = x.shape[0]
    use_rope = cs is not None
    row_spec = lambda width: pl.BlockSpec((tm, width), lambda t: (t, 0))
    in_specs = [
        row_spec(D_MODEL),
        _mod_spec(layer),
        _const_spec((1, D_MODEL)),
        _const_spec((D_MODEL, AB_IN_PAD)),
        _const_spec((1, Q_RANK)),
        _const_spec((1, KV_RANK)),
        _const_spec((Q_RANK, HEAD_W)),
        _const_spec((KV_RANK, HEAD_W)),
        _const_spec((MLA_WIDTH, KV_RANK)),
        _const_spec((1, LANES)),
        _const_spec((1, LANES)),
        _const_spec((POOL_GROUPS, POOL_CH, POOL_CH)),
        _const_spec((1, POOL_WIDTH)),
        _const_spec((2 * LANES, 2 * LANES)),
    ]
    args = [x, mod_all, w["norm_mix_g"], w["w_in"], w["q_norm_g"], w["kv_norm_g"], w["w_uq"],
            w["w_k"], w["w_vt"], w["qn_g_rope"] if use_rope else w["qn_g"], w["kn_g"],
            w["pool_w"], w["pool_scale"], w["head_ones"]]
    if use_rope:
        assert tm == seq_len
        in_specs.append(_const_spec((4, seq_len, LANES)))
        args.append(cs)
    out_shape = [
        jax.ShapeDtypeStruct((n_tok, POOL_WIDTH), BF16),
        jax.ShapeDtypeStruct((n_tok, HEAD_W), BF16),
        jax.ShapeDtypeStruct((n_tok, HEAD_W), BF16),
        jax.ShapeDtypeStruct((MLA_WIDTH, n_tok), BF16),
        jax.ShapeDtypeStruct((n_tok, KV_RANK), F32),
        jax.ShapeDtypeStruct((n_tok, QK_ROPE), F32),
    ]
    out_specs = [row_spec(POOL_WIDTH), row_spec(HEAD_W), row_spec(HEAD_W),
                 pl.BlockSpec((MLA_WIDTH, tm), lambda t: (0, t)),
                 row_spec(KV_RANK), row_spec(QK_ROPE)]
    return pl.pallas_call(
        functools.partial(_l0a_kernel, seq_len=seq_len, use_rope=use_rope, mod_row=mod_row),
        grid=(n_tok // tm,),
        in_specs=in_specs,
        out_specs=out_specs,
        out_shape=out_shape,
        compiler_params=_params(1),
        name="l0_front_rope" if use_rope else "l0_front",
    )(*args)


def _kvc_kernel(ckv_ref, kr_ref, wk_ref, wv_ref, kng_ref, k_ref, v_ref):
    kpre = _kv_matmuls(ckv_ref[...].astype(BF16), wk_ref, wv_ref, v_ref)
    _k_heads(kpre, kr_ref[...], kng_ref, None, k_ref)


def _kvc_call(ckv, kr_pad, w, tm):
    n_tok = ckv.shape[0]
    row_spec = lambda width: pl.BlockSpec((tm, width), lambda t: (t, 0))
    return pl.pallas_call(
        _kvc_kernel,
        grid=(n_tok // tm,),
        in_specs=[row_spec(KV_RANK), row_spec(LANES), _const_spec((KV_RANK, HEAD_W)),
                  _const_spec((MLA_WIDTH, KV_RANK)), _const_spec((1, LANES))],
        out_specs=[row_spec(HEAD_W), pl.BlockSpec((MLA_WIDTH, tm), lambda t: (0, t))],
        out_shape=[jax.ShapeDtypeStruct((n_tok, HEAD_W), BF16),
                   jax.ShapeDtypeStruct((MLA_WIDTH, n_tok), BF16)],
        compiler_params=_params(1),
        name="cache_kv",
    )(ckv, kr_pad, w["w_k"], w["w_vt"], w["kn_g"])


ATT_TQ = 256
ATT_UNITS = 4
ATT_KC = 512
ATT_AHEAD = 8


def _att_kernel(xc_ref, ypc_ref, qc_ref, kc_ref, vc_ref,
                xl_ref, ypl_ref, ql_ref, kl_ref, vl_ref, kp_ref, vp_ref,
                mod_ref, wop_ref, woa_ref, o_ref, ya_ref, *, n_ctx_steps, steps_per_batch):
    t = pl.program_id(0)

    def rows(u):
        return slice(u * ATT_TQ, (u + 1) * ATT_TQ)

    @pl.when(t < n_ctx_steps)
    def _():
        units = [(rows(u), [(kc_ref, vc_ref, rows(u))]) for u in range(ATT_UNITS)]
        _att_tile(units, xc_ref, ypc_ref, qc_ref, mod_ref, 0, wop_ref, woa_ref, o_ref, ya_ref)

    @pl.when(t >= n_ctx_steps)
    def _():
        srcs = [(kp_ref, vp_ref, slice(0, kp_ref.shape[0])), (kl_ref, vl_ref, slice(0, kl_ref.shape[0]))]
        units = [(rows(u), srcs) for u in range(ATT_UNITS)]
        row = 1 + (t - n_ctx_steps) // steps_per_batch
        _att_tile(units, xl_ref, ypl_ref, ql_ref, mod_ref, row, wop_ref, woa_ref, o_ref, ya_ref)


def _att_tile(units, x_ref, yp_ref, q_ref, mod_ref, row, wop_ref, woa_ref, o_ref, ya_ref):
    items = []
    for u, (qrows, sources) in enumerate(units):
        chunks = []
        for ks_ref, vs_ref, krange in sources:
            n_keys = krange.stop - krange.start
            kc = min(ATT_KC, n_keys)
            assert n_keys % kc == 0
            chunks += [(ks_ref, vs_ref, slice(krange.start + c * kc, krange.start + (c + 1) * kc))
                       for c in range(n_keys // kc)]
        items += [(u, qrows, hd, ch, ch is chunks[-1]) for hd in range(MLA_HEADS) for ch in chunks]

    def scores(item):
        _, qrows, hd, (ks_ref, _, keys), _ = item
        sl = slice(hd * LANES, (hd + 1) * LANES)
        s = _dot_nt(ks_ref[keys, sl], q_ref[qrows, sl])
        return s, jnp.max(s, axis=0, keepdims=True)

    pending = [scores(it) for it in items[:ATT_AHEAD]]
    state = {}
    gate = _mod(mod_ref, 2, row)
    low_half = lax.broadcasted_iota(jnp.int32, (LANES, ATT_TQ), 0) < V_DIM
    for n, (u, qrows, hd, (_, vs_ref, keys), last_chunk) in enumerate(items):
        if n + ATT_AHEAD < len(items):
            pending.append(scores(items[n + ATT_AHEAD]))
        s, cmax = pending.pop(0)
        vrows = slice((hd // 2) * LANES, (hd // 2 + 1) * LANES)
        if (u, hd) not in state:
            mx = cmax
            p = jnp.exp2(s - mx)
            den = jnp.sum(p, axis=0, keepdims=True)
            acc = _dot(vs_ref[vrows, keys], p.astype(BF16))
        else:
            mx, den, acc = state[u, hd]
            new = jnp.maximum(mx, cmax)
            alpha = jnp.exp2(mx - new)
            p = jnp.exp2(s - new)
            den = alpha * den + jnp.sum(p, axis=0, keepdims=True)
            acc = alpha * acc + _dot(vs_ref[vrows, keys], p.astype(BF16))
            mx = new
        state[u, hd] = (mx, den, acc)
        if hd % 2 == 1 and last_chunk:
            even = state[u, hd - 1][2] / state[u, hd - 1][1]
            pair = jnp.where(low_half, even, acc / den)
            ya_ref[vrows, qrows] = pair.astype(BF16)
        if hd == MLA_HEADS - 1 and last_chunk:
            y = _dot(yp_ref[qrows, :], wop_ref[...]) + _dot_tn(ya_ref[:, qrows], woa_ref[...])
            o_ref[qrows, :] = x_ref[qrows, :] + gate * y


def _att_call(ctx_in, lat_in, cache, mod_all, layer, ctx_len, lat_len, w):
    assert ctx_len == ATT_TQ
    tb = ATT_TQ * ATT_UNITS
    n_ctx, n_lat = ctx_in[0].shape[0], lat_in[0].shape[0]
    assert n_ctx % tb == 0 and lat_len % tb == 0
    n_c = n_ctx // tb
    per_batch = lat_len // tb
    past = cache[0].shape[0] // (n_lat // lat_len)
    cstep = lambda t: jnp.minimum(t, n_c - 1)
    lstep = lambda t: jnp.maximum(t - n_c, 0)
    crow = lambda width: pl.BlockSpec((tb, width), lambda t: (cstep(t), 0))
    lrow = lambda width: pl.BlockSpec((tb, width), lambda t: (lstep(t), 0))
    lbat = lambda rows: pl.BlockSpec((rows, HEAD_W), lambda t: (lstep(t) // per_batch, 0))
    lbat_t = lambda cols: pl.BlockSpec((MLA_WIDTH, cols), lambda t: (0, lstep(t) // per_batch))
    in_specs = [
        crow(D_MODEL), crow(POOL_WIDTH), crow(HEAD_W), crow(HEAD_W),
        pl.BlockSpec((MLA_WIDTH, tb), lambda t: (0, cstep(t))),
        lrow(D_MODEL), lrow(POOL_WIDTH), lrow(HEAD_W), lbat(lat_len), lbat_t(lat_len),
        lbat(past), lbat_t(past),
        _mod_spec(layer), _const_spec((POOL_WIDTH, D_MODEL)), _const_spec((MLA_WIDTH, D_MODEL)),
    ]
    return pl.pallas_call(
        functools.partial(_att_kernel, n_ctx_steps=n_c, steps_per_batch=per_batch),
        grid=(n_c + n_lat // tb,),
        in_specs=in_specs,
        out_specs=pl.BlockSpec((tb, D_MODEL), lambda t: (t, 0)),
        out_shape=jax.ShapeDtypeStruct((n_ctx + n_lat, D_MODEL), F32),
        scratch_shapes=[pltpu.VMEM((MLA_WIDTH, tb), BF16)],
        compiler_params=_params(1),
        name="att_out",
    )(*ctx_in, *lat_in, *cache, mod_all, w["w_out_pool"], w["w_out_att"])


FFN_SUB = 256
FFN_GCH = D_MODEL // W_STEPS
FFN_DCH = D_FF // W_STEPS


def _ffn_kernel(x_ref, mod_ref, ng_ref, wg_ref, wu_ref, wd_ref, *rest, mod_row, split):
    outs, (wg_s, wu_s, wd_s) = rest[:-3], rest[-3:]
    t = pl.program_id(0)

    @pl.when(t < W_STEPS)
    def _():
        r = pl.multiple_of(t * FFN_GCH, FFN_GCH)
        wg_s[pl.ds(r, FFN_GCH), :] = wg_ref[0].astype(BF16)
        wu_s[pl.ds(r, FFN_GCH), :] = wu_ref[0].astype(BF16)
        r = pl.multiple_of(t * FFN_DCH, FFN_DCH)
        wd_s[pl.ds(r, FFN_DCH), :] = wd_ref[0].astype(BF16)

    def tile(o_ref):
        row = mod_row(t - W_STEPS)
        scale = ng_ref[...] * (1.0 + _mod(mod_ref, 4, row))
        shift = _mod(mod_ref, 3, row)
        gate = _mod(mod_ref, 5, row)
        for s in range(x_ref.shape[0] // FFN_SUB):
            rows = slice(s * FFN_SUB, (s + 1) * FFN_SUB)
            x = x_ref[rows, :]
            h = (_rms(x) * scale + shift).astype(BF16)
            g = _dot(h, wg_s[...])
            u = _dot(h, wu_s[...])
            d = _dot((_silu(g) * u).astype(BF16), wd_s[...])
            o_ref[rows, :] = x + gate * d

    if split is None:
        pl.when(t >= W_STEPS)(lambda: tile(outs[0]))
    else:
        pl.when((t >= W_STEPS) & (t < W_STEPS + split))(lambda: tile(outs[0]))
        pl.when(t >= W_STEPS + split)(lambda: tile(outs[1]))


def _ffn_call(x, mod_all, layer, mod_row, tm, ng, wg, wu, wd, n_first=None):
    n_tok = x.shape[0]
    n_tiles = n_tok // tm
    tok = lambda t: jnp.maximum(t - W_STEPS, 0)
    wstep = lambda t: jnp.minimum(t, W_STEPS - 1)
    if n_first is None:
        split = None
        out_specs = pl.BlockSpec((tm, D_MODEL), lambda t: (tok(t), 0))
        out_shape = jax.ShapeDtypeStruct((n_tok, D_MODEL), F32)
    else:
        split = n_first // tm
        out_specs = [
            pl.BlockSpec((tm, D_MODEL), lambda t: (jnp.minimum(tok(t), split - 1), 0)),
            pl.BlockSpec((tm, D_MODEL), lambda t: (jnp.maximum(tok(t) - split, 0), 0)),
        ]
        out_shape = [jax.ShapeDtypeStruct((n_first, D_MODEL), F32),
                     jax.ShapeDtypeStruct((n_tok - n_first, D_MODEL), F32)]
    return pl.pallas_call(
        functools.partial(_ffn_kernel, mod_row=mod_row, split=split),
        grid=(W_STEPS + n_tiles,),
        in_specs=[
            pl.BlockSpec((tm, D_MODEL), lambda t: (tok(t), 0)),
            _mod_spec(layer),
            _const_spec((1, D_MODEL)),
            pl.BlockSpec((1, FFN_GCH, D_FF), lambda t: (layer, wstep(t), 0)),
            pl.BlockSpec((1, FFN_GCH, D_FF), lambda t: (layer, wstep(t), 0)),
            pl.BlockSpec((1, FFN_DCH, D_MODEL), lambda t: (layer, wstep(t), 0)),
        ],
        out_specs=out_specs,
        out_shape=out_shape,
        scratch_shapes=[pltpu.VMEM((D_MODEL, D_FF), BF16), pltpu.VMEM((D_MODEL, D_FF), BF16),
                        pltpu.VMEM((D_FF, D_MODEL), BF16)],
        compiler_params=_params(1),
        name="ffn",
    )(x, mod_all, ng, wg, wu, wd)


GM_TM = 1024
GM_SUB = 256
GM_WCH = D_MODEL // W_STEPS


def _gmlp_kernel(x_ref, mod_ref, ng_ref, win_ref, vg_ref, ws_ref, bs_ref, wout_ref, o_ref,
                 win_s, wout_s, ws_s, gs_ref, *, mod_row):
    t = pl.program_id(0)

    @pl.when(t < W_STEPS)
    def _():
        r = pl.multiple_of(t * GM_WCH, GM_WCH)
        win_s[pl.ds(r, GM_WCH), :] = win_ref[0].astype(BF16)
        wout_s[pl.ds(r, GM_WCH), :] = wout_ref[0].astype(BF16)
        ws_s[t] = ws_ref[0, 0].astype(BF16)

    @pl.when(t >= W_STEPS)
    def _():
        row = mod_row(t - W_STEPS)
        scale = ng_ref[...] * (1.0 + _mod(mod_ref, 1, row))
        shift = _mod(mod_ref, 0, row)
        gate = _mod(mod_ref, 2, row)
        n_chunks = GM_SUB // CHUNK
        n_sub = x_ref.shape[0] // GM_SUB

        def project(s):
            x = x_ref[s * GM_SUB:(s + 1) * GM_SUB, :]
            h = (_rms(x) * scale + shift).astype(BF16)
            return x, _dot(h, win_s[...])

        ahead = project(0)
        for s in range(n_sub):
            rows = slice(s * GM_SUB, (s + 1) * GM_SUB)
            x, z = ahead
            if s + 1 < n_sub:
                ahead = project(s + 1)
            vn = (_rms(z[:, C_WIDTH:]) * vg_ref[...]).astype(BF16)
            for g in range(C_GROUPS):
                cols = slice(g * C_CH, (g + 1) * C_CH)
                rhs = jnp.concatenate(
                    [vn[n * CHUNK:(n + 1) * CHUNK, cols] for n in range(n_chunks)], axis=1)
                sp = _dot(ws_s[g], rhs)
                for n in range(n_chunks):
                    crow = slice(n * CHUNK, (n + 1) * CHUNK)
                    grow = slice(s * GM_SUB + n * CHUNK, s * GM_SUB + (n + 1) * CHUNK)
                    sn = sp[:, n * C_CH:(n + 1) * C_CH] + bs_ref[g]
                    gs_ref[grow, cols] = (z[crow, cols] * sn).astype(BF16)
            o_ref[rows, :] = x + gate * _dot(gs_ref[rows, :], wout_s[...])


def _gmlp_call(x, mod_all, layer, mod_row, w, o):
    n_tok = x.shape[0]
    tok = lambda t: jnp.maximum(t - W_STEPS, 0)
    wstep = lambda t: jnp.minimum(t, W_STEPS - 1)
    assert C_GROUPS == W_STEPS
    return pl.pallas_call(
        functools.partial(_gmlp_kernel, mod_row=mod_row),
        grid=(W_STEPS + n_tok // GM_TM,),
        in_specs=[
            pl.BlockSpec((GM_TM, D_MODEL), lambda t: (tok(t), 0)),
            _mod_spec(layer),
            _const_spec((1, D_MODEL)),
            pl.BlockSpec((1, GM_WCH, 2 * C_WIDTH), lambda t: (o, wstep(t), 0)),
            _const_spec((1, C_WIDTH)),
            pl.BlockSpec((1, 1, CHUNK, CHUNK), lambda t: (o, wstep(t), 0, 0)),
            _const_spec((C_GROUPS, CHUNK, C_CH)),
            pl.BlockSpec((1, GM_WCH, D_MODEL), lambda t: (o, wstep(t), 0)),
        ],
        out_specs=pl.BlockSpec((GM_TM, D_MODEL), lambda t: (tok(t), 0)),
        out_shape=jax.ShapeDtypeStruct((n_tok, D_MODEL), F32),
        scratch_shapes=[pltpu.VMEM((D_MODEL, 2 * C_WIDTH), BF16), pltpu.VMEM((C_WIDTH, D_MODEL), BF16),
                        pltpu.VMEM((C_GROUPS, CHUNK, CHUNK), BF16), pltpu.VMEM((GM_TM, C_WIDTH), BF16)],
        compiler_params=_params(1),
        name="gmlp",
    )(x, mod_all, w["norm_mix_g"], w["w_in"], w["vnorm_g"], w["w_s"], w["b_s"], w["w_out"])


def _swap_halves(t):
    return jnp.concatenate([t[..., HALF:], t[..., :HALF]], axis=-1)


def _head_ones():
    lane = np.arange(2 * LANES)
    m = (lane[:, None] // LANES == lane[None, :] // LANES) & (lane[:, None] % LANES < QK_DIM)
    return jnp.asarray(m.astype(np.float32), dtype=BF16)


def _layer0_weights(e, norm_mix_g, ab_w_in, pool_w, pool_scale, q_norm_g, kv_norm_g, w_uq, w_ukv,
                    qn_g, kn_g, ab_w_out):
    w_in = ab_w_in[e]
    o3 = POOL_WIDTH + Q_RANK + KV_RANK
    kr_cols = jnp.concatenate(
        [jnp.zeros((D_MODEL, QK_NOPE), F32), w_in[:, o3:], w_in[:, o3:]], axis=1)
    w_in_p = jnp.concatenate([w_in[:, :o3], kr_cols], axis=1).astype(BF16)
    uq = w_uq[e].reshape(Q_RANK, MLA_HEADS, QK_DIM)
    uq = jnp.concatenate([uq, _swap_halves(uq[:, :, QK_NOPE:])], axis=-1)
    score_scale = QK_DIM ** -0.5 * math.log2(math.e)
    gq = qn_g[e] * score_scale
    gk = kn_g[e]
    ukv = w_ukv[e].reshape(KV_RANK, MLA_HEADS, QK_NOPE + V_DIM)
    w_k = jnp.pad(ukv[:, :, :QK_NOPE], ((0, 0), (0, 0), (0, LANES - QK_NOPE)))
    w_v = ukv[:, :, QK_NOPE:].reshape(KV_RANK, MLA_WIDTH)
    return {
        "norm_mix_g": norm_mix_g.reshape(1, D_MODEL),
        "w_in": w_in_p,
        "q_norm_g": q_norm_g[e].reshape(1, Q_RANK),
        "kv_norm_g": kv_norm_g[e].reshape(1, KV_RANK),
        "w_uq": uq.reshape(Q_RANK, HEAD_W).astype(BF16),
        "w_k": w_k.reshape(KV_RANK, HEAD_W).astype(BF16),
        "w_vt": w_v.T.astype(BF16),
        "qn_g": jnp.pad(gq, (0, LANES - QK_DIM)).reshape(1, LANES),
        "qn_g_rope": jnp.concatenate([gq, _swap_halves(gq[QK_NOPE:])]).reshape(1, LANES),
        "kn_g": jnp.concatenate([gk, gk[QK_NOPE:]]).reshape(1, LANES),
        "head_ones": _head_ones(),
        "pool_w": pool_w[e].astype(BF16),
        "pool_scale": pool_scale[e].reshape(1, POOL_WIDTH),
        "w_out_pool": ab_w_out[e, :POOL_WIDTH].astype(BF16),
        "w_out_att": ab_w_out[e, POOL_WIDTH:].astype(BF16),
    }


def _rope_tables(seq_len):
    rows = seq_len // GRID_W
    row = np.repeat(np.arange(rows), GRID_W).astype(np.float32)
    col = np.tile(np.arange(GRID_W), rows).astype(np.float32)
    per_axis = QK_ROPE // 2
    inv = (1.0 / (np.float32(ROPE_BASE) ** (np.arange(0, per_axis, 2, dtype=np.float32) / per_axis))
           ).astype(np.float32)
    ang = np.concatenate([row[:, None] * inv, col[:, None] * inv], axis=-1)
    cos, sin = np.cos(ang).astype(np.float32), np.sin(ang).astype(np.float32)
    ones = np.ones((seq_len, QK_NOPE), np.float32)
    z_nope = np.zeros((seq_len, QK_NOPE), np.float32)
    z_half = np.zeros((seq_len, HALF), np.float32)
    c_tab = np.concatenate([ones, cos, cos, cos, cos], axis=-1)
    a_tab = np.concatenate([z_nope, -sin, z_half, -sin, z_half], axis=-1)
    b_tab = np.concatenate([z_nope, z_half, sin, z_half, sin], axis=-1)
    q_tab = np.concatenate([ones, cos, cos, -sin, sin], axis=-1)
    return jnp.asarray(np.stack([c_tab, a_tab, b_tab, q_tab]))


def kernel(x_prompt, x_sample, cache_ckv, cache_krope, c, c_ctx, ada_w, ada_b, norm_mix_g, norm_ffn_g, ffn_wg, ffn_wu, ffn_wd, ab_w_in, pool_w, pool_scale, q_norm_g, kv_norm_g, w_uq, w_ukv, qn_g, kn_g, ab_w_out, gm_w_in, gm_vnorm_g, gm_ws, gm_bs, gm_w_out):
    n_ctx_b, ctx_len, _ = x_prompt.shape
    n_lat_b, lat_len, _ = x_sample.shape
    past = cache_ckv.shape[2]
    assert ctx_len & (ctx_len - 1) == 0 and lat_len & (lat_len - 1) == 0
    assert MOD_ROWS >= 1 + n_lat_b
    assert DEPTH == 2
    n_ctx, n_lat = n_ctx_b * ctx_len, n_lat_b * lat_len

    ctx = x_prompt.reshape(n_ctx, D_MODEL)
    lat = x_sample.reshape(n_lat, D_MODEL)

    cond = jnp.concatenate(
        [c_ctx[None, :], c, jnp.zeros((MOD_ROWS - 1 - n_lat_b, D_MODEL), F32)], axis=0)
    mod_all = _ada_call(cond, ada_w, ada_b)

    def all_row(tm):
        n_ctx_tiles, per = n_ctx // tm, lat_len // tm
        return lambda j: jnp.where(j < n_ctx_tiles, 0, 1 + (j - n_ctx_tiles) // per)

    w = _layer0_weights(0, norm_mix_g[0], ab_w_in, pool_w, pool_scale, q_norm_g, kv_norm_g,
                        w_uq, w_ukv, qn_g, kn_g, ab_w_out)
    cs = _rope_tables(lat_len)
    yp_c, q_c, k_c, v_c, ckv_c, kr_c = _l0a_call(ctx, mod_all, 0, lambda t: 0, 1024, ctx_len, w, None)
    yp_l, q_l, k_l, v_l, _, _ = _l0a_call(lat, mod_all, 0, lambda t: 1 + t, lat_len, lat_len, w, cs)
    kr_c2 = cache_krope[:, 0].reshape(n_lat_b * past, QK_ROPE)
    kr_pad = jnp.concatenate([jnp.zeros((n_lat_b * past, QK_NOPE), F32), kr_c2, kr_c2], axis=1)
    cache = _kvc_call(cache_ckv[:, 0].reshape(n_lat_b * past, KV_RANK), kr_pad, w, 1024)
    xs = _att_call((ctx, yp_c, q_c, k_c, v_c), (lat, yp_l, q_l, k_l, v_l), cache,
                   mod_all, 0, ctx_len, lat_len, w)
    xs = _ffn_call(xs, mod_all, 0, all_row(1024), 1024, norm_ffn_g[0].reshape(1, D_MODEL),
                   ffn_wg, ffn_wu, ffn_wd)

    gw = {
        "norm_mix_g": norm_mix_g[1].reshape(1, D_MODEL),
        "w_in": gm_w_in,
        "vnorm_g": gm_vnorm_g[0].reshape(1, C_WIDTH),
        "w_s": gm_ws,
        "b_s": jnp.broadcast_to(gm_bs[0][:, :, None], (C_GROUPS, CHUNK, C_CH)),
        "w_out": gm_w_out,
    }
    xs = _gmlp_call(xs, mod_all, 1, all_row(GM_TM), gw, 0)
    y_ctx, y_lat = _ffn_call(xs, mod_all, 1, all_row(512), 512, norm_ffn_g[1].reshape(1, D_MODEL),
                             ffn_wg, ffn_wu, ffn_wd, n_first=n_ctx)

    state_ckv = ckv_c.reshape(n_ctx_b, 1, ctx_len, KV_RANK)
    state_krope = kr_c.reshape(n_ctx_b, 1, ctx_len, QK_ROPE)
    return (y_ctx.reshape(n_ctx_b, ctx_len, D_MODEL), y_lat.reshape(n_lat_b, lat_len, D_MODEL),
            state_ckv, state_krope)
```

```python
import functools
import math

import numpy as np
import jax
import jax.numpy as jnp
from jax import lax
from jax.experimental import pallas as pl
from jax.experimental.pallas import tpu as pltpu

D_MODEL = 1024
DEPTH = 2
GRID_W = 64
POOL_WINDOWS = (2, 4, 8, 16)
POOL_GROUPS = 4
POOL_CH = 128
POOL_WIDTH = POOL_GROUPS * POOL_CH
MLA_HEADS = 8
QK_NOPE = 64
QK_ROPE = 32
QK_DIM = QK_NOPE + QK_ROPE
HALF = QK_ROPE // 2
SQRT_QK = math.sqrt(QK_DIM)
V_DIM = 64
Q_RANK = 384
KV_RANK = 256
MLA_WIDTH = MLA_HEADS * V_DIM
CHUNK = 128
C_GROUPS = 8
C_WIDTH = D_MODEL
C_CH = C_WIDTH // C_GROUPS
D_FF = 2816
ROPE_BASE = 10000.0
EPS = 1e-6

LANES = 128
HEAD_W = MLA_HEADS * LANES
AB_IN_PAD = POOL_WIDTH + Q_RANK + KV_RANK + LANES
MOD_ROWS = 16
VMEM_LIMIT = 60 * 1024 * 1024
W_STEPS = 8
Q_SS_MXU_PAIRS = 2

F32 = jnp.float32
BF16 = jnp.bfloat16


def _rms(x):
    return x * lax.rsqrt(jnp.mean(x * x, axis=-1, keepdims=True) + EPS)


def _dot(a, b):
    return jnp.dot(a, b, preferred_element_type=F32)


def _dot_nt(a, b):
    return lax.dot_general(a, b, (((1,), (1,)), ((), ())), preferred_element_type=F32)


def _dot_tn(a, b):
    return lax.dot_general(a, b, (((0,), (0,)), ((), ())), preferred_element_type=F32)


def _silu(x):
    return x * jax.nn.sigmoid(x)


def _params(n_axes):
    return pltpu.CompilerParams(
        dimension_semantics=("arbitrary",) * n_axes, vmem_limit_bytes=VMEM_LIMIT)


def _const_spec(shape):
    zeros = (0,) * len(shape)
    return pl.BlockSpec(shape, lambda *_: zeros)


def _mod_spec(layer):
    return pl.BlockSpec((1, 6, MOD_ROWS, D_MODEL), lambda *_: (layer, 0, 0, 0))


def _mod(mod_ref, term, row):
    return mod_ref[0, term, pl.ds(row, 1), :]


ADA_TERMS = 2


def _ada_kernel(cond_ref, w_ref, b_ref, o_ref):
    s = _silu(cond_ref[...]).astype(BF16)
    bias = b_ref[pl.ds(pl.program_id(0), 1), :]
    y = _dot(s, w_ref[0].astype(BF16)) + bias
    for k in range(ADA_TERMS):
        o_ref[0, k] = y[:, k * D_MODEL:(k + 1) * D_MODEL]


def _ada_call(cond, ada_w, ada_b):
    width = ADA_TERMS * D_MODEL
    return pl.pallas_call(
        _ada_kernel,
        grid=(DEPTH, 6 // ADA_TERMS),
        in_specs=[
            pl.BlockSpec((MOD_ROWS, D_MODEL), lambda i, j: (0, 0)),
            pl.BlockSpec((1, D_MODEL, width), lambda i, j: (i, 0, j)),
            pl.BlockSpec((DEPTH, width), lambda i, j: (0, j)),
        ],
        out_specs=pl.BlockSpec((1, ADA_TERMS, MOD_ROWS, D_MODEL), lambda i, j: (i, j, 0, 0)),
        out_shape=jax.ShapeDtypeStruct((DEPTH, 6, MOD_ROWS, D_MODEL), F32),
        compiler_params=_params(2),
        name="ada_mod",
    )(cond, ada_w, ada_b)


def _rope(t, cs_ref):
    return (t * cs_ref[0]
            + pltpu.roll(t, LANES - HALF, axis=1) * cs_ref[1]
            + pltpu.roll(t, HALF, axis=1) * cs_ref[2])


def _kv_matmuls(ckv_bf, wk_ref, wv_ref, v_ref):
    v_ref[...] = _dot_nt(wv_ref[...], ckv_bf).astype(BF16)
    return _dot(ckv_bf, wk_ref[...])


def _k_heads(kpre, kr, kng_ref, cs_ref, k_ref):
    kg = kng_ref[...] * SQRT_QK
    krg = kr * kg
    if cs_ref is not None:
        krg = _rope(krg, cs_ref)
    lane = lax.broadcasted_iota(jnp.int32, kr.shape, 1)
    kr_ss = jnp.sum(jnp.where(lane < QK_DIM, kr * kr, 0.0), axis=-1, keepdims=True) + QK_DIM * EPS
    for h in range(MLA_HEADS):
        kh = kpre[:, h * LANES:(h + 1) * LANES]
        r = lax.rsqrt(jnp.sum(kh * kh, axis=-1, keepdims=True) + kr_ss)
        k_ref[:, h * LANES:(h + 1) * LANES] = ((kh * kg + krg) * r).astype(BF16)


def _shift_rows(a, k, pos, seq_len):
    n = a.shape[0]
    r = pltpu.roll(a, k % n, axis=0)
    src = pos - k
    ok = (src >= 0) if k > 0 else (src < seq_len)
    return jnp.where(ok, r, 0.0)


def _pool_group(ug, half, pos, seq_len):
    fw = ug
    step = 1
    while step < half:
        fw = fw + _shift_rows(fw, -step, pos, seq_len)
        step *= 2
    bk = _shift_rows(ug, 1, pos, seq_len)
    step = 1
    while step < half:
        bk = bk + _shift_rows(bk, step, pos, seq_len)
        step *= 2
    cnt = jnp.minimum(pos + half, seq_len) - jnp.maximum(pos - half, 0)
    return (fw + bk) / cnt.astype(F32) - ug


def _l0a_kernel(*refs, seq_len, use_rope, mod_row):
    (x_ref, mod_ref, ng_ref, win_ref, qg_ref, kvg_ref, wuq_ref, wk_ref, wv_ref,
     qng_ref, kng_ref, pw_ref, ps_ref, ones_ref) = refs[:14]
    if use_rope:
        cs_ref = refs[14]
        outs = refs[15:]
    else:
        cs_ref = None
        outs = refs[14:]
    yp_ref, q_ref, k_ref, v_ref, ckv_ref, kr_ref = outs

    tm = x_ref.shape[0]
    row = mod_row(pl.program_id(0))
    scale = ng_ref[...] * (1.0 + _mod(mod_ref, 1, row))
    h = _rms(x_ref[...]) * scale + _mod(mod_ref, 0, row)
    z = _dot(h.astype(BF16), win_ref[...])
    o1, o2, o3 = POOL_WIDTH, POOL_WIDTH + Q_RANK, POOL_WIDTH + Q_RANK + KV_RANK

    ckv = _rms(z[:, o2:o3]) * kvg_ref[...]
    kr = z[:, o3:]
    ckv_ref[...] = ckv
    kr_ref[...] = kr[:, QK_NOPE:QK_DIM]
    cqn = (_rms(z[:, o1:o2]) * qg_ref[...]).astype(BF16)
    kpre = _kv_matmuls(ckv.astype(BF16), wk_ref, wv_ref, v_ref)
    qf = _dot(cqn, wuq_ref[...])

    qt = qng_ref[...] * SQRT_QK
    if use_rope:
        qt = qt * cs_ref[3]
    lane2 = lax.broadcasted_iota(jnp.int32, (tm, 2 * LANES), 1)
    for j in range(MLA_HEADS // 2):
        qp = qf[:, 2 * j * LANES:(2 * j + 2) * LANES]
        sq = qp * qp
        if j < Q_SS_MXU_PAIRS:
            hi = sq.astype(BF16)
            lo = (sq - hi.astype(F32)).astype(BF16)
            ss = _dot(hi, ones_ref[...]) + _dot(lo, ones_ref[...])
            qn = qp * lax.rsqrt(ss + QK_DIM * EPS)
            parts = [qn[:, :LANES], qn[:, LANES:]]
        else:
            sq = jnp.where((lane2 & (LANES - 1)) < QK_DIM, sq, 0.0)
            parts = []
            for hh in range(2):
                ss = jnp.sum(sq[:, hh * LANES:(hh + 1) * LANES], axis=-1, keepdims=True)
                parts.append(qp[:, hh * LANES:(hh + 1) * LANES] * lax.rsqrt(ss + QK_DIM * EPS))
        for hh in range(2):
            hd = 2 * j + hh
            q_ref[:, hd * LANES:(hd + 1) * LANES] = (parts[hh] * qt).astype(BF16)

    _k_heads(kpre, kr, kng_ref, cs_ref, k_ref)

    pos = lax.broadcasted_iota(jnp.int32, (tm, POOL_CH), 0) & (seq_len - 1)
    for g, w in enumerate(POOL_WINDOWS):
        ug = z[:, g * POOL_CH:(g + 1) * POOL_CH]
        p = _pool_group(ug, w // 2, pos, seq_len)
        y = _dot(p.astype(BF16), pw_ref[g]) * ps_ref[:, g * POOL_CH:(g + 1) * POOL_CH]
        yp_ref[:, g * POOL_CH:(g + 1) * POOL_CH] = y.astype(BF16)


def _l0a_call(x, mod_all, layer, mod_row, tm, seq_len, w, cs):
    n_tok = x.shape[0]
    use_rope = cs is not None
    row_spec = lambda width: pl.BlockSpec((tm, width), lambda t: (t, 0))
    in_specs = [
        row_spec(D_MODEL),
        _mod_spec(layer),
        _const_spec((1, D_MODEL)),
        _const_spec((D_MODEL, AB_IN_PAD)),
        _const_spec((1, Q_RANK)),
        _const_spec((1, KV_RANK)),
        _const_spec((Q_RANK, HEAD_W)),
        _const_spec((KV_RANK, HEAD_W)),
        _const_spec((MLA_WIDTH, KV_RANK)),
        _const_spec((1, LANES)),
        _const_spec((1, LANES)),
        _const_spec((POOL_GROUPS, POOL_CH, POOL_CH)),
        _const_spec((1, POOL_WIDTH)),
        _const_spec((2 * LANES, 2 * LANES)),
    ]
    args = [x, mod_all, w["norm_mix_g"], w["w_in"], w["q_norm_g"], w["kv_norm_g"], w["w_uq"],
            w["w_k"], w["w_vt"], w["qn_g_rope"] if use_rope else w["qn_g"], w["kn_g"],
            w["pool_w"], w["pool_scale"], w["head_ones"]]
    if use_rope:
        assert tm == seq_len
        in_specs.append(_const_spec((4, seq_len, LANES)))
        args.append(cs)
    out_shape = [
        jax.ShapeDtypeStruct((n_tok, POOL_WIDTH), BF16),
        jax.ShapeDtypeStruct((n_tok, HEAD_W), BF16),
        jax.ShapeDtypeStruct((n_tok, HEAD_W), BF16),
        jax.ShapeDtypeStruct((MLA_WIDTH, n_tok), BF16),
        jax.ShapeDtypeStruct((n_tok, KV_RANK), F32),
        jax.ShapeDtypeStruct((n_tok, QK_ROPE), F32),
    ]
    out_specs = [row_spec(POOL_WIDTH), row_spec(HEAD_W), row_spec(HEAD_W),
                 pl.BlockSpec((MLA_WIDTH, tm), lambda t: (0, t)),
                 row_spec(KV_RANK), row_spec(QK_ROPE)]
    return pl.pallas_call(
        functools.partial(_l0a_kernel, seq_len=seq_len, use_rope=use_rope, mod_row=mod_row),
        grid=(n_tok // tm,),
        in_specs=in_specs,
        out_specs=out_specs,
        out_shape=out_shape,
        compiler_params=_params(1),
        name="l0_front_rope" if use_rope else "l0_front",
    )(*args)


def _kvc_kernel(ckv_ref, kr_ref, wk_ref, wv_ref, kng_ref, k_ref, v_ref):
    kpre = _kv_matmuls(ckv_ref[...].astype(BF16), wk_ref, wv_ref, v_ref)
    _k_heads(kpre, kr_ref[...], kng_ref, None, k_ref)


def _kvc_call(ckv, kr_pad, w, tm):
    n_tok = ckv.shape[0]
    row_spec = lambda width: pl.BlockSpec((tm, width), lambda t: (t, 0))
    return pl.pallas_call(
        _kvc_kernel,
        grid=(n_tok // tm,),
        in_specs=[row_spec(KV_RANK), row_spec(LANES), _const_spec((KV_RANK, HEAD_W)),
                  _const_spec((MLA_WIDTH, KV_RANK)), _const_spec((1, LANES))],
        out_specs=[row_spec(HEAD_W), pl.BlockSpec((MLA_WIDTH, tm), lambda t: (0, t))],
        out_shape=[jax.ShapeDtypeStruct((n_tok, HEAD_W), BF16),
                   jax.ShapeDtypeStruct((MLA_WIDTH, n_tok), BF16)],
        compiler_params=_params(1),
        name="cache_kv",
    )(ckv, kr_pad, w["w_k"], w["w_vt"], w["kn_g"])


ATT_TQ = 256
ATT_UNITS = 4
ATT_KC = 512
ATT_AHEAD = 8


def _att_kernel(xc_ref, ypc_ref, qc_ref, kc_ref, vc_ref,
                xl_ref, ypl_ref, ql_ref, kl_ref, vl_ref, kp_ref, vp_ref,
                mod_ref, wop_ref, woa_ref, o_ref, ya_ref, *, n_ctx_steps, steps_per_batch):
    t = pl.program_id(0)

    def rows(u):
        return slice(u * ATT_TQ, (u + 1) * ATT_TQ)

    @pl.when(t < n_ctx_steps)
    def _():
        units = [(rows(u), [(kc_ref, vc_ref, rows(u))]) for u in range(ATT_UNITS)]
        _att_tile(units, xc_ref, ypc_ref, qc_ref, mod_ref, 0, wop_ref, woa_ref, o_ref, ya_ref)

    @pl.when(t >= n_ctx_steps)
    def _():
        srcs = [(kp_ref, vp_ref, slice(0, kp_ref.shape[0])), (kl_ref, vl_ref, slice(0, kl_ref.shape[0]))]
        units = [(rows(u), srcs) for u in range(ATT_UNITS)]
        row = 1 + (t - n_ctx_steps) // steps_per_batch
        _att_tile(units, xl_ref, ypl_ref, ql_ref, mod_ref, row, wop_ref, woa_ref, o_ref, ya_ref)


def _att_tile(units, x_ref, yp_ref, q_ref, mod_ref, row, wop_ref, woa_ref, o_ref, ya_ref):
    items = []
    for u, (qrows, sources) in enumerate(units):
        chunks = []
        for ks_ref, vs_ref, krange in sources:
            n_keys = krange.stop - krange.start
            kc = min(ATT_KC, n_keys)
            assert n_keys % kc == 0
            chunks += [(ks_ref, vs_ref, slice(krange.start + c * kc, krange.start + (c + 1) * kc))
                       for c in range(n_keys // kc)]
        items += [(u, qrows, hd, ch, ch is chunks[-1]) for hd in range(MLA_HEADS) for ch in chunks]

    def scores(item):
        _, qrows, hd, (ks_ref, _, keys), _ = item
        sl = slice(hd * LANES, (hd + 1) * LANES)
        s = _dot_nt(ks_ref[keys, sl], q_ref[qrows, sl])
        return s, jnp.max(s, axis=0, keepdims=True)

    pending = [scores(it) for it in items[:ATT_AHEAD]]
    state = {}
    gate = _mod(mod_ref, 2, row)
    low_half = lax.broadcasted_iota(jnp.int32, (LANES, ATT_TQ), 0) < V_DIM
    for n, (u, qrows, hd, (_, vs_ref, keys), last_chunk) in enumerate(items):
        if n + ATT_AHEAD < len(items):
            pending.append(scores(items[n + ATT_AHEAD]))
        s, cmax = pending.pop(0)
        vrows = slice((hd // 2) * LANES, (hd // 2 + 1) * LANES)
        if (u, hd) not in state:
            mx = cmax
            p = jnp.exp2(s - mx)
            den = jnp.sum(p, axis=0, keepdims=True)
            acc = _dot(vs_ref[vrows, keys], p.astype(BF16))
        else:
            mx, den, acc = state[u, hd]
            new = jnp.maximum(mx, cmax)
            alpha = jnp.exp2(mx - new)
            p = jnp.exp2(s - new)
            den = alpha * den + jnp.sum(p, axis=0, keepdims=True)
            acc = alpha * acc + _dot(vs_ref[vrows, keys], p.astype(BF16))
            mx = new
        state[u, hd] = (mx, den, acc)
        if hd % 2 == 1 and last_chunk:
            even = state[u, hd - 1][2] / state[u, hd - 1][1]
            pair = jnp.where(low_half, even, acc / den)
            ya_ref[vrows, qrows] = pair.astype(BF16)
        if hd == MLA_HEADS - 1 and last_chunk:
            y = _dot(yp_ref[qrows, :], wop_ref[...]) + _dot_tn(ya_ref[:, qrows], woa_ref[...])
            o_ref[qrows, :] = x_ref[qrows, :] + gate * y


def _att_call(ctx_in, lat_in, cache, mod_all, layer, ctx_len, lat_len, w):
    assert ctx_len == ATT_TQ
    tb = ATT_TQ * ATT_UNITS
    n_ctx, n_lat = ctx_in[0].shape[0], lat_in[0].shape[0]
    assert n_ctx % tb == 0 and lat_len % tb == 0
    n_c = n_ctx // tb
    per_batch = lat_len // tb
    past = cache[0].shape[0] // (n_lat // lat_len)
    cstep = lambda t: jnp.minimum(t, n_c - 1)
    lstep = lambda t: jnp.maximum(t - n_c, 0)
    crow = lambda width: pl.BlockSpec((tb, width), lambda t: (cstep(t), 0))
    lrow = lambda width: pl.BlockSpec((tb, width), lambda t: (lstep(t), 0))
    lbat = lambda rows: pl.BlockSpec((rows, HEAD_W), lambda t: (lstep(t) // per_batch, 0))
    lbat_t = lambda cols: pl.BlockSpec((MLA_WIDTH, cols), lambda t: (0, lstep(t) // per_batch))
    in_specs = [
        crow(D_MODEL), crow(POOL_WIDTH), crow(HEAD_W), crow(HEAD_W),
        pl.BlockSpec((MLA_WIDTH, tb), lambda t: (0, cstep(t))),
        lrow(D_MODEL), lrow(POOL_WIDTH), lrow(HEAD_W), lbat(lat_len), lbat_t(lat_len),
        lbat(past), lbat_t(past),
        _mod_spec(layer), _const_spec((POOL_WIDTH, D_MODEL)), _const_spec((MLA_WIDTH, D_MODEL)),
    ]
    return pl.pallas_call(
        functools.partial(_att_kernel, n_ctx_steps=n_c, steps_per_batch=per_batch),
        grid=(n_c + n_lat // tb,),
        in_specs=in_specs,
        out_specs=pl.BlockSpec((tb, D_MODEL), lambda t: (t, 0)),
        out_shape=jax.ShapeDtypeStruct((n_ctx + n_lat, D_MODEL), F32),
        scratch_shapes=[pltpu.VMEM((MLA_WIDTH, tb), BF16)],
        compiler_params=_params(1),
        name="att_out",
    )(*ctx_in, *lat_in, *cache, mod_all, w["w_out_pool"], w["w_out_att"])


FFN_SUB = 256
FFN_GCH = D_MODEL // W_STEPS
FFN_DCH = D_FF // W_STEPS


def _ffn_kernel(x_ref, mod_ref, ng_ref, wg_ref, wu_ref, wd_ref, *rest, mod_row, split):
    outs, (wg_s, wu_s, wd_s) = rest[:-3], rest[-3:]
    t = pl.program_id(0)

    @pl.when(t < W_STEPS)
    def _():
        r = pl.multiple_of(t * FFN_GCH, FFN_GCH)
        wg_s[pl.ds(r, FFN_GCH), :] = wg_ref[0].astype(BF16)
        wu_s[pl.ds(r, FFN_GCH), :] = wu_ref[0].astype(BF16)
        r = pl.multiple_of(t * FFN_DCH, FFN_DCH)
        wd_s[pl.ds(r, FFN_DCH), :] = wd_ref[0].astype(BF16)

    def tile(o_ref):
        row = mod_row(t - W_STEPS)
        scale = ng_ref[...] * (1.0 + _mod(mod_ref, 4, row))
        shift = _mod(mod_ref, 3, row)
        gate = _mod(mod_ref, 5, row)
        for s in range(x_ref.shape[0] // FFN_SUB):
            rows = slice(s * FFN_SUB, (s + 1) * FFN_SUB)
            x = x_ref[rows, :]
            h = (_rms(x) * scale + shift).astype(BF16)
            g = _dot(h, wg_s[...])
            u = _dot(h, wu_s[...])
            d = _dot((_silu(g) * u).astype(BF16), wd_s[...])
            o_ref[rows, :] = x + gate * d

    if split is None:
        pl.when(t >= W_STEPS)(lambda: tile(outs[0]))
    else:
        pl.when((t >= W_STEPS) & (t < W_STEPS + split))(lambda: tile(outs[0]))
        pl.when(t >= W_STEPS + split)(lambda: tile(outs[1]))


def _ffn_call(x, mod_all, layer, mod_row, tm, ng, wg, wu, wd, n_first=None):
    n_tok = x.shape[0]
    n_tiles = n_tok // tm
    tok = lambda t: jnp.maximum(t - W_STEPS, 0)
    wstep = lambda t: jnp.minimum(t, W_STEPS - 1)
    if n_first is None:
        split = None
        out_specs = pl.BlockSpec((tm, D_MODEL), lambda t: (tok(t), 0))
        out_shape = jax.ShapeDtypeStruct((n_tok, D_MODEL), F32)
    else:
        split = n_first // tm
        out_specs = [
            pl.BlockSpec((tm, D_MODEL), lambda t: (jnp.minimum(tok(t), split - 1), 0)),
            pl.BlockSpec((tm, D_MODEL), lambda t: (jnp.maximum(tok(t) - split, 0), 0)),
        ]
        out_shape = [jax.ShapeDtypeStruct((n_first, D_MODEL), F32),
                     jax.ShapeDtypeStruct((n_tok - n_first, D_MODEL), F32)]
    return pl.pallas_call(
        functools.partial(_ffn_kernel, mod_row=mod_row, split=split),
        grid=(W_STEPS + n_tiles,),
        in_specs=[
            pl.BlockSpec((tm, D_MODEL), lambda t: (tok(t), 0)),
            _mod_spec(layer),
            _const_spec((1, D_MODEL)),
            pl.BlockSpec((1, FFN_GCH, D_FF), lambda t: (layer, wstep(t), 0)),
            pl.BlockSpec((1, FFN_GCH, D_FF), lambda t: (layer, wstep(t), 0)),
            pl.BlockSpec((1, FFN_DCH, D_MODEL), lambda t: (layer, wstep(t), 0)),
        ],
        out_specs=out_specs,
        out_shape=out_shape,
        scratch_shapes=[pltpu.VMEM((D_MODEL, D_FF), BF16), pltpu.VMEM((D_MODEL, D_FF), BF16),
                        pltpu.VMEM((D_FF, D_MODEL), BF16)],
        compiler_params=_params(1),
        name="ffn",
    )(x, mod_all, ng, wg, wu, wd)


GM_TM = 1024
GM_SUB = 256
GM_WCH = D_MODEL // W_STEPS


def _gmlp_kernel(x_ref, mod_ref, ng_ref, win_ref, vg_ref, ws_ref, bs_ref, wout_ref, o_ref,
                 win_s, wout_s, ws_s, gs_ref, *, mod_row):
    t = pl.program_id(0)

    @pl.when(t < W_STEPS)
    def _():
        r = pl.multiple_of(t * GM_WCH, GM_WCH)
        win_s[pl.ds(r, GM_WCH), :] = win_ref[0].astype(BF16)
        wout_s[pl.ds(r, GM_WCH), :] = wout_ref[0].astype(BF16)
        ws_s[t] = ws_ref[0, 0].astype(BF16)

    @pl.when(t >= W_STEPS)
    def _():
        row = mod_row(t - W_STEPS)
        scale = ng_ref[...] * (1.0 + _mod(mod_ref, 1, row))
        shift = _mod(mod_ref, 0, row)
        gate = _mod(mod_ref, 2, row)
        n_chunks = GM_SUB // CHUNK
        n_sub = x_ref.shape[0] // GM_SUB

        def project(s):
            x = x_ref[s * GM_SUB:(s + 1) * GM_SUB, :]
            h = (_rms(x) * scale + shift).astype(BF16)
            return x, _dot(h, win_s[...])

        ahead = project(0)
        for s in range(n_sub):
            rows = slice(s * GM_SUB, (s + 1) * GM_SUB)
            x, z = ahead
            if s + 1 < n_sub:
                ahead = project(s + 1)
            vn = (_rms(z[:, C_WIDTH:]) * vg_ref[...]).astype(BF16)
            for g in range(C_GROUPS):
                cols = slice(g * C_CH, (g + 1) * C_CH)
                rhs = jnp.concatenate(
                    [vn[n * CHUNK:(n + 1) * CHUNK, cols] for n in range(n_chunks)], axis=1)
                sp = _dot(ws_s[g], rhs)
                for n in range(n_chunks):
                    crow = slice(n * CHUNK, (n + 1) * CHUNK)
                    grow = slice(s * GM_SUB + n * CHUNK, s * GM_SUB + (n + 1) * CHUNK)
                    sn = sp[:, n * C_CH:(n + 1) * C_CH] + bs_ref[g]
                    gs_ref[grow, cols] = (z[crow, cols] * sn).astype(BF16)
            o_ref[rows, :] = x + gate * _dot(gs_ref[rows, :], wout_s[...])


def _gmlp_call(x, mod_all, layer, mod_row, w, o):
    n_tok = x.shape[0]
    tok = lambda t: jnp.maximum(t - W_STEPS, 0)
    wstep = lambda t: jnp.minimum(t, W_STEPS - 1)
    assert C_GROUPS == W_STEPS
    return pl.pallas_call(
        functools.partial(_gmlp_kernel, mod_row=mod_row),
        grid=(W_STEPS + n_tok // GM_TM,),
        in_specs=[
            pl.BlockSpec((GM_TM, D_MODEL), lambda t: (tok(t), 0)),
            _mod_spec(layer),
            _const_spec((1, D_MODEL)),
            pl.BlockSpec((1, GM_WCH, 2 * C_WIDTH), lambda t: (o, wstep(t), 0)),
            _const_spec((1, C_WIDTH)),
            pl.BlockSpec((1, 1, CHUNK, CHUNK), lambda t: (o, wstep(t), 0, 0)),
            _const_spec((C_GROUPS, CHUNK, C_CH)),
            pl.BlockSpec((1, GM_WCH, D_MODEL), lambda t: (o, wstep(t), 0)),
        ],
        out_specs=pl.BlockSpec((GM_TM, D_MODEL), lambda t: (tok(t), 0)),
        out_shape=jax.ShapeDtypeStruct((n_tok, D_MODEL), F32),
        scratch_shapes=[pltpu.VMEM((D_MODEL, 2 * C_WIDTH), BF16), pltpu.VMEM((C_WIDTH, D_MODEL), BF16),
                        pltpu.VMEM((C_GROUPS, CHUNK, CHUNK), BF16), pltpu.VMEM((GM_TM, C_WIDTH), BF16)],
        compiler_params=_params(1),
        name="gmlp",
    )(x, mod_all, w["norm_mix_g"], w["w_in"], w["vnorm_g"], w["w_s"], w["b_s"], w["w_out"])


def _swap_halves(t):
    return jnp.concatenate([t[..., HALF:], t[..., :HALF]], axis=-1)


def _head_ones():
    lane = np.arange(2 * LANES)
    m = (lane[:, None] // LANES == lane[None, :] // LANES) & (lane[:, None] % LANES < QK_DIM)
    return jnp.asarray(m.astype(np.float32), dtype=BF16)


def _layer0_weights(e, norm_mix_g, ab_w_in, pool_w, pool_scale, q_norm_g, kv_norm_g, w_uq, w_ukv,
                    qn_g, kn_g, ab_w_out):
    w_in = ab_w_in[e]
    o3 = POOL_WIDTH + Q_RANK + KV_RANK
    kr_cols = jnp.concatenate(
        [jnp.zeros((D_MODEL, QK_NOPE), F32), w_in[:, o3:], w_in[:, o3:]], axis=1)
    w_in_p = jnp.concatenate([w_in[:, :o3], kr_cols], axis=1).astype(BF16)
    uq = w_uq[e].reshape(Q_RANK, MLA_HEADS, QK_DIM)
    uq = jnp.concatenate([uq, _swap_halves(uq[:, :, QK_NOPE:])], axis=-1)
    score_scale = QK_DIM ** -0.5 * math.log2(math.e)
    gq = qn_g[e] * score_scale
    gk = kn_g[e]
    ukv = w_ukv[e].reshape(KV_RANK, MLA_HEADS, QK_NOPE + V_DIM)
    w_k = jnp.pad(ukv[:, :, :QK_NOPE], ((0, 0), (0, 0), (0, LANES - QK_NOPE)))
    w_v = ukv[:, :, QK_NOPE:].reshape(KV_RANK, MLA_WIDTH)
    return {
        "norm_mix_g": norm_mix_g.reshape(1, D_MODEL),
        "w_in": w_in_p,
        "q_norm_g": q_norm_g[e].reshape(1, Q_RANK),
        "kv_norm_g": kv_norm_g[e].reshape(1, KV_RANK),
        "w_uq": uq.reshape(Q_RANK, HEAD_W).astype(BF16),
        "w_k": w_k.reshape(KV_RANK, HEAD_W).astype(BF16),
        "w_vt": w_v.T.astype(BF16),
        "qn_g": jnp.pad(gq, (0, LANES - QK_DIM)).reshape(1, LANES),
        "qn_g_rope": jnp.concatenate([gq, _swap_halves(gq[QK_NOPE:])]).reshape(1, LANES),
        "kn_g": jnp.concatenate([gk, gk[QK_NOPE:]]).reshape(1, LANES),
        "head_ones": _head_ones(),
        "pool_w": pool_w[e].astype(BF16),
        "pool_scale": pool_scale[e].reshape(1, POOL_WIDTH),
        "w_out_pool": ab_w_out[e, :POOL_WIDTH].astype(BF16),
        "w_out_att": ab_w_out[e, POOL_WIDTH:].astype(BF16),
    }


def _rope_tables(seq_len):
    rows = seq_len // GRID_W
    row = np.repeat(np.arange(rows), GRID_W).astype(np.float32)
    col = np.tile(np.arange(GRID_W), rows).astype(np.float32)
    per_axis = QK_ROPE // 2
    inv = (1.0 / (np.float32(ROPE_BASE) ** (np.arange(0, per_axis, 2, dtype=np.float32) / per_axis))
           ).astype(np.float32)
    ang = np.concatenate([row[:, None] * inv, col[:, None] * inv], axis=-1)
    cos, sin = np.cos(ang).astype(np.float32), np.sin(ang).astype(np.float32)
    ones = np.ones((seq_len, QK_NOPE), np.float32)
    z_nope = np.zeros((seq_len, QK_NOPE), np.float32)
    z_half = np.zeros((seq_len, HALF), np.float32)
    c_tab = np.concatenate([ones, cos, cos, cos, cos], axis=-1)
    a_tab = np.concatenate([z_nope, -sin, z_half, -sin, z_half], axis=-1)
    b_tab = np.concatenate([z_nope, z_half, sin, z_half, sin], axis=-1)
    q_tab = np.concatenate([ones, cos, cos, -sin, sin], axis=-1)
    return jnp.asarray(np.stack([c_tab, a_tab, b_tab, q_tab]))


def kernel(x_prompt, x_sample, cache_ckv, cache_krope, c, c_ctx, ada_w, ada_b, norm_mix_g, norm_ffn_g, ffn_wg, ffn_wu, ffn_wd, ab_w_in, pool_w, pool_scale, q_norm_g, kv_norm_g, w_uq, w_ukv, qn_g, kn_g, ab_w_out, gm_w_in, gm_vnorm_g, gm_ws, gm_bs, gm_w_out):
    n_ctx_b, ctx_len, _ = x_prompt.shape
    n_lat_b, lat_len, _ = x_sample.shape
    past = cache_ckv.shape[2]
    assert ctx_len & (ctx_len - 1) == 0 and lat_len & (lat_len - 1) == 0
    assert MOD_ROWS >= 1 + n_lat_b
    assert DEPTH == 2
    n_ctx, n_lat = n_ctx_b * ctx_len, n_lat_b * lat_len

    ctx = x_prompt.reshape(n_ctx, D_MODEL)
    lat = x_sample.reshape(n_lat, D_MODEL)

    cond = jnp.concatenate(
        [c_ctx[None, :], c, jnp.zeros((MOD_ROWS - 1 - n_lat_b, D_MODEL), F32)], axis=0)
    mod_all = _ada_call(cond, ada_w, ada_b)

    def all_row(tm):
        n_ctx_tiles, per = n_ctx // tm, lat_len // tm
        return lambda j: jnp.where(j < n_ctx_tiles, 0, 1 + (j - n_ctx_tiles) // per)

    w = _layer0_weights(0, norm_mix_g[0], ab_w_in, pool_w, pool_scale, q_norm_g, kv_norm_g,
                        w_uq, w_ukv, qn_g, kn_g, ab_w_out)
    cs = _rope_tables(lat_len)
    yp_c, q_c, k_c, v_c, ckv_c, kr_c = _l0a_call(ctx, mod_all, 0, lambda t: 0, 1024, ctx_len, w, None)
    yp_l, q_l, k_l, v_l, _, _ = _l0a_call(lat, mod_all, 0, lambda t: 1 + t, lat_len, lat_len, w, cs)
    kr_c2 = cache_krope[:, 0].reshape(n_lat_b * past, QK_ROPE)
    kr_pad = jnp.concatenate([jnp.zeros((n_lat_b * past, QK_NOPE), F32), kr_c2, kr_c2], axis=1)
    cache = _kvc_call(cache_ckv[:, 0].reshape(n_lat_b * past, KV_RANK), kr_pad, w, 1024)
    xs = _att_call((ctx, yp_c, q_c, k_c, v_c), (lat, yp_l, q_l, k_l, v_l), cache,
                   mod_all, 0, ctx_len, lat_len, w)
    xs = _ffn_call(xs, mod_all, 0, all_row(1024), 1024, norm_ffn_g[0].reshape(1, D_MODEL),
                   ffn_wg, ffn_wu, ffn_wd)

    gw = {
        "norm_mix_g": norm_mix_g[1].reshape(1, D_MODEL),
        "w_in": gm_w_in,
        "vnorm_g": gm_vnorm_g[0].reshape(1, C_WIDTH),
        "w_s": gm_ws,
        "b_s": jnp.broadcast_to(gm_bs[0][:, :, None], (C_GROUPS, CHUNK, C_CH)),
        "w_out": gm_w_out,
    }
    xs = _gmlp_call(xs, mod_all, 1, all_row(GM_TM), gw, 0)
    y_ctx, y_lat = _ffn_call(xs, mod_all, 1, all_row(1024), 1024, norm_ffn_g[1].reshape(1, D_MODEL),
                             ffn_wg, ffn_wu, ffn_wd, n_first=n_ctx)

    state_ckv = ckv_c.reshape(n_ctx_b, 1, ctx_len, KV_RANK)
    state_krope = kr_c.reshape(n_ctx_b, 1, ctx_len, QK_ROPE)
    return (y_ctx.reshape(n_ctx_b, ctx_len, D_MODEL), y_lat.reshape(n_lat_b, lat_len, D_MODEL),
            state_ckv, state_krope)
```

```python
import functools
import math

import numpy as np
import jax
import jax.numpy as jnp
from jax import lax
from jax.experimental import pallas as pl
from jax.experimental.pallas import tpu as pltpu

D_MODEL = 1024
DEPTH = 2
GRID_W = 64
POOL_WINDOWS = (2, 4, 8, 16)
POOL_GROUPS = 4
POOL_CH = 128
POOL_WIDTH = POOL_GROUPS * POOL_CH
MLA_HEADS = 8
QK_NOPE = 64
QK_ROPE = 32
QK_DIM = QK_NOPE + QK_ROPE
HALF = QK_ROPE // 2
SQRT_QK = math.sqrt(QK_DIM)
V_DIM = 64
Q_RANK = 384
KV_RANK = 256
MLA_WIDTH = MLA_HEADS * V_DIM
CHUNK = 128
C_GROUPS = 8
C_WIDTH = D_MODEL
C_CH = C_WIDTH // C_GROUPS
D_FF = 2816
ROPE_BASE = 10000.0
EPS = 1e-6

LANES = 128
HEAD_W = MLA_HEADS * LANES
AB_IN_PAD = POOL_WIDTH + Q_RANK + KV_RANK + LANES
MOD_ROWS = 16
VMEM_LIMIT = 60 * 1024 * 1024
W_STEPS = 8
Q_SS_MXU_PAIRS = 2

F32 = jnp.float32
BF16 = jnp.bfloat16


def _rms(x):
    return x * lax.rsqrt(jnp.mean(x * x, axis=-1, keepdims=True) + EPS)


def _dot(a, b):
    return jnp.dot(a, b, preferred_element_type=F32)


def _dot_nt(a, b):
    return lax.dot_general(a, b, (((1,), (1,)), ((), ())), preferred_element_type=F32)


def _dot_tn(a, b):
    return lax.dot_general(a, b, (((0,), (0,)), ((), ())), preferred_element_type=F32)


def _silu(x):
    return x * jax.nn.sigmoid(x)


def _params(n_axes, vmem_mib):
    assert vmem_mib * 1024 * 1024 <= VMEM_LIMIT
    return pltpu.CompilerParams(
        dimension_semantics=("arbitrary",) * n_axes, vmem_limit_bytes=vmem_mib * 1024 * 1024)


def _const_spec(shape):
    zeros = (0,) * len(shape)
    return pl.BlockSpec(shape, lambda *_: zeros)


def _mod_spec(layer):
    return pl.BlockSpec((1, 6, MOD_ROWS, D_MODEL), lambda *_: (layer, 0, 0, 0))


def _mod(mod_ref, term, row):
    return mod_ref[0, term, pl.ds(row, 1), :]


ADA_TERMS = 2


def _ada_kernel(cond_ref, w_ref, b_ref, o_ref):
    s = _silu(cond_ref[...]).astype(BF16)
    bias = b_ref[pl.ds(pl.program_id(0), 1), :]
    y = _dot(s, w_ref[0].astype(BF16)) + bias
    for k in range(ADA_TERMS):
        o_ref[0, k] = y[:, k * D_MODEL:(k + 1) * D_MODEL]


def _ada_call(cond, ada_w, ada_b):
    width = ADA_TERMS * D_MODEL
    return pl.pallas_call(
        _ada_kernel,
        grid=(DEPTH, 6 // ADA_TERMS),
        in_specs=[
            pl.BlockSpec((MOD_ROWS, D_MODEL), lambda i, j: (0, 0)),
            pl.BlockSpec((1, D_MODEL, width), lambda i, j: (i, 0, j)),
            pl.BlockSpec((DEPTH, width), lambda i, j: (0, j)),
        ],
        out_specs=pl.BlockSpec((1, ADA_TERMS, MOD_ROWS, D_MODEL), lambda i, j: (i, j, 0, 0)),
        out_shape=jax.ShapeDtypeStruct((DEPTH, 6, MOD_ROWS, D_MODEL), F32),
        compiler_params=_params(2, 20),
        name="ada_mod",
    )(cond, ada_w, ada_b)


def _rope(t, cs_ref):
    return (t * cs_ref[0]
            + pltpu.roll(t, LANES - HALF, axis=1) * cs_ref[1]
            + pltpu.roll(t, HALF, axis=1) * cs_ref[2])


def _kv_matmuls(ckv_bf, wk_ref, wv_ref, v_ref):
    v_ref[...] = _dot_nt(wv_ref[...], ckv_bf).astype(BF16)
    return _dot(ckv_bf, wk_ref[...])


def _k_heads(kpre, kr, kng_ref, cs_ref, k_ref):
    kg = kng_ref[...] * SQRT_QK
    krg = kr * kg
    if cs_ref is not None:
        krg = _rope(krg, cs_ref)
    lane = lax.broadcasted_iota(jnp.int32, kr.shape, 1)
    kr_ss = jnp.sum(jnp.where(lane < QK_DIM, kr * kr, 0.0), axis=-1, keepdims=True) + QK_DIM * EPS
    for h in range(MLA_HEADS):
        kh = kpre[:, h * LANES:(h + 1) * LANES]
        r = lax.rsqrt(jnp.sum(kh * kh, axis=-1, keepdims=True) + kr_ss)
        k_ref[:, h * LANES:(h + 1) * LANES] = ((kh * kg + krg) * r).astype(BF16)


def _shift_rows(a, k, pos, seq_len):
    n = a.shape[0]
    r = pltpu.roll(a, k % n, axis=0)
    src = pos - k
    ok = (src >= 0) if k > 0 else (src < seq_len)
    return jnp.where(ok, r, 0.0)


def _pool_group(ug, half, pos, seq_len):
    fw = ug
    step = 1
    while step < half:
        fw = fw + _shift_rows(fw, -step, pos, seq_len)
        step *= 2
    bk = _shift_rows(ug, 1, pos, seq_len)
    step = 1
    while step < half:
        bk = bk + _shift_rows(bk, step, pos, seq_len)
        step *= 2
    cnt = jnp.minimum(pos + half, seq_len) - jnp.maximum(pos - half, 0)
    return (fw + bk) / cnt.astype(F32) - ug


def _l0a_kernel(*refs, seq_len, use_rope, mod_row):
    (x_ref, mod_ref, ng_ref, win_ref, qg_ref, kvg_ref, wuq_ref, wk_ref, wv_ref,
     qng_ref, kng_ref, pw_ref, ps_ref, ones_ref) = refs[:14]
    if use_rope:
        cs_ref = refs[14]
        outs = refs[15:]
    else:
        cs_ref = None
        outs = refs[14:]
    yp_ref, q_ref, k_ref, v_ref, ckv_ref, kr_ref = outs

    tm = x_ref.shape[0]
    row = mod_row(pl.program_id(0))
    scale = ng_ref[...] * (1.0 + _mod(mod_ref, 1, row))
    h = _rms(x_ref[...]) * scale + _mod(mod_ref, 0, row)
    z = _dot(h.astype(BF16), win_ref[...])
    o1, o2, o3 = POOL_WIDTH, POOL_WIDTH + Q_RANK, POOL_WIDTH + Q_RANK + KV_RANK

    ckv = _rms(z[:, o2:o3]) * kvg_ref[...]
    kr = z[:, o3:]
    ckv_ref[...] = ckv
    kr_ref[...] = kr[:, QK_NOPE:QK_DIM]
    cqn = (_rms(z[:, o1:o2]) * qg_ref[...]).astype(BF16)
    kpre = _kv_matmuls(ckv.astype(BF16), wk_ref, wv_ref, v_ref)
    qf = _dot(cqn, wuq_ref[...])

    qt = qng_ref[...] * SQRT_QK
    if use_rope:
        qt = qt * cs_ref[3]
    lane2 = lax.broadcasted_iota(jnp.int32, (tm, 2 * LANES), 1)
    for j in range(MLA_HEADS // 2):
        qp = qf[:, 2 * j * LANES:(2 * j + 2) * LANES]
        sq = qp * qp
        if j < Q_SS_MXU_PAIRS:
            hi = sq.astype(BF16)
            lo = (sq - hi.astype(F32)).astype(BF16)
            ss = _dot(hi, ones_ref[...]) + _dot(lo, ones_ref[...])
            qn = qp * lax.rsqrt(ss + QK_DIM * EPS)
            parts = [qn[:, :LANES], qn[:, LANES:]]
        else:
            sq = jnp.where((lane2 & (LANES - 1)) < QK_DIM, sq, 0.0)
            parts = []
            for hh in range(2):
                ss = jnp.sum(sq[:, hh * LANES:(hh + 1) * LANES], axis=-1, keepdims=True)
                parts.append(qp[:, hh * LANES:(hh + 1) * LANES] * lax.rsqrt(ss + QK_DIM * EPS))
        for hh in range(2):
            hd = 2 * j + hh
            q_ref[:, hd * LANES:(hd + 1) * LANES] = (parts[hh] * qt).astype(BF16)

    _k_heads(kpre, kr, kng_ref, cs_ref, k_ref)

    pos = lax.broadcasted_iota(jnp.int32, (tm, POOL_CH), 0) & (seq_len - 1)
    for g, w in enumerate(POOL_WINDOWS):
        ug = z[:, g * POOL_CH:(g + 1) * POOL_CH]
        p = _pool_group(ug, w // 2, pos, seq_len)
        y = _dot(p.astype(BF16), pw_ref[g]) * ps_ref[:, g * POOL_CH:(g + 1) * POOL_CH]
        yp_ref[:, g * POOL_CH:(g + 1) * POOL_CH] = y.astype(BF16)


def _l0a_call(x, mod_all, layer, mod_row, tm, seq_len, w, cs):
    n_tok = x.shape[0]
    use_rope = cs is not None
    row_spec = lambda width: pl.BlockSpec((tm, width), lambda t: (t, 0))
    in_specs = [
        row_spec(D_MODEL),
        _mod_spec(layer),
        _const_spec((1, D_MODEL)),
        _const_spec((D_MODEL, AB_IN_PAD)),
        _const_spec((1, Q_RANK)),
        _const_spec((1, KV_RANK)),
        _const_spec((Q_RANK, HEAD_W)),
        _const_spec((KV_RANK, HEAD_W)),
        _const_spec((MLA_WIDTH, KV_RANK)),
        _const_spec((1, LANES)),
        _const_spec((1, LANES)),
        _const_spec((POOL_GROUPS, POOL_CH, POOL_CH)),
        _const_spec((1, POOL_WIDTH)),
        _const_spec((2 * LANES, 2 * LANES)),
    ]
    args = [x, mod_all, w["norm_mix_g"], w["w_in"], w["q_norm_g"], w["kv_norm_g"], w["w_uq"],
            w["w_k"], w["w_vt"], w["qn_g_rope"] if use_rope else w["qn_g"], w["kn_g"],
            w["pool_w"], w["pool_scale"], w["head_ones"]]
    if use_rope:
        assert tm == seq_len
        in_specs.append(_const_spec((4, seq_len, LANES)))
        args.append(cs)
    out_shape = [
        jax.ShapeDtypeStruct((n_tok, POOL_WIDTH), BF16),
        jax.ShapeDtypeStruct((n_tok, HEAD_W), BF16),
        jax.ShapeDtypeStruct((n_tok, HEAD_W), BF16),
        jax.ShapeDtypeStruct((MLA_WIDTH, n_tok), BF16),
        jax.ShapeDtypeStruct((n_tok, KV_RANK), F32),
        jax.ShapeDtypeStruct((n_tok, QK_ROPE), F32),
    ]
    out_specs = [row_spec(POOL_WIDTH), row_spec(HEAD_W), row_spec(HEAD_W),
                 pl.BlockSpec((MLA_WIDTH, tm), lambda t: (0, t)),
                 row_spec(KV_RANK), row_spec(QK_ROPE)]
    return pl.pallas_call(
        functools.partial(_l0a_kernel, seq_len=seq_len, use_rope=use_rope, mod_row=mod_row),
        grid=(n_tok // tm,),
        in_specs=in_specs,
        out_specs=out_specs,
        out_shape=out_shape,
        compiler_params=_params(1, 42),
        name="l0_front_rope" if use_rope else "l0_front",
    )(*args)


def _kvc_kernel(ckv_ref, kr_ref, wk_ref, wv_ref, kng_ref, k_ref, v_ref):
    kpre = _kv_matmuls(ckv_ref[...].astype(BF16), wk_ref, wv_ref, v_ref)
    _k_heads(kpre, kr_ref[...], kng_ref, None, k_ref)


def _kvc_call(ckv, kr_pad, w, tm):
    n_tok = ckv.shape[0]
    row_spec = lambda width: pl.BlockSpec((tm, width), lambda t: (t, 0))
    return pl.pallas_call(
        _kvc_kernel,
        grid=(n_tok // tm,),
        in_specs=[row_spec(KV_RANK), row_spec(LANES), _const_spec((KV_RANK, HEAD_W)),
                  _const_spec((MLA_WIDTH, KV_RANK)), _const_spec((1, LANES))],
        out_specs=[row_spec(HEAD_W), pl.BlockSpec((MLA_WIDTH, tm), lambda t: (0, t))],
        out_shape=[jax.ShapeDtypeStruct((n_tok, HEAD_W), BF16),
                   jax.ShapeDtypeStruct((MLA_WIDTH, n_tok), BF16)],
        compiler_params=_params(1, 16),
        name="cache_kv",
    )(ckv, kr_pad, w["w_k"], w["w_vt"], w["kn_g"])


ATT_TQ = 256
ATT_UNITS = 4
ATT_KC = 512
ATT_AHEAD = 8


def _att_kernel(xc_ref, ypc_ref, qc_ref, kc_ref, vc_ref,
                xl_ref, ypl_ref, ql_ref, kl_ref, vl_ref, kp_ref, vp_ref,
                mod_ref, wop_ref, woa_ref, o_ref, ya_ref, *, n_ctx_steps, steps_per_batch):
    t = pl.program_id(0)

    def rows(u):
        return slice(u * ATT_TQ, (u + 1) * ATT_TQ)

    @pl.when(t < n_ctx_steps)
    def _():
        units = [(rows(u), [(kc_ref, vc_ref, rows(u))]) for u in range(ATT_UNITS)]
        _att_tile(units, xc_ref, ypc_ref, qc_ref, mod_ref, 0, wop_ref, woa_ref, o_ref, ya_ref)

    @pl.when(t >= n_ctx_steps)
    def _():
        srcs = [(kp_ref, vp_ref, slice(0, kp_ref.shape[0])), (kl_ref, vl_ref, slice(0, kl_ref.shape[0]))]
        units = [(rows(u), srcs) for u in range(ATT_UNITS)]
        row = 1 + (t - n_ctx_steps) // steps_per_batch
        _att_tile(units, xl_ref, ypl_ref, ql_ref, mod_ref, row, wop_ref, woa_ref, o_ref, ya_ref)


def _att_tile(units, x_ref, yp_ref, q_ref, mod_ref, row, wop_ref, woa_ref, o_ref, ya_ref):
    items = []
    for u, (qrows, sources) in enumerate(units):
        chunks = []
        for ks_ref, vs_ref, krange in sources:
            n_keys = krange.stop - krange.start
            kc = min(ATT_KC, n_keys)
            assert n_keys % kc == 0
            chunks += [(ks_ref, vs_ref, slice(krange.start + c * kc, krange.start + (c + 1) * kc))
                       for c in range(n_keys // kc)]
        items += [(u, qrows, hd, ch, ch is chunks[-1]) for hd in range(MLA_HEADS) for ch in chunks]

    def scores(item):
        _, qrows, hd, (ks_ref, _, keys), _ = item
        sl = slice(hd * LANES, (hd + 1) * LANES)
        s = _dot_nt(ks_ref[keys, sl], q_ref[qrows, sl])
        return s, jnp.max(s, axis=0, keepdims=True)

    pending = [scores(it) for it in items[:ATT_AHEAD]]
    state = {}
    gate = _mod(mod_ref, 2, row)
    low_half = lax.broadcasted_iota(jnp.int32, (LANES, ATT_TQ), 0) < V_DIM
    for n, (u, qrows, hd, (_, vs_ref, keys), last_chunk) in enumerate(items):
        if n + ATT_AHEAD < len(items):
            pending.append(scores(items[n + ATT_AHEAD]))
        s, cmax = pending.pop(0)
        vrows = slice((hd // 2) * LANES, (hd // 2 + 1) * LANES)
        if (u, hd) not in state:
            mx = cmax
            p = jnp.exp2(s - mx)
            den = jnp.sum(p, axis=0, keepdims=True)
            acc = _dot(vs_ref[vrows, keys], p.astype(BF16))
        else:
            mx, den, acc = state[u, hd]
            new = jnp.maximum(mx, cmax)
            alpha = jnp.exp2(mx - new)
            p = jnp.exp2(s - new)
            den = alpha * den + jnp.sum(p, axis=0, keepdims=True)
            acc = alpha * acc + _dot(vs_ref[vrows, keys], p.astype(BF16))
            mx = new
        state[u, hd] = (mx, den, acc)
        if hd % 2 == 1 and last_chunk:
            even = state[u, hd - 1][2] / state[u, hd - 1][1]
            pair = jnp.where(low_half, even, acc / den)
            ya_ref[vrows, qrows] = pair.astype(BF16)
        if hd == MLA_HEADS - 1 and last_chunk:
            y = _dot(yp_ref[qrows, :], wop_ref[...]) + _dot_tn(ya_ref[:, qrows], woa_ref[...])
            o_ref[qrows, :] = x_ref[qrows, :] + gate * y


def _att_call(ctx_in, lat_in, cache, mod_all, layer, ctx_len, lat_len, w):
    assert ctx_len == ATT_TQ
    tb = ATT_TQ * ATT_UNITS
    n_ctx, n_lat = ctx_in[0].shape[0], lat_in[0].shape[0]
    assert n_ctx % tb == 0 and lat_len % tb == 0
    n_c = n_ctx // tb
    per_batch = lat_len // tb
    past = cache[0].shape[0] // (n_lat // lat_len)
    cstep = lambda t: jnp.minimum(t, n_c - 1)
    lstep = lambda t: jnp.maximum(t - n_c, 0)
    crow = lambda width: pl.BlockSpec((tb, width), lambda t: (cstep(t), 0))
    lrow = lambda width: pl.BlockSpec((tb, width), lambda t: (lstep(t), 0))
    lbat = lambda rows: pl.BlockSpec((rows, HEAD_W), lambda t: (lstep(t) // per_batch, 0))
    lbat_t = lambda cols: pl.BlockSpec((MLA_WIDTH, cols), lambda t: (0, lstep(t) // per_batch))
    in_specs = [
        crow(D_MODEL), crow(POOL_WIDTH), crow(HEAD_W), crow(HEAD_W),
        pl.BlockSpec((MLA_WIDTH, tb), lambda t: (0, cstep(t))),
        lrow(D_MODEL), lrow(POOL_WIDTH), lrow(HEAD_W), lbat(lat_len), lbat_t(lat_len),
        lbat(past), lbat_t(past),
        _mod_spec(layer), _const_spec((POOL_WIDTH, D_MODEL)), _const_spec((MLA_WIDTH, D_MODEL)),
    ]
    return pl.pallas_call(
        functools.partial(_att_kernel, n_ctx_steps=n_c, steps_per_batch=per_batch),
        grid=(n_c + n_lat // tb,),
        in_specs=in_specs,
        out_specs=pl.BlockSpec((tb, D_MODEL), lambda t: (t, 0)),
        out_shape=jax.ShapeDtypeStruct((n_ctx + n_lat, D_MODEL), F32),
        scratch_shapes=[pltpu.VMEM((MLA_WIDTH, tb), BF16)],
        compiler_params=_params(1, 60),
        name="att_out",
    )(*ctx_in, *lat_in, *cache, mod_all, w["w_out_pool"], w["w_out_att"])


FFN_SUB = 256
FFN_GCH = D_MODEL // W_STEPS
FFN_DCH = D_FF // W_STEPS


def _ffn_kernel(x_ref, mod_ref, ng_ref, wg_ref, wu_ref, wd_ref, *rest, mod_row, split):
    outs, (wg_s, wu_s, wd_s) = rest[:-3], rest[-3:]
    t = pl.program_id(0)

    @pl.when(t < W_STEPS)
    def _():
        r = pl.multiple_of(t * FFN_GCH, FFN_GCH)
        wg_s[pl.ds(r, FFN_GCH), :] = wg_ref[0].astype(BF16)
        wu_s[pl.ds(r, FFN_GCH), :] = wu_ref[0].astype(BF16)
        r = pl.multiple_of(t * FFN_DCH, FFN_DCH)
        wd_s[pl.ds(r, FFN_DCH), :] = wd_ref[0].astype(BF16)

    def tile(o_ref):
        row = mod_row(t - W_STEPS)
        scale = ng_ref[...] * (1.0 + _mod(mod_ref, 4, row))
        shift = _mod(mod_ref, 3, row)
        gate = _mod(mod_ref, 5, row)
        for s in range(x_ref.shape[0] // FFN_SUB):
            rows = slice(s * FFN_SUB, (s + 1) * FFN_SUB)
            x = x_ref[rows, :]
            h = (_rms(x) * scale + shift).astype(BF16)
            g = _dot(h, wg_s[...])
            u = _dot(h, wu_s[...])
            d = _dot((_silu(g) * u).astype(BF16), wd_s[...])
            o_ref[rows, :] = x + gate * d

    if split is None:
        pl.when(t >= W_STEPS)(lambda: tile(outs[0]))
    else:
        pl.when((t >= W_STEPS) & (t < W_STEPS + split))(lambda: tile(outs[0]))
        pl.when(t >= W_STEPS + split)(lambda: tile(outs[1]))


def _ffn_call(x, mod_all, layer, mod_row, tm, ng, wg, wu, wd, n_first=None):
    n_tok = x.shape[0]
    n_tiles = n_tok // tm
    tok = lambda t: jnp.maximum(t - W_STEPS, 0)
    wstep = lambda t: jnp.minimum(t, W_STEPS - 1)
    if n_first is None:
        split = None
        out_specs = pl.BlockSpec((tm, D_MODEL), lambda t: (tok(t), 0))
        out_shape = jax.ShapeDtypeStruct((n_tok, D_MODEL), F32)
    else:
        split = n_first // tm
        out_specs = [
            pl.BlockSpec((tm, D_MODEL), lambda t: (jnp.minimum(tok(t), split - 1), 0)),
            pl.BlockSpec((tm, D_MODEL), lambda t: (jnp.maximum(tok(t) - split, 0), 0)),
        ]
        out_shape = [jax.ShapeDtypeStruct((n_first, D_MODEL), F32),
                     jax.ShapeDtypeStruct((n_tok - n_first, D_MODEL), F32)]
    return pl.pallas_call(
        functools.partial(_ffn_kernel, mod_row=mod_row, split=split),
        grid=(W_STEPS + n_tiles,),
        in_specs=[
            pl.BlockSpec((tm, D_MODEL), lambda t: (tok(t), 0)),
            _mod_spec(layer),
            _const_spec((1, D_MODEL)),
            pl.BlockSpec((1, FFN_GCH, D_FF), lambda t: (layer, wstep(t), 0)),
            pl.BlockSpec((1, FFN_GCH, D_FF), lambda t: (layer, wstep(t), 0)),
            pl.BlockSpec((1, FFN_DCH, D_MODEL), lambda t: (layer, wstep(t), 0)),
        ],
        out_specs=out_specs,
        out_shape=out_shape,
        scratch_shapes=[pltpu.VMEM((D_MODEL, D_FF), BF16), pltpu.VMEM((D_MODEL, D_FF), BF16),
                        pltpu.VMEM((D_FF, D_MODEL), BF16)],
        compiler_params=_params(1, 52),
        name="ffn",
    )(x, mod_all, ng, wg, wu, wd)


GM_TM = 1024
GM_SUB = 256
GM_WCH = D_MODEL // W_STEPS


def _gmlp_kernel(x_ref, mod_ref, ng_ref, win_ref, vg_ref, ws_ref, bs_ref, wout_ref, o_ref,
                 win_s, wout_s, ws_s, gs_ref, *, mod_row):
    t = pl.program_id(0)

    @pl.when(t < W_STEPS)
    def _():
        r = pl.multiple_of(t * GM_WCH, GM_WCH)
        win_s[pl.ds(r, GM_WCH), :] = win_ref[0].astype(BF16)
        wout_s[pl.ds(r, GM_WCH), :] = wout_ref[0].astype(BF16)
        ws_s[t] = ws_ref[0, 0].astype(BF16)

    @pl.when(t >= W_STEPS)
    def _():
        row = mod_row(t - W_STEPS)
        scale = ng_ref[...] * (1.0 + _mod(mod_ref, 1, row))
        shift = _mod(mod_ref, 0, row)
        gate = _mod(mod_ref, 2, row)
        n_chunks = GM_SUB // CHUNK
        n_sub = x_ref.shape[0] // GM_SUB

        def project(s):
            x = x_ref[s * GM_SUB:(s + 1) * GM_SUB, :]
            h = (_rms(x) * scale + shift).astype(BF16)
            return x, _dot(h, win_s[...])

        ahead = project(0)
        for s in range(n_sub):
            rows = slice(s * GM_SUB, (s + 1) * GM_SUB)
            x, z = ahead
            if s + 1 < n_sub:
                ahead = project(s + 1)
            vn = (_rms(z[:, C_WIDTH:]) * vg_ref[...]).astype(BF16)
            for g in range(C_GROUPS):
                cols = slice(g * C_CH, (g + 1) * C_CH)
                rhs = jnp.concatenate(
                    [vn[n * CHUNK:(n + 1) * CHUNK, cols] for n in range(n_chunks)], axis=1)
                sp = _dot(ws_s[g], rhs)
                for n in range(n_chunks):
                    crow = slice(n * CHUNK, (n + 1) * CHUNK)
                    grow = slice(s * GM_SUB + n * CHUNK, s * GM_SUB + (n + 1) * CHUNK)
                    sn = sp[:, n * C_CH:(n + 1) * C_CH] + bs_ref[g]
                    gs_ref[grow, cols] = (z[crow, cols] * sn).astype(BF16)
            o_ref[rows, :] = x + gate * _dot(gs_ref[rows, :], wout_s[...])


def _gmlp_call(x, mod_all, layer, mod_row, w, o):
    n_tok = x.shape[0]
    tok = lambda t: jnp.maximum(t - W_STEPS, 0)
    wstep = lambda t: jnp.minimum(t, W_STEPS - 1)
    assert C_GROUPS == W_STEPS
    return pl.pallas_call(
        functools.partial(_gmlp_kernel, mod_row=mod_row),
        grid=(W_STEPS + n_tok // GM_TM,),
        in_specs=[
            pl.BlockSpec((GM_TM, D_MODEL), lambda t: (tok(t), 0)),
            _mod_spec(layer),
            _const_spec((1, D_MODEL)),
            pl.BlockSpec((1, GM_WCH, 2 * C_WIDTH), lambda t: (o, wstep(t), 0)),
            _const_spec((1, C_WIDTH)),
            pl.BlockSpec((1, 1, CHUNK, CHUNK), lambda t: (o, wstep(t), 0, 0)),
            _const_spec((C_GROUPS, CHUNK, C_CH)),
            pl.BlockSpec((1, GM_WCH, D_MODEL), lambda t: (o, wstep(t), 0)),
        ],
        out_specs=pl.BlockSpec((GM_TM, D_MODEL), lambda t: (tok(t), 0)),
        out_shape=jax.ShapeDtypeStruct((n_tok, D_MODEL), F32),
        scratch_shapes=[pltpu.VMEM((D_MODEL, 2 * C_WIDTH), BF16), pltpu.VMEM((C_WIDTH, D_MODEL), BF16),
                        pltpu.VMEM((C_GROUPS, CHUNK, CHUNK), BF16), pltpu.VMEM((GM_TM, C_WIDTH), BF16)],
        compiler_params=_params(1, 36),
        name="gmlp",
    )(x, mod_all, w["norm_mix_g"], w["w_in"], w["vnorm_g"], w["w_s"], w["b_s"], w["w_out"])


def _swap_halves(t):
    return jnp.concatenate([t[..., HALF:], t[..., :HALF]], axis=-1)


def _head_ones():
    lane = np.arange(2 * LANES)
    m = (lane[:, None] // LANES == lane[None, :] // LANES) & (lane[:, None] % LANES < QK_DIM)
    return jnp.asarray(m.astype(np.float32), dtype=BF16)


def _layer0_weights(e, norm_mix_g, ab_w_in, pool_w, pool_scale, q_norm_g, kv_norm_g, w_uq, w_ukv,
                    qn_g, kn_g, ab_w_out):
    w_in = ab_w_in[e]
    o3 = POOL_WIDTH + Q_RANK + KV_RANK
    kr_cols = jnp.concatenate(
        [jnp.zeros((D_MODEL, QK_NOPE), F32), w_in[:, o3:], w_in[:, o3:]], axis=1)
    w_in_p = jnp.concatenate([w_in[:, :o3], kr_cols], axis=1).astype(BF16)
    uq = w_uq[e].reshape(Q_RANK, MLA_HEADS, QK_DIM)
    uq = jnp.concatenate([uq, _swap_halves(uq[:, :, QK_NOPE:])], axis=-1)
    score_scale = QK_DIM ** -0.5 * math.log2(math.e)
    gq = qn_g[e] * score_scale
    gk = kn_g[e]
    ukv = w_ukv[e].reshape(KV_RANK, MLA_HEADS, QK_NOPE + V_DIM)
    w_k = jnp.pad(ukv[:, :, :QK_NOPE], ((0, 0), (0, 0), (0, LANES - QK_NOPE)))
    w_v = ukv[:, :, QK_NOPE:].reshape(KV_RANK, MLA_WIDTH)
    return {
        "norm_mix_g": norm_mix_g.reshape(1, D_MODEL),
        "w_in": w_in_p,
        "q_norm_g": q_norm_g[e].reshape(1, Q_RANK),
        "kv_norm_g": kv_norm_g[e].reshape(1, KV_RANK),
        "w_uq": uq.reshape(Q_RANK, HEAD_W).astype(BF16),
        "w_k": w_k.reshape(KV_RANK, HEAD_W).astype(BF16),
        "w_vt": w_v.T.astype(BF16),
        "qn_g": jnp.pad(gq, (0, LANES - QK_DIM)).reshape(1, LANES),
        "qn_g_rope": jnp.concatenate([gq, _swap_halves(gq[QK_NOPE:])]).reshape(1, LANES),
        "kn_g": jnp.concatenate([gk, gk[QK_NOPE:]]).reshape(1, LANES),
        "head_ones": _head_ones(),
        "pool_w": pool_w[e].astype(BF16),
        "pool_scale": pool_scale[e].reshape(1, POOL_WIDTH),
        "w_out_pool": ab_w_out[e, :POOL_WIDTH].astype(BF16),
        "w_out_att": ab_w_out[e, POOL_WIDTH:].astype(BF16),
    }


def _rope_tables(seq_len):
    rows = seq_len // GRID_W
    row = np.repeat(np.arange(rows), GRID_W).astype(np.float32)
    col = np.tile(np.arange(GRID_W), rows).astype(np.float32)
    per_axis = QK_ROPE // 2
    inv = (1.0 / (np.float32(ROPE_BASE) ** (np.arange(0, per_axis, 2, dtype=np.float32) / per_axis))
           ).astype(np.float32)
    ang = np.concatenate([row[:, None] * inv, col[:, None] * inv], axis=-1)
    cos, sin = np.cos(ang).astype(np.float32), np.sin(ang).astype(np.float32)
    ones = np.ones((seq_len, QK_NOPE), np.float32)
    z_nope = np.zeros((seq_len, QK_NOPE), np.float32)
    z_half = np.zeros((seq_len, HALF), np.float32)
    c_tab = np.concatenate([ones, cos, cos, cos, cos], axis=-1)
    a_tab = np.concatenate([z_nope, -sin, z_half, -sin, z_half], axis=-1)
    b_tab = np.concatenate([z_nope, z_half, sin, z_half, sin], axis=-1)
    q_tab = np.concatenate([ones, cos, cos, -sin, sin], axis=-1)
    return jnp.asarray(np.stack([c_tab, a_tab, b_tab, q_tab]))


def kernel(x_prompt, x_sample, cache_ckv, cache_krope, c, c_ctx, ada_w, ada_b, norm_mix_g, norm_ffn_g, ffn_wg, ffn_wu, ffn_wd, ab_w_in, pool_w, pool_scale, q_norm_g, kv_norm_g, w_uq, w_ukv, qn_g, kn_g, ab_w_out, gm_w_in, gm_vnorm_g, gm_ws, gm_bs, gm_w_out):
    n_ctx_b, ctx_len, _ = x_prompt.shape
    n_lat_b, lat_len, _ = x_sample.shape
    past = cache_ckv.shape[2]
    assert ctx_len & (ctx_len - 1) == 0 and lat_len & (lat_len - 1) == 0
    assert MOD_ROWS >= 1 + n_lat_b
    assert DEPTH == 2
    n_ctx, n_lat = n_ctx_b * ctx_len, n_lat_b * lat_len

    ctx = x_prompt.reshape(n_ctx, D_MODEL)
    lat = x_sample.reshape(n_lat, D_MODEL)

    cond = jnp.concatenate(
        [c_ctx[None, :], c, jnp.zeros((MOD_ROWS - 1 - n_lat_b, D_MODEL), F32)], axis=0)
    mod_all = _ada_call(cond, ada_w, ada_b)

    def all_row(tm):
        n_ctx_tiles, per = n_ctx // tm, lat_len // tm
        return lambda j: jnp.where(j < n_ctx_tiles, 0, 1 + (j - n_ctx_tiles) // per)

    w = _layer0_weights(0, norm_mix_g[0], ab_w_in, pool_w, pool_scale, q_norm_g, kv_norm_g,
                        w_uq, w_ukv, qn_g, kn_g, ab_w_out)
    cs = _rope_tables(lat_len)
    yp_c, q_c, k_c, v_c, ckv_c, kr_c = _l0a_call(ctx, mod_all, 0, lambda t: 0, 1024, ctx_len, w, None)
    yp_l, q_l, k_l, v_l, _, _ = _l0a_call(lat, mod_all, 0, lambda t: 1 + t, lat_len, lat_len, w, cs)
    kr_c2 = cache_krope[:, 0].reshape(n_lat_b * past, QK_ROPE)
    kr_pad = jnp.concatenate([jnp.zeros((n_lat_b * past, QK_NOPE), F32), kr_c2, kr_c2], axis=1)
    cache = _kvc_call(cache_ckv[:, 0].reshape(n_lat_b * past, KV_RANK), kr_pad, w, 1024)
    xs = _att_call((ctx, yp_c, q_c, k_c, v_c), (lat, yp_l, q_l, k_l, v_l), cache,
                   mod_all, 0, ctx_len, lat_len, w)
    xs = _ffn_call(xs, mod_all, 0, all_row(1024), 1024, norm_ffn_g[0].reshape(1, D_MODEL),
                   ffn_wg, ffn_wu, ffn_wd)

    gw = {
        "norm_mix_g": norm_mix_g[1].reshape(1, D_MODEL),
        "w_in": gm_w_in,
        "vnorm_g": gm_vnorm_g[0].reshape(1, C_WIDTH),
        "w_s": gm_ws,
        "b_s": jnp.broadcast_to(gm_bs[0][:, :, None], (C_GROUPS, CHUNK, C_CH)),
        "w_out": gm_w_out,
    }
    xs = _gmlp_call(xs, mod_all, 1, all_row(GM_TM), gw, 0)
    y_ctx, y_lat = _ffn_call(xs, mod_all, 1, all_row(512), 512, norm_ffn_g[1].reshape(1, D_MODEL),
                             ffn_wg, ffn_wu, ffn_wd, n_first=n_ctx)

    state_ckv = ckv_c.reshape(n_ctx_b, 1, ctx_len, KV_RANK)
    state_krope = kr_c.reshape(n_ctx_b, 1, ctx_len, QK_ROPE)
    return (y_ctx.reshape(n_ctx_b, ctx_len, D_MODEL), y_lat.reshape(n_lat_b, lat_len, D_MODEL),
            state_ckv, state_krope)
```

```python
import functools
import math

import numpy as np
import jax
import jax.numpy as jnp
from jax import lax
from jax.experimental import pallas as pl
from jax.experimental.pallas import tpu as pltpu

D_MODEL = 1024
DEPTH = 2
GRID_W = 64
POOL_WINDOWS = (2, 4, 8, 16)
POOL_GROUPS = 4
POOL_CH = 128
POOL_WIDTH = POOL_GROUPS * POOL_CH
MLA_HEADS = 8
QK_NOPE = 64
QK_ROPE = 32
QK_DIM = QK_NOPE + QK_ROPE
HALF = QK_ROPE // 2
SQRT_QK = math.sqrt(QK_DIM)
V_DIM = 64
Q_RANK = 384
KV_RANK = 256
MLA_WIDTH = MLA_HEADS * V_DIM
CHUNK = 128
C_GROUPS = 8
C_WIDTH = D_MODEL
C_CH = C_WIDTH // C_GROUPS
D_FF = 2816
ROPE_BASE = 10000.0
EPS = 1e-6

LANES = 128
HEAD_W = MLA_HEADS * LANES
AB_IN_PAD = POOL_WIDTH + Q_RANK + KV_RANK + LANES
MOD_ROWS = 16
VMEM_LIMIT = 60 * 1024 * 1024
SMALL_CALL_VMEM = 20 * 1024 * 1024
GMLP_VMEM = 36 * 1024 * 1024
W_STEPS = 8
Q_SS_MXU_PAIRS = 2

F32 = jnp.float32
BF16 = jnp.bfloat16


def _rms(x):
    return x * lax.rsqrt(jnp.mean(x * x, axis=-1, keepdims=True) + EPS)


def _dot(a, b):
    return jnp.dot(a, b, preferred_element_type=F32)


def _dot_nt(a, b):
    return lax.dot_general(a, b, (((1,), (1,)), ((), ())), preferred_element_type=F32)


def _dot_tn(a, b):
    return lax.dot_general(a, b, (((0,), (0,)), ((), ())), preferred_element_type=F32)


def _silu(x):
    return x * jax.nn.sigmoid(x)


def _params(n_axes, vmem_bytes=VMEM_LIMIT):
    return pltpu.CompilerParams(
        dimension_semantics=("arbitrary",) * n_axes, vmem_limit_bytes=vmem_bytes)


def _const_spec(shape):
    zeros = (0,) * len(shape)
    return pl.BlockSpec(shape, lambda *_: zeros)


def _mod_spec(layer):
    return pl.BlockSpec((1, 6, MOD_ROWS, D_MODEL), lambda *_: (layer, 0, 0, 0))


def _mod(mod_ref, term, row):
    return mod_ref[0, term, pl.ds(row, 1), :]


ADA_TERMS = 2


def _ada_kernel(cond_ref, w_ref, b_ref, o_ref):
    s = _silu(cond_ref[...]).astype(BF16)
    bias = b_ref[pl.ds(pl.program_id(0), 1), :]
    y = _dot(s, w_ref[0].astype(BF16)) + bias
    for k in range(ADA_TERMS):
        o_ref[0, k] = y[:, k * D_MODEL:(k + 1) * D_MODEL]


def _ada_call(cond, ada_w, ada_b):
    width = ADA_TERMS * D_MODEL
    return pl.pallas_call(
        _ada_kernel,
        grid=(DEPTH, 6 // ADA_TERMS),
        in_specs=[
            pl.BlockSpec((MOD_ROWS, D_MODEL), lambda i, j: (0, 0)),
            pl.BlockSpec((1, D_MODEL, width), lambda i, j: (i, 0, j)),
            pl.BlockSpec((DEPTH, width), lambda i, j: (0, j)),
        ],
        out_specs=pl.BlockSpec((1, ADA_TERMS, MOD_ROWS, D_MODEL), lambda i, j: (i, j, 0, 0)),
        out_shape=jax.ShapeDtypeStruct((DEPTH, 6, MOD_ROWS, D_MODEL), F32),
        compiler_params=_params(2, SMALL_CALL_VMEM),
        name="ada_mod",
    )(cond, ada_w, ada_b)


def _rope(t, cs_ref):
    return (t * cs_ref[0]
            + pltpu.roll(t, LANES - HALF, axis=1) * cs_ref[1]
            + pltpu.roll(t, HALF, axis=1) * cs_ref[2])


def _kv_matmuls(ckv_bf, wk_ref, wv_ref, v_ref):
    v_ref[...] = _dot_nt(wv_ref[...], ckv_bf).astype(BF16)
    return _dot(ckv_bf, wk_ref[...])


def _k_heads(kpre, kr, kng_ref, cs_ref, k_ref):
    kg = kng_ref[...] * SQRT_QK
    krg = kr * kg
    if cs_ref is not None:
        krg = _rope(krg, cs_ref)
    lane = lax.broadcasted_iota(jnp.int32, kr.shape, 1)
    kr_ss = jnp.sum(jnp.where(lane < QK_DIM, kr * kr, 0.0), axis=-1, keepdims=True) + QK_DIM * EPS
    for h in range(MLA_HEADS):
        kh = kpre[:, h * LANES:(h + 1) * LANES]
        r = lax.rsqrt(jnp.sum(kh * kh, axis=-1, keepdims=True) + kr_ss)
        k_ref[:, h * LANES:(h + 1) * LANES] = ((kh * kg + krg) * r).astype(BF16)


def _shift_rows(a, k, pos, seq_len):
    n = a.shape[0]
    r = pltpu.roll(a, k % n, axis=0)
    src = pos - k
    ok = (src >= 0) if k > 0 else (src < seq_len)
    return jnp.where(ok, r, 0.0)


def _pool_group(ug, half, pos, seq_len):
    fw = ug
    step = 1
    while step < half:
        fw = fw + _shift_rows(fw, -step, pos, seq_len)
        step *= 2
    bk = _shift_rows(ug, 1, pos, seq_len)
    step = 1
    while step < half:
        bk = bk + _shift_rows(bk, step, pos, seq_len)
        step *= 2
    cnt = jnp.minimum(pos + half, seq_len) - jnp.maximum(pos - half, 0)
    return (fw + bk) / cnt.astype(F32) - ug


def _l0a_kernel(*refs, seq_len, use_rope, mod_row):
    (x_ref, mod_ref, ng_ref, win_ref, qg_ref, kvg_ref, wuq_ref, wk_ref, wv_ref,
     qng_ref, kng_ref, pw_ref, ps_ref, ones_ref) = refs[:14]
    if use_rope:
        cs_ref = refs[14]
        outs = refs[15:]
    else:
        cs_ref = None
        outs = refs[14:]
    yp_ref, q_ref, k_ref, v_ref, ckv_ref, kr_ref = outs

    tm = x_ref.shape[0]
    row = mod_row(pl.program_id(0))
    scale = ng_ref[...] * (1.0 + _mod(mod_ref, 1, row))
    h = _rms(x_ref[...]) * scale + _mod(mod_ref, 0, row)
    z = _dot(h.astype(BF16), win_ref[...])
    o1, o2, o3 = POOL_WIDTH, POOL_WIDTH + Q_RANK, POOL_WIDTH + Q_RANK + KV_RANK

    ckv = _rms(z[:, o2:o3]) * kvg_ref[...]
    kr = z[:, o3:]
    ckv_ref[...] = ckv
    kr_ref[...] = kr[:, QK_NOPE:QK_DIM]
    cqn = (_rms(z[:, o1:o2]) * qg_ref[...]).astype(BF16)
    kpre = _kv_matmuls(ckv.astype(BF16), wk_ref, wv_ref, v_ref)
    qf = _dot(cqn, wuq_ref[...])

    qt = qng_ref[...] * SQRT_QK
    if use_rope:
        qt = qt * cs_ref[3]
    lane2 = lax.broadcasted_iota(jnp.int32, (tm, 2 * LANES), 1)
    for j in range(MLA_HEADS // 2):
        qp = qf[:, 2 * j * LANES:(2 * j + 2) * LANES]
        sq = qp * qp
        if j < Q_SS_MXU_PAIRS:
            hi = sq.astype(BF16)
            lo = (sq - hi.astype(F32)).astype(BF16)
            ss = _dot(hi, ones_ref[...]) + _dot(lo, ones_ref[...])
            qn = qp * lax.rsqrt(ss + QK_DIM * EPS)
            parts = [qn[:, :LANES], qn[:, LANES:]]
        else:
            sq = jnp.where((lane2 & (LANES - 1)) < QK_DIM, sq, 0.0)
            parts = []
            for hh in range(2):
                ss = jnp.sum(sq[:, hh * LANES:(hh + 1) * LANES], axis=-1, keepdims=True)
                parts.append(qp[:, hh * LANES:(hh + 1) * LANES] * lax.rsqrt(ss + QK_DIM * EPS))
        for hh in range(2):
            hd = 2 * j + hh
            q_ref[:, hd * LANES:(hd + 1) * LANES] = (parts[hh] * qt).astype(BF16)

    _k_heads(kpre, kr, kng_ref, cs_ref, k_ref)

    pos = lax.broadcasted_iota(jnp.int32, (tm, POOL_CH), 0) & (seq_len - 1)
    for g, w in enumerate(POOL_WINDOWS):
        ug = z[:, g * POOL_CH:(g + 1) * POOL_CH]
        p = _pool_group(ug, w // 2, pos, seq_len)
        y = _dot(p.astype(BF16), pw_ref[g]) * ps_ref[:, g * POOL_CH:(g + 1) * POOL_CH]
        yp_ref[:, g * POOL_CH:(g + 1) * POOL_CH] = y.astype(BF16)


def _l0a_call(x, mod_all, layer, mod_row, tm, seq_len, w, cs):
    n_tok = x.shape[0]
    use_rope = cs is not None
    row_spec = lambda width: pl.BlockSpec((tm, width), lambda t: (t, 0))
    in_specs = [
        row_spec(D_MODEL),
        _mod_spec(layer),
        _const_spec((1, D_MODEL)),
        _const_spec((D_MODEL, AB_IN_PAD)),
        _const_spec((1, Q_RANK)),
        _const_spec((1, KV_RANK)),
        _const_spec((Q_RANK, HEAD_W)),
        _const_spec((KV_RANK, HEAD_W)),
        _const_spec((MLA_WIDTH, KV_RANK)),
        _const_spec((1, LANES)),
        _const_spec((1, LANES)),
        _const_spec((POOL_GROUPS, POOL_CH, POOL_CH)),
        _const_spec((1, POOL_WIDTH)),
        _const_spec((2 * LANES, 2 * LANES)),
    ]
    args = [x, mod_all, w["norm_mix_g"], w["w_in"], w["q_norm_g"], w["kv_norm_g"], w["w_uq"],
            w["w_k"], w["w_vt"], w["qn_g_rope"] if use_rope else w["qn_g"], w["kn_g"],
            w["pool_w"], w["pool_scale"], w["head_ones"]]
    if use_rope:
        assert tm == seq_len
        in_specs.append(_const_spec((4, seq_len, LANES)))
        args.append(cs)
    out_shape = [
        jax.ShapeDtypeStruct((n_tok, POOL_WIDTH), BF16),
        jax.ShapeDtypeStruct((n_tok, HEAD_W), BF16),
        jax.ShapeDtypeStruct((n_tok, HEAD_W), BF16),
        jax.ShapeDtypeStruct((MLA_WIDTH, n_tok), BF16),
        jax.ShapeDtypeStruct((n_tok, KV_RANK), F32),
        jax.ShapeDtypeStruct((n_tok, QK_ROPE), F32),
    ]
    out_specs = [row_spec(POOL_WIDTH), row_spec(HEAD_W), row_spec(HEAD_W),
                 pl.BlockSpec((MLA_WIDTH, tm), lambda t: (0, t)),
                 row_spec(KV_RANK), row_spec(QK_ROPE)]
    return pl.pallas_call(
        functools.partial(_l0a_kernel, seq_len=seq_len, use_rope=use_rope, mod_row=mod_row),
        grid=(n_tok // tm,),
        in_specs=in_specs,
        out_specs=out_specs,
        out_shape=out_shape,
        compiler_params=_params(1),
        name="l0_front_rope" if use_rope else "l0_front",
    )(*args)


def _kvc_kernel(ckv_ref, kr_ref, wk_ref, wv_ref, kng_ref, k_ref, v_ref):
    kpre = _kv_matmuls(ckv_ref[...].astype(BF16), wk_ref, wv_ref, v_ref)
    _k_heads(kpre, kr_ref[...], kng_ref, None, k_ref)


def _kvc_call(ckv, kr_pad, w, tm):
    n_tok = ckv.shape[0]
    row_spec = lambda width: pl.BlockSpec((tm, width), lambda t: (t, 0))
    return pl.pallas_call(
        _kvc_kernel,
        grid=(n_tok // tm,),
        in_specs=[row_spec(KV_RANK), row_spec(LANES), _const_spec((KV_RANK, HEAD_W)),
                  _const_spec((MLA_WIDTH, KV_RANK)), _const_spec((1, LANES))],
        out_specs=[row_spec(HEAD_W), pl.BlockSpec((MLA_WIDTH, tm), lambda t: (0, t))],
        out_shape=[jax.ShapeDtypeStruct((n_tok, HEAD_W), BF16),
                   jax.ShapeDtypeStruct((MLA_WIDTH, n_tok), BF16)],
        compiler_params=_params(1, SMALL_CALL_VMEM),
        name="cache_kv",
    )(ckv, kr_pad, w["w_k"], w["w_vt"], w["kn_g"])


ATT_TQ = 256
ATT_UNITS = 4
ATT_KC = 512
ATT_AHEAD = 8


def _att_kernel(xc_ref, ypc_ref, qc_ref, kc_ref, vc_ref,
                xl_ref, ypl_ref, ql_ref, kl_ref, vl_ref, kp_ref, vp_ref,
                mod_ref, wop_ref, woa_ref, o_ref, ya_ref, *, n_ctx_steps, steps_per_batch):
    t = pl.program_id(0)

    def rows(u):
        return slice(u * ATT_TQ, (u + 1) * ATT_TQ)

    @pl.when(t < n_ctx_steps)
    def _():
        units = [(rows(u), [(kc_ref, vc_ref, rows(u))]) for u in range(ATT_UNITS)]
        _att_tile(units, xc_ref, ypc_ref, qc_ref, mod_ref, 0, wop_ref, woa_ref, o_ref, ya_ref)

    @pl.when(t >= n_ctx_steps)
    def _():
        srcs = [(kp_ref, vp_ref, slice(0, kp_ref.shape[0])), (kl_ref, vl_ref, slice(0, kl_ref.shape[0]))]
        units = [(rows(u), srcs) for u in range(ATT_UNITS)]
        row = 1 + (t - n_ctx_steps) // steps_per_batch
        _att_tile(units, xl_ref, ypl_ref, ql_ref, mod_ref, row, wop_ref, woa_ref, o_ref, ya_ref)


def _att_tile(units, x_ref, yp_ref, q_ref, mod_ref, row, wop_ref, woa_ref, o_ref, ya_ref):
    items = []
    for u, (qrows, sources) in enumerate(units):
        chunks = []
        for ks_ref, vs_ref, krange in sources:
            n_keys = krange.stop - krange.start
            kc = min(ATT_KC, n_keys)
            assert n_keys % kc == 0
            chunks += [(ks_ref, vs_ref, slice(krange.start + c * kc, krange.start + (c + 1) * kc))
                       for c in range(n_keys // kc)]
        items += [(u, qrows, hd, ch, ch is chunks[-1]) for hd in range(MLA_HEADS) for ch in chunks]

    def scores(item):
        _, qrows, hd, (ks_ref, _, keys), _ = item
        sl = slice(hd * LANES, (hd + 1) * LANES)
        s = _dot_nt(ks_ref[keys, sl], q_ref[qrows, sl])
        return s, jnp.max(s, axis=0, keepdims=True)

    pending = [scores(it) for it in items[:ATT_AHEAD]]
    state = {}
    gate = _mod(mod_ref, 2, row)
    low_half = lax.broadcasted_iota(jnp.int32, (LANES, ATT_TQ), 0) < V_DIM
    for n, (u, qrows, hd, (_, vs_ref, keys), last_chunk) in enumerate(items):
        if n + ATT_AHEAD < len(items):
            pending.append(scores(items[n + ATT_AHEAD]))
        s, cmax = pending.pop(0)
        vrows = slice((hd // 2) * LANES, (hd // 2 + 1) * LANES)
        if (u, hd) not in state:
            mx = cmax
            p = jnp.exp2(s - mx)
            den = jnp.sum(p, axis=0, keepdims=True)
            acc = _dot(vs_ref[vrows, keys], p.astype(BF16))
        else:
            mx, den, acc = state[u, hd]
            new = jnp.maximum(mx, cmax)
            alpha = jnp.exp2(mx - new)
            p = jnp.exp2(s - new)
            den = alpha * den + jnp.sum(p, axis=0, keepdims=True)
            acc = alpha * acc + _dot(vs_ref[vrows, keys], p.astype(BF16))
            mx = new
        state[u, hd] = (mx, den, acc)
        if hd % 2 == 1 and last_chunk:
            even = state[u, hd - 1][2] / state[u, hd - 1][1]
            pair = jnp.where(low_half, even, acc / den)
            ya_ref[vrows, qrows] = pair.astype(BF16)
        if hd == MLA_HEADS - 1 and last_chunk:
            y = _dot(yp_ref[qrows, :], wop_ref[...]) + _dot_tn(ya_ref[:, qrows], woa_ref[...])
            o_ref[qrows, :] = x_ref[qrows, :] + gate * y


def _att_call(ctx_in, lat_in, cache, mod_all, layer, ctx_len, lat_len, w):
    assert ctx_len == ATT_TQ
    tb = ATT_TQ * ATT_UNITS
    n_ctx, n_lat = ctx_in[0].shape[0], lat_in[0].shape[0]
    assert n_ctx % tb == 0 and lat_len % tb == 0
    n_c = n_ctx // tb
    per_batch = lat_len // tb
    past = cache[0].shape[0] // (n_lat // lat_len)
    cstep = lambda t: jnp.minimum(t, n_c - 1)
    lstep = lambda t: jnp.maximum(t - n_c, 0)
    crow = lambda width: pl.BlockSpec((tb, width), lambda t: (cstep(t), 0))
    lrow = lambda width: pl.BlockSpec((tb, width), lambda t: (lstep(t), 0))
    lbat = lambda rows: pl.BlockSpec((rows, HEAD_W), lambda t: (lstep(t) // per_batch, 0))
    lbat_t = lambda cols: pl.BlockSpec((MLA_WIDTH, cols), lambda t: (0, lstep(t) // per_batch))
    in_specs = [
        crow(D_MODEL), crow(POOL_WIDTH), crow(HEAD_W), crow(HEAD_W),
        pl.BlockSpec((MLA_WIDTH, tb), lambda t: (0, cstep(t))),
        lrow(D_MODEL), lrow(POOL_WIDTH), lrow(HEAD_W), lbat(lat_len), lbat_t(lat_len),
        lbat(past), lbat_t(past),
        _mod_spec(layer), _const_spec((POOL_WIDTH, D_MODEL)), _const_spec((MLA_WIDTH, D_MODEL)),
    ]
    return pl.pallas_call(
        functools.partial(_att_kernel, n_ctx_steps=n_c, steps_per_batch=per_batch),
        grid=(n_c + n_lat // tb,),
        in_specs=in_specs,
        out_specs=pl.BlockSpec((tb, D_MODEL), lambda t: (t, 0)),
        out_shape=jax.ShapeDtypeStruct((n_ctx + n_lat, D_MODEL), F32),
        scratch_shapes=[pltpu.VMEM((MLA_WIDTH, tb), BF16)],
        compiler_params=_params(1),
        name="att_out",
    )(*ctx_in, *lat_in, *cache, mod_all, w["w_out_pool"], w["w_out_att"])


FFN_SUB = 256
FFN_GCH = D_MODEL // W_STEPS
FFN_DCH = D_FF // W_STEPS


def _ffn_kernel(x_ref, mod_ref, ng_ref, wg_ref, wu_ref, wd_ref, *rest, mod_row, split):
    outs, (wg_s, wu_s, wd_s) = rest[:-3], rest[-3:]
    t = pl.program_id(0)

    @pl.when(t < W_STEPS)
    def _():
        r = pl.multiple_of(t * FFN_GCH, FFN_GCH)
        wg_s[pl.ds(r, FFN_GCH), :] = wg_ref[0].astype(BF16)
        wu_s[pl.ds(r, FFN_GCH), :] = wu_ref[0].astype(BF16)
        r = pl.multiple_of(t * FFN_DCH, FFN_DCH)
        wd_s[pl.ds(r, FFN_DCH), :] = wd_ref[0].astype(BF16)

    def tile(o_ref):
        row = mod_row(t - W_STEPS)
        scale = ng_ref[...] * (1.0 + _mod(mod_ref, 4, row))
        shift = _mod(mod_ref, 3, row)
        gate = _mod(mod_ref, 5, row)
        for s in range(x_ref.shape[0] // FFN_SUB):
            rows = slice(s * FFN_SUB, (s + 1) * FFN_SUB)
            x = x_ref[rows, :]
            h = (_rms(x) * scale + shift).astype(BF16)
            g = _dot(h, wg_s[...])
            u = _dot(h, wu_s[...])
            d = _dot((_silu(g) * u).astype(BF16), wd_s[...])
            o_ref[rows, :] = x + gate * d

    if split is None:
        pl.when(t >= W_STEPS)(lambda: tile(outs[0]))
    else:
        pl.when((t >= W_STEPS) & (t < W_STEPS + split))(lambda: tile(outs[0]))
        pl.when(t >= W_STEPS + split)(lambda: tile(outs[1]))


def _ffn_call(x, mod_all, layer, mod_row, tm, ng, wg, wu, wd, n_first=None):
    n_tok = x.shape[0]
    n_tiles = n_tok // tm
    tok = lambda t: jnp.maximum(t - W_STEPS, 0)
    wstep = lambda t: jnp.minimum(t, W_STEPS - 1)
    if n_first is None:
        split = None
        out_specs = pl.BlockSpec((tm, D_MODEL), lambda t: (tok(t), 0))
        out_shape = jax.ShapeDtypeStruct((n_tok, D_MODEL), F32)
    else:
        split = n_first // tm
        out_specs = [
            pl.BlockSpec((tm, D_MODEL), lambda t: (jnp.minimum(tok(t), split - 1), 0)),
            pl.BlockSpec((tm, D_MODEL), lambda t: (jnp.maximum(tok(t) - split, 0), 0)),
        ]
        out_shape = [jax.ShapeDtypeStruct((n_first, D_MODEL), F32),
                     jax.ShapeDtypeStruct((n_tok - n_first, D_MODEL), F32)]
    return pl.pallas_call(
        functools.partial(_ffn_kernel, mod_row=mod_row, split=split),
        grid=(W_STEPS + n_tiles,),
        in_specs=[
            pl.BlockSpec((tm, D_MODEL), lambda t: (tok(t), 0)),
            _mod_spec(layer),
            _const_spec((1, D_MODEL)),
            pl.BlockSpec((1, FFN_GCH, D_FF), lambda t: (layer, wstep(t), 0)),
            pl.BlockSpec((1, FFN_GCH, D_FF), lambda t: (layer, wstep(t), 0)),
            pl.BlockSpec((1, FFN_DCH, D_MODEL), lambda t: (layer, wstep(t), 0)),
        ],
        out_specs=out_specs,
        out_shape=out_shape,
        scratch_shapes=[pltpu.VMEM((D_MODEL, D_FF), BF16), pltpu.VMEM((D_MODEL, D_FF), BF16),
                        pltpu.VMEM((D_FF, D_MODEL), BF16)],
        compiler_params=_params(1),
        name="ffn",
    )(x, mod_all, ng, wg, wu, wd)


GM_TM = 1024
GM_SUB = 256
GM_WCH = D_MODEL // W_STEPS


def _gmlp_kernel(x_ref, mod_ref, ng_ref, win_ref, vg_ref, ws_ref, bs_ref, wout_ref, o_ref,
                 win_s, wout_s, ws_s, gs_ref, *, mod_row):
    t = pl.program_id(0)

    @pl.when(t < W_STEPS)
    def _():
        r = pl.multiple_of(t * GM_WCH, GM_WCH)
        win_s[pl.ds(r, GM_WCH), :] = win_ref[0].astype(BF16)
        wout_s[pl.ds(r, GM_WCH), :] = wout_ref[0].astype(BF16)
        ws_s[t] = ws_ref[0, 0].astype(BF16)

    @pl.when(t >= W_STEPS)
    def _():
        row = mod_row(t - W_STEPS)
        scale = ng_ref[...] * (1.0 + _mod(mod_ref, 1, row))
        shift = _mod(mod_ref, 0, row)
        gate = _mod(mod_ref, 2, row)
        n_chunks = GM_SUB // CHUNK
        n_sub = x_ref.shape[0] // GM_SUB

        def project(s):
            x = x_ref[s * GM_SUB:(s + 1) * GM_SUB, :]
            h = (_rms(x) * scale + shift).astype(BF16)
            return x, _dot(h, win_s[...])

        ahead = project(0)
        for s in range(n_sub):
            rows = slice(s * GM_SUB, (s + 1) * GM_SUB)
            x, z = ahead
            if s + 1 < n_sub:
                ahead = project(s + 1)
            vn = (_rms(z[:, C_WIDTH:]) * vg_ref[...]).astype(BF16)
            for g in range(C_GROUPS):
                cols = slice(g * C_CH, (g + 1) * C_CH)
                rhs = jnp.concatenate(
                    [vn[n * CHUNK:(n + 1) * CHUNK, cols] for n in range(n_chunks)], axis=1)
                sp = _dot(ws_s[g], rhs)
                for n in range(n_chunks):
                    crow = slice(n * CHUNK, (n + 1) * CHUNK)
                    grow = slice(s * GM_SUB + n * CHUNK, s * GM_SUB + (n + 1) * CHUNK)
                    sn = sp[:, n * C_CH:(n + 1) * C_CH] + bs_ref[g]
                    gs_ref[grow, cols] = (z[crow, cols] * sn).astype(BF16)
            o_ref[rows, :] = x + gate * _dot(gs_ref[rows, :], wout_s[...])


def _gmlp_call(x, mod_all, layer, mod_row, w, o):
    n_tok = x.shape[0]
    tok = lambda t: jnp.maximum(t - W_STEPS, 0)
    wstep = lambda t: jnp.minimum(t, W_STEPS - 1)
    assert C_GROUPS == W_STEPS
    return pl.pallas_call(
        functools.partial(_gmlp_kernel, mod_row=mod_row),
        grid=(W_STEPS + n_tok // GM_TM,),
        in_specs=[
            pl.BlockSpec((GM_TM, D_MODEL), lambda t: (tok(t), 0)),
            _mod_spec(layer),
            _const_spec((1, D_MODEL)),
            pl.BlockSpec((1, GM_WCH, 2 * C_WIDTH), lambda t: (o, wstep(t), 0)),
            _const_spec((1, C_WIDTH)),
            pl.BlockSpec((1, 1, CHUNK, CHUNK), lambda t: (o, wstep(t), 0, 0)),
            _const_spec((C_GROUPS, CHUNK, C_CH)),
            pl.BlockSpec((1, GM_WCH, D_MODEL), lambda t: (o, wstep(t), 0)),
        ],
        out_specs=pl.BlockSpec((GM_TM, D_MODEL), lambda t: (tok(t), 0)),
        out_shape=jax.ShapeDtypeStruct((n_tok, D_MODEL), F32),
        scratch_shapes=[pltpu.VMEM((D_MODEL, 2 * C_WIDTH), BF16), pltpu.VMEM((C_WIDTH, D_MODEL), BF16),
                        pltpu.VMEM((C_GROUPS, CHUNK, CHUNK), BF16), pltpu.VMEM((GM_TM, C_WIDTH), BF16)],
        compiler_params=_params(1, GMLP_VMEM),
        name="gmlp",
    )(x, mod_all, w["norm_mix_g"], w["w_in"], w["vnorm_g"], w["w_s"], w["b_s"], w["w_out"])


def _swap_halves(t):
    return jnp.concatenate([t[..., HALF:], t[..., :HALF]], axis=-1)


def _head_ones():
    lane = np.arange(2 * LANES)
    m = (lane[:, None] // LANES == lane[None, :] // LANES) & (lane[:, None] % LANES < QK_DIM)
    return jnp.asarray(m.astype(np.float32), dtype=BF16)


def _layer0_weights(e, norm_mix_g, ab_w_in, pool_w, pool_scale, q_norm_g, kv_norm_g, w_uq, w_ukv,
                    qn_g, kn_g, ab_w_out):
    w_in = ab_w_in[e]
    o3 = POOL_WIDTH + Q_RANK + KV_RANK
    kr_cols = jnp.concatenate(
        [jnp.zeros((D_MODEL, QK_NOPE), F32), w_in[:, o3:], w_in[:, o3:]], axis=1)
    w_in_p = jnp.concatenate([w_in[:, :o3], kr_cols], axis=1).astype(BF16)
    uq = w_uq[e].reshape(Q_RANK, MLA_HEADS, QK_DIM)
    uq = jnp.concatenate([uq, _swap_halves(uq[:, :, QK_NOPE:])], axis=-1)
    score_scale = QK_DIM ** -0.5 * math.log2(math.e)
    gq = qn_g[e] * score_scale
    gk = kn_g[e]
    ukv = w_ukv[e].reshape(KV_RANK, MLA_HEADS, QK_NOPE + V_DIM)
    w_k = jnp.pad(ukv[:, :, :QK_NOPE], ((0, 0), (0, 0), (0, LANES - QK_NOPE)))
    w_v = ukv[:, :, QK_NOPE:].reshape(KV_RANK, MLA_WIDTH)
    return {
        "norm_mix_g": norm_mix_g.reshape(1, D_MODEL),
        "w_in": w_in_p,
        "q_norm_g": q_norm_g[e].reshape(1, Q_RANK),
        "kv_norm_g": kv_norm_g[e].reshape(1, KV_RANK),
        "w_uq": uq.reshape(Q_RANK, HEAD_W).astype(BF16),
        "w_k": w_k.reshape(KV_RANK, HEAD_W).astype(BF16),
        "w_vt": w_v.T.astype(BF16),
        "qn_g": jnp.pad(gq, (0, LANES - QK_DIM)).reshape(1, LANES),
        "qn_g_rope": jnp.concatenate([gq, _swap_halves(gq[QK_NOPE:])]).reshape(1, LANES),
        "kn_g": jnp.concatenate([gk, gk[QK_NOPE:]]).reshape(1, LANES),
        "head_ones": _head_ones(),
        "pool_w": pool_w[e].astype(BF16),
        "pool_scale": pool_scale[e].reshape(1, POOL_WIDTH),
        "w_out_pool": ab_w_out[e, :POOL_WIDTH].astype(BF16),
        "w_out_att": ab_w_out[e, POOL_WIDTH:].astype(BF16),
    }


def _rope_tables(seq_len):
    rows = seq_len // GRID_W
    row = np.repeat(np.arange(rows), GRID_W).astype(np.float32)
    col = np.tile(np.arange(GRID_W), rows).astype(np.float32)
    per_axis = QK_ROPE // 2
    inv = (1.0 / (np.float32(ROPE_BASE) ** (np.arange(0, per_axis, 2, dtype=np.float32) / per_axis))
           ).astype(np.float32)
    ang = np.concatenate([row[:, None] * inv, col[:, None] * inv], axis=-1)
    cos, sin = np.cos(ang).astype(np.float32), np.sin(ang).astype(np.float32)
    ones = np.ones((seq_len, QK_NOPE), np.float32)
    z_nope = np.zeros((seq_len, QK_NOPE), np.float32)
    z_half = np.zeros((seq_len, HALF), np.float32)
    c_tab = np.concatenate([ones, cos, cos, cos, cos], axis=-1)
    a_tab = np.concatenate([z_nope, -sin, z_half, -sin, z_half], axis=-1)
    b_tab = np.concatenate([z_nope, z_half, sin, z_half, sin], axis=-1)
    q_tab = np.concatenate([ones, cos, cos, -sin, sin], axis=-1)
    return jnp.asarray(np.stack([c_tab, a_tab, b_tab, q_tab]))


def kernel(x_prompt, x_sample, cache_ckv, cache_krope, c, c_ctx, ada_w, ada_b, norm_mix_g, norm_ffn_g, ffn_wg, ffn_wu, ffn_wd, ab_w_in, pool_w, pool_scale, q_norm_g, kv_norm_g, w_uq, w_ukv, qn_g, kn_g, ab_w_out, gm_w_in, gm_vnorm_g, gm_ws, gm_bs, gm_w_out):
    n_ctx_b, ctx_len, _ = x_prompt.shape
    n_lat_b, lat_len, _ = x_sample.shape
    past = cache_ckv.shape[2]
    assert ctx_len & (ctx_len - 1) == 0 and lat_len & (lat_len - 1) == 0
    assert MOD_ROWS >= 1 + n_lat_b
    assert DEPTH == 2
    n_ctx, n_lat = n_ctx_b * ctx_len, n_lat_b * lat_len

    ctx = x_prompt.reshape(n_ctx, D_MODEL)
    lat = x_sample.reshape(n_lat, D_MODEL)

    cond = jnp.concatenate(
        [c_ctx[None, :], c, jnp.zeros((MOD_ROWS - 1 - n_lat_b, D_MODEL), F32)], axis=0)
    mod_all = _ada_call(cond, ada_w, ada_b)

    def all_row(tm):
        n_ctx_tiles, per = n_ctx // tm, lat_len // tm
        return lambda j: jnp.where(j < n_ctx_tiles, 0, 1 + (j - n_ctx_tiles) // per)

    w = _layer0_weights(0, norm_mix_g[0], ab_w_in, pool_w, pool_scale, q_norm_g, kv_norm_g,
                        w_uq, w_ukv, qn_g, kn_g, ab_w_out)
    cs = _rope_tables(lat_len)
    yp_c, q_c, k_c, v_c, ckv_c, kr_c = _l0a_call(ctx, mod_all, 0, lambda t: 0, 1024, ctx_len, w, None)
    yp_l, q_l, k_l, v_l, _, _ = _l0a_call(lat, mod_all, 0, lambda t: 1 + t, lat_len, lat_len, w, cs)
    kr_c2 = cache_krope[:, 0].reshape(n_lat_b * past, QK_ROPE)
    kr_pad = jnp.concatenate([jnp.zeros((n_lat_b * past, QK_NOPE), F32), kr_c2, kr_c2], axis=1)
    cache = _kvc_call(cache_ckv[:, 0].reshape(n_lat_b * past, KV_RANK), kr_pad, w, 1024)
    xs = _att_call((ctx, yp_c, q_c, k_c, v_c), (lat, yp_l, q_l, k_l, v_l), cache,
                   mod_all, 0, ctx_len, lat_len, w)
    xs = _ffn_call(xs, mod_all, 0, all_row(1024), 1024, norm_ffn_g[0].reshape(1, D_MODEL),
                   ffn_wg, ffn_wu, ffn_wd)

    gw = {
        "norm_mix_g": norm_mix_g[1].reshape(1, D_MODEL),
        "w_in": gm_w_in,
        "vnorm_g": gm_vnorm_g[0].reshape(1, C_WIDTH),
        "w_s": gm_ws,
        "b_s": jnp.broadcast_to(gm_bs[0][:, :, None], (C_GROUPS, CHUNK, C_CH)),
        "w_out": gm_w_out,
    }
    xs = _gmlp_call(xs, mod_all, 1, all_row(GM_TM), gw, 0)
    y_ctx, y_lat = _ffn_call(xs, mod_all, 1, all_row(512), 512, norm_ffn_g[1].reshape(1, D_MODEL),
                             ffn_wg, ffn_wu, ffn_wd, n_first=n_ctx)

    state_ckv = ckv_c.reshape(n_ctx_b, 1, ctx_len, KV_RANK)
    state_krope = kr_c.reshape(n_ctx_b, 1, ctx_len, QK_ROPE)
    return (y_ctx.reshape(n_ctx_b, ctx_len, D_MODEL), y_lat.reshape(n_lat_b, lat_len, D_MODEL),
            state_ckv, state_krope)
```
